```python
import jax, jax.numpy as jnp
from jax import lax
import numpy as np

D_MODEL = 2048
BATCH = 4
SEQ = 4096
DEPTH = 1

GRID_W = 64
CTX_LEN = 256
NORM_EPS = 1e-6

HEAD_WIDTH = 128
MLA_HEADS = D_MODEL // (2 * HEAD_WIDTH)
MLA_Q_RANK = D_MODEL // 4
MLA_KV_RANK = D_MODEL // 8
MLA_NOPE = 128
MLA_ROPE = 64
MLA_V = HEAD_WIDTH
MLA_QK = MLA_NOPE + MLA_ROPE
MLA_SCALE = MLA_QK ** -0.5
Q_BLOCK = 128
ROPE_THETA = 10000.0
HG_HEADS = D_MODEL // (2 * HEAD_WIDTH)
HG_DK = 128
HG_DV = HEAD_WIDTH
HG_CHUNK = 64

MIX_WIDTH = MLA_HEADS * MLA_V + HG_HEADS * HG_DV
IN_SPLITS = (MLA_Q_RANK, MLA_KV_RANK, MLA_ROPE, HG_HEADS * HG_DK, HG_HEADS * HG_DK, HG_HEADS * HG_DK, HG_HEADS * HG_DV, HG_HEADS * HG_DV)
IN_COLS = MLA_Q_RANK + MLA_KV_RANK + MLA_ROPE + 3 * HG_HEADS * HG_DK + 2 * HG_HEADS * HG_DV

N_EXPERTS = 32
TOP_K = 4
D_FF = D_MODEL
SWIGLU_LIMIT = 7.0
SWIGLU_ALPHA = 1.702
MOE_BLOCK = 128

kernel_name = "hybrid_mla_hgrn2_moe_dit_block"


def rms_norm(x, w):
    xf = x.astype(jnp.float32)
    y = xf * lax.rsqrt(jnp.mean(xf * xf, axis=-1, keepdims=True) + NORM_EPS)
    return (y * w.astype(jnp.float32)).astype(x.dtype)


def modulate(x, shift, scale):
    return x * (1.0 + scale) + shift


def split_cols(p):
    out, start = [], 0
    for size in IN_SPLITS:
        out.append(p[..., start:start + size])
        start += size
    return out


def axial_rope(rows):
    r, cl = jnp.meshgrid(jnp.arange(rows), jnp.arange(GRID_W), indexing="ij")
    r = r.reshape(-1).astype(jnp.float32)
    cl = cl.reshape(-1).astype(jnp.float32)
    n_freq = MLA_ROPE // 4
    inv = ROPE_THETA ** (-jnp.arange(n_freq, dtype=jnp.float32) / n_freq)
    ang = jnp.concatenate([r[:, None] * inv, cl[:, None] * inv], axis=-1)
    return jnp.cos(ang), jnp.sin(ang)


def apply_rope(x, cos, sin):
    xf = x.astype(jnp.float32)
    x1, x2 = xf[..., :MLA_ROPE // 2], xf[..., MLA_ROPE // 2:]
    return jnp.concatenate([x1 * cos - x2 * sin, x1 * sin + x2 * cos], axis=-1).astype(x.dtype)


def mla_keys(ckv_raw, kr_raw, kv_norm, w_ukv, cos, sin):
    ckv = rms_norm(ckv_raw, kv_norm)
    kv = jnp.einsum("bnr,rhe->bnhe", ckv, w_ukv)
    k_nope, v = kv[..., :MLA_NOPE], kv[..., MLA_NOPE:]
    k_rope = kr_raw if cos is None else apply_rope(kr_raw, cos, sin)
    return k_nope, k_rope, v


def mla_queries(cq_raw, q_norm, w_uq, cos, sin):
    cq = rms_norm(cq_raw, q_norm)
    q = jnp.einsum("bnr,rhe->bnhe", cq, w_uq)
    q_nope, q_rope = q[..., :MLA_NOPE], q[..., MLA_NOPE:]
    if cos is not None:
        q_rope = apply_rope(q_rope, cos[:, None, :], sin[:, None, :])
    return q_nope, q_rope


def block_attn(q_nope, q_rope, k_nope, k_rope, v):
    s = (jnp.einsum("bqhe,bkhe->bhqk", q_nope, k_nope)
         + jnp.einsum("bqhe,bke->bhqk", q_rope, k_rope)).astype(jnp.float32) * MLA_SCALE
    p = jax.nn.softmax(s, axis=-1).astype(v.dtype)
    return jnp.einsum("bhqk,bkhe->bqhe", p, v)


def latent_attention(q_nope, q_rope, k_nope, k_rope, v):
    B, N = q_nope.shape[:2]
    nb = N // Q_BLOCK
    qn = q_nope.reshape(B, nb, Q_BLOCK, MLA_HEADS, MLA_NOPE).swapaxes(0, 1)
    qr = q_rope.reshape(B, nb, Q_BLOCK, MLA_HEADS, MLA_ROPE).swapaxes(0, 1)
    out = lax.map(lambda qs: block_attn(qs[0], qs[1], k_nope, k_rope, v), (qn, qr))
    return out.swapaxes(0, 1).reshape(B, N, MLA_HEADS * MLA_V)


def hgrn_chunk_scan(q, k, v, g, s0):
    B, N, H, DK = q.shape
    DV = v.shape[-1]
    nc = N // HG_CHUNK

    def to_chunks(a):
        return a.reshape(B, nc, HG_CHUNK, H, a.shape[-1]).transpose(1, 0, 3, 2, 4)

    lower = jnp.tril(jnp.ones((HG_CHUNK, HG_CHUNK), dtype=bool))[:, :, None]

    def step(S, xs):
        qc, kc, vc, gc = xs
        b = jnp.cumsum(gc, axis=2)
        b_last = b[:, :, -1:, :]
        o_inter = jnp.einsum("bhtd,bhde->bhte", qc * jnp.exp(b), S)
        rel = jnp.exp(jnp.where(lower, b[:, :, :, None, :] - b[:, :, None, :, :], -jnp.inf))
        scores = jnp.einsum("bhtd,bhsd,bhtsd->bhts", qc, kc, rel)
        o = o_inter + jnp.einsum("bhts,bhse->bhte", scores, vc)
        S_new = (jnp.exp(b_last[:, :, 0, :])[..., None] * S
                 + jnp.einsum("bhsd,bhse->bhde", kc * jnp.exp(b_last - b), vc))
        return S_new, o

    s_fin, o = lax.scan(step, s0, (to_chunks(q), to_chunks(k), to_chunks(v), to_chunks(g)))
    return s_fin, o.transpose(1, 0, 3, 2, 4).reshape(B, N, H, DV)


def hgrn_direction(q, i, f_raw, lb, s0, reverse):
    B, N = q.shape[:2]
    f = lb + (1.0 - lb) * jax.nn.sigmoid(f_raw.astype(jnp.float32))
    shp = (B, N, HG_HEADS, HG_DK)
    g = jnp.log(f).reshape(shp)
    k = (1.0 - f).reshape(shp)
    seqs = (q, k, i, g)
    if reverse:
        seqs = tuple(jnp.flip(a, axis=1) for a in seqs)
    s_fin, o = hgrn_chunk_scan(*seqs, s0)
    if reverse:
        o = jnp.flip(o, axis=1)
    return s_fin, o


def hgrn_readout(o, gate_raw, g_norm, dtype):
    B, N = o.shape[:2]
    o = rms_norm(o, g_norm).reshape(B, N, HG_HEADS * HG_DV)
    return (o * jax.nn.silu(gate_raw.astype(jnp.float32))).astype(dtype)


def mixer_layer(h, hc, w_in, q_norm, kv_norm, w_uq, w_ukv, lb_fwd, lb_bwd, g_norm, w_out, cos, sin, want_ctx):
    B, N = h.shape[:2]
    M = hc.shape[1]
    cq, ckv, kr, hq, hf_f, hf_b, hi, hg = split_cols(h @ w_in)
    cq_c, ckv_c, kr_c, hq_c, hf_f_c, hf_b_c, hi_c, hg_c = split_cols(hc @ w_in)

    kn_c, krr_c, v_c = mla_keys(ckv_c, kr_c, kv_norm, w_ukv, None, None)
    kn_l, krr_l, v_l = mla_keys(ckv, kr, kv_norm, w_ukv, cos, sin)
    kn_all = jnp.concatenate([kn_c, kn_l], axis=1)
    kr_all = jnp.concatenate([krr_c, krr_l], axis=1)
    v_all = jnp.concatenate([v_c, v_l], axis=1)
    qn, qr = mla_queries(cq, q_norm, w_uq, cos, sin)
    att = latent_attention(qn, qr, kn_all, kr_all, v_all)

    f32 = jnp.float32
    q_h = jax.nn.silu(hq.astype(f32)).reshape(B, N, HG_HEADS, HG_DK)
    i_h = hi.astype(f32).reshape(B, N, HG_HEADS, HG_DV)
    qc_h = jax.nn.silu(hq_c.astype(f32)).reshape(B, M, HG_HEADS, HG_DK)
    ic_h = hi_c.astype(f32).reshape(B, M, HG_HEADS, HG_DV)
    s0 = jnp.zeros((B, HG_HEADS, HG_DK, HG_DV), f32)
    s_f, o_cf = hgrn_direction(qc_h, ic_h, hf_f_c, lb_fwd, s0, False)
    s_b, o_cb = hgrn_direction(qc_h, ic_h, hf_b_c, lb_bwd, s0, True)
    _, o_f = hgrn_direction(q_h, i_h, hf_f, lb_fwd, s_f, False)
    _, o_b = hgrn_direction(q_h, i_h, hf_b, lb_bwd, s_b, True)
    rec = hgrn_readout(o_f + o_b, hg, g_norm, h.dtype)

    mix = jnp.concatenate([att, rec], axis=-1) @ w_out
    if want_ctx:
        qn_c, qr_c = mla_queries(cq_c, q_norm, w_uq, None, None)
        att_c = block_attn(qn_c, qr_c, kn_c, krr_c, v_c).reshape(B, M, MLA_HEADS * MLA_V)
        rec_c = hgrn_readout(o_cf + o_cb, hg_c, g_norm, hc.dtype)
        mix_c = jnp.concatenate([att_c, rec_c], axis=-1) @ w_out
        return mix, mix_c
    return mix, None


def moe_ffn(h, router_w, router_b, w_gu, b_gu, w_dn, b_dn):
    T, D = h.shape
    logits = (h @ router_w + router_b).astype(jnp.float32)
    top_logit, top_expert = lax.top_k(logits, TOP_K)
    gates = jax.nn.softmax(top_logit, axis=-1)
    n_assign = T * TOP_K
    flat_e = top_expert.reshape(-1)
    order = jnp.argsort(flat_e)
    sorted_e = flat_e[order]
    counts = jnp.bincount(flat_e, length=N_EXPERTS)
    padded = (counts + MOE_BLOCK - 1) // MOE_BLOCK * MOE_BLOCK
    pad_end = jnp.cumsum(padded)
    pad_start = pad_end - padded
    start = jnp.cumsum(counts) - counts
    dest = pad_start[sorted_e] + jnp.arange(n_assign) - start[sorted_e]
    n_blocks = -(-n_assign // MOE_BLOCK) + N_EXPERTS
    n_slots = n_blocks * MOE_BLOCK
    slot_tok = jnp.full((n_slots,), T, jnp.int32).at[dest].set((order // TOP_K).astype(jnp.int32))
    slot_gate = jnp.zeros((n_slots,), jnp.float32).at[dest].set(gates.reshape(-1)[order])
    block_e = jnp.minimum(jnp.searchsorted(pad_end, jnp.arange(n_blocks) * MOE_BLOCK, side="right"), N_EXPERTS - 1)
    h_pad = jnp.concatenate([h, jnp.zeros((1, D), h.dtype)], axis=0)

    def expert_block(args):
        tok, gate, e = args
        xs = h_pad[tok]
        gu = xs @ w_gu[e] + b_gu[e]
        glu = jnp.minimum(gu[:, :D_FF], SWIGLU_LIMIT)
        lin = jnp.clip(gu[:, D_FF:], -SWIGLU_LIMIT, SWIGLU_LIMIT)
        act = glu * jax.nn.sigmoid(SWIGLU_ALPHA * glu) * (lin + 1.0)
        return (act @ w_dn[e] + b_dn[e]) * gate[:, None].astype(h.dtype)

    y = lax.map(expert_block, (slot_tok.reshape(n_blocks, MOE_BLOCK), slot_gate.reshape(n_blocks, MOE_BLOCK), block_e))
    return jax.ops.segment_sum(y.reshape(n_slots, D), slot_tok, num_segments=T + 1)[:T]


def setup_inputs(seed: int = 0) -> dict:
    key = jax.random.key(seed)
    ks = jax.random.split(key, 24)
    f32 = jnp.float32
    L = DEPTH

    def nrm(k, shape, scale):
        return jax.random.normal(k, shape, f32) * scale

    def gain(k, shape):
        return 1.0 + 0.05 * jax.random.normal(k, shape, f32)

    return {
        "x": nrm(ks[0], (BATCH, SEQ, D_MODEL), 1.0),
        "c": nrm(ks[1], (BATCH, D_MODEL), 1.0),
        "ctx": nrm(ks[2], (BATCH, CTX_LEN, D_MODEL), 1.0),
        "c_ctx": nrm(ks[3], (D_MODEL,), 1.0),
        "w_mod": nrm(ks[4], (L, D_MODEL, 6 * D_MODEL), 0.5 * D_MODEL ** -0.5),
        "b_mod": nrm(ks[5], (L, 6 * D_MODEL), 0.02),
        "ln_mix_pre": gain(ks[6], (L, D_MODEL)),
        "ln_mix_post": gain(ks[7], (L, D_MODEL)),
        "ln_ffn_pre": gain(ks[8], (L, D_MODEL)),
        "ln_ffn_post": gain(ks[9], (L, D_MODEL)),
        "w_in": nrm(ks[10], (L, D_MODEL, IN_COLS), D_MODEL ** -0.5),
        "q_norm": gain(ks[11], (L, MLA_Q_RANK)),
        "kv_norm": gain(ks[12], (L, MLA_KV_RANK)),
        "w_uq": nrm(ks[13], (L, MLA_Q_RANK, MLA_HEADS, MLA_QK), MLA_Q_RANK ** -0.5),
        "w_ukv": nrm(ks[14], (L, MLA_KV_RANK, MLA_HEADS, MLA_NOPE + MLA_V), MLA_KV_RANK ** -0.5),
        "hgrn_lb": nrm(ks[15], (2, L + 1, HG_HEADS * HG_DK), 0.5),
        "g_norm": gain(ks[16], (L, HG_DV)),
        "w_out": nrm(ks[17], (L, MIX_WIDTH, D_MODEL), MIX_WIDTH ** -0.5),
        "router_w": nrm(ks[18], (L, D_MODEL, N_EXPERTS), D_MODEL ** -0.5),
        "router_b": nrm(ks[19], (L, N_EXPERTS), 0.01),
        "w_gate_up": nrm(ks[20], (L, N_EXPERTS, D_MODEL, 2 * D_FF), D_MODEL ** -0.5),
        "b_gate_up": nrm(ks[21], (L, N_EXPERTS, 2 * D_FF), 0.02),
        "w_down": nrm(ks[22], (L, N_EXPERTS, D_FF, D_MODEL), D_FF ** -0.5),
        "b_down": nrm(ks[23], (L, N_EXPERTS, D_MODEL), 0.02),
    }


def reference(x, c, ctx, c_ctx, w_mod, b_mod, ln_mix_pre, ln_mix_post, ln_ffn_pre, ln_ffn_post, w_in, q_norm, kv_norm, w_uq, w_ukv, hgrn_lb, g_norm, w_out, router_w, router_b, w_gate_up, b_gate_up, w_down, b_down):
    B, N, D = x.shape
    ROWS = N // GRID_W
    cos, sin = axial_rope(ROWS)
    lb = jnp.cumsum(jax.nn.softmax(hgrn_lb.astype(jnp.float32), axis=1), axis=1)
    xc = ctx
    for l in range(DEPTH):
        want_ctx = l < DEPTH - 1
        mod = jax.nn.silu(c) @ w_mod[l] + b_mod[l]
        mod_c = jax.nn.silu(c_ctx) @ w_mod[l] + b_mod[l]
        sh_m, sc_m, gt_m, sh_f, sc_f, gt_f = jnp.split(mod[:, None, :], 6, axis=-1)
        sh_mc, sc_mc, gt_mc, sh_fc, sc_fc, gt_fc = jnp.split(mod_c, 6, axis=-1)

        h = modulate(rms_norm(x, ln_mix_pre[l]), sh_m, sc_m)
        hc = modulate(rms_norm(xc, ln_mix_pre[l]), sh_mc, sc_mc)
        mix, mix_c = mixer_layer(h, hc, w_in[l], q_norm[l], kv_norm[l], w_uq[l], w_ukv[l], lb[0, l], lb[1, l], g_norm[l], w_out[l], cos, sin, want_ctx)
        x = x + gt_m * rms_norm(mix, ln_mix_post[l])

        h2 = modulate(rms_norm(x, ln_ffn_pre[l]), sh_f, sc_f)
        y = moe_ffn(h2.reshape(B * N, D), router_w[l], router_b[l], w_gate_up[l], b_gate_up[l], w_down[l], b_down[l]).reshape(B, N, D)
        x = x + gt_f * rms_norm(y, ln_ffn_post[l])

        if want_ctx:
            M = xc.shape[1]
            xc = xc + gt_mc * rms_norm(mix_c, ln_mix_post[l])
            hc2 = modulate(rms_norm(xc, ln_ffn_pre[l]), sh_fc, sc_fc)
            yc = moe_ffn(hc2.reshape(B * M, D), router_w[l], router_b[l], w_gate_up[l], b_gate_up[l], w_down[l], b_down[l]).reshape(B, M, D)
            xc = xc + gt_fc * rms_norm(yc, ln_ffn_post[l])
    return x
```

```python
import functools

import numpy as np
import jax
import jax.numpy as jnp
from jax import lax
from jax.experimental import pallas as pl
from jax.experimental.pallas import tpu as pltpu

F32 = jnp.float32
BF16 = jnp.bfloat16
I32 = jnp.int32

D = 2048
B = 4
N = 4096
M = 256
NA = N + M
T = B * N
TA = B * NA
EPS = 1e-6
H = 8
HW = 128
Q_RANK = 512
KV_RANK = 256
ROPE = 64
QK_SCALE = float((128 + ROPE) ** -0.5)
GRID_W = 64
ROPE_THETA = 10000.0
NE = 32
TOPK = 4
DFF = 2048
LIMIT = 7.0
ALPHA = 1.702

LANES = 128
SUBLANES = 8
VMEM_LIMIT = 56 * 1024 * 1024

ROW_T = 256
TILES_A = NA // ROW_T
TILES_L = N // ROW_T
MM_TM = 1024
MM_TN = 1024
HG_C = 128
HG_STEPS = NA // HG_C
HG_CTX = M // HG_C
HG_LEVELS = (64, 32, 16, 8, 4, 2, 1)
SUB = 256
SB_R = 1024
SB_SUBS = SB_R // SUB
S_MAX = T * TOPK // SB_R + NE
FF_T = 256
N1 = DFF // FF_T
N2 = D // FF_T
PACK_W = D // 2 // LANES
DISP_T = 512


def _sigmoid(x):
    return 1.0 / (1.0 + jnp.exp(-x))


def _rms(x, w):
    ms = jnp.mean(x * x, axis=-1, keepdims=True)
    return x * lax.rsqrt(ms + EPS) * w


def _cparams(sem, vmem=VMEM_LIMIT):
    return pltpu.CompilerParams(dimension_semantics=sem, vmem_limit_bytes=vmem)


def _mod_kernel(c_ref, w_ref, b_ref, o_ref):
    c = c_ref[...]
    a = (c * _sigmoid(c)).astype(BF16)
    o_ref[...] = jnp.dot(a, w_ref[...].astype(BF16), preferred_element_type=F32) + b_ref[...]


def _mod_call(c_all, w_mod, b_mod):
    tn = 1024
    return pl.pallas_call(
        _mod_kernel,
        grid=(6 * D // tn,),
        in_specs=[pl.BlockSpec((8, D), lambda j: (0, 0)),
                  pl.BlockSpec((D, tn), lambda j: (0, j)),
                  pl.BlockSpec((1, tn), lambda j: (0, j))],
        out_specs=pl.BlockSpec((8, tn), lambda j: (0, j)),
        out_shape=jax.ShapeDtypeStruct((8, 6 * D), F32),
        compiler_params=_cparams(("arbitrary",)),
        name="mod",
    )(c_all, w_mod, b_mod)


def _normmod_kernel(x_ref, ctx_ref, sh_ref, sc_ref, w_ref, o_ref):
    def go(src):
        y = _rms(src, w_ref[...])
        o_ref[0] = (y * (1.0 + sc_ref[0]) + sh_ref[0]).astype(BF16)

    i = pl.program_id(1)

    @pl.when(i == 0)
    def _():
        go(ctx_ref[0])

    @pl.when(i > 0)
    def _():
        go(x_ref[0])


def _normmod_call(x, ctx, mod3, ln_w):
    row = lambda b, i: jnp.where(i == 0, B, b)
    return pl.pallas_call(
        _normmod_kernel,
        grid=(B, TILES_A),
        in_specs=[pl.BlockSpec((1, ROW_T, D), lambda b, i: (b, jnp.maximum(i - 1, 0), 0)),
                  pl.BlockSpec((1, ROW_T, D), lambda b, i: (b, 0, 0)),
                  pl.BlockSpec((1, 1, D), lambda b, i: (row(b, i), 0, 0)),
                  pl.BlockSpec((1, 1, D), lambda b, i: (row(b, i), 0, 1)),
                  pl.BlockSpec((1, D), lambda b, i: (0, 0))],
        out_specs=pl.BlockSpec((1, ROW_T, D), lambda b, i: (b, i, 0)),
        out_shape=jax.ShapeDtypeStruct((B, NA, D), BF16),
        compiler_params=_cparams(("arbitrary", "arbitrary")),
        name="normmod",
    )(x, ctx, mod3, mod3, ln_w)


def _matmul_kernel(a_ref, w_ref, o_ref):
    o_ref[...] = jnp.dot(a_ref[...], w_ref[...], preferred_element_type=F32).astype(o_ref.dtype)


def _matmul_call(a, w, out_dtype, name):
    rows, k = a.shape
    cols = w.shape[1]
    return pl.pallas_call(
        _matmul_kernel,
        grid=(rows // MM_TM, cols // MM_TN),
        in_specs=[pl.BlockSpec((MM_TM, k), lambda i, j: (i, 0)),
                  pl.BlockSpec((k, MM_TN), lambda i, j: (0, j))],
        out_specs=pl.BlockSpec((MM_TM, MM_TN), lambda i, j: (i, j)),
        out_shape=jax.ShapeDtypeStruct((rows, cols), out_dtype),
        compiler_params=_cparams(("arbitrary", "arbitrary")),
        name=name,
    )(a, w)


def _kvprep_kernel(p_ref, kvn_ref, w_ref, cs_ref, k_ref, v_ref):
    ckv = p_ref[:, Q_RANK:Q_RANK + KV_RANK].astype(F32)
    ckvn = _rms(ckv, kvn_ref[...]).astype(BF16)
    kv = jnp.dot(ckvn, w_ref[...], preferred_element_type=F32)
    a = p_ref[:, Q_RANK + KV_RANK:Q_RANK + KV_RANK + LANES].astype(F32) * cs_ref[...]
    r = a + pltpu.roll(a, ROPE, axis=1)
    lane = lax.broadcasted_iota(I32, r.shape, 1)
    r = jnp.where(lane < ROPE, r, 0.0).astype(BF16)
    for h in range(H):
        k_ref[0, h, :, 0:HW] = kv[:, h * HW:(h + 1) * HW].astype(BF16)
        k_ref[0, h, :, HW:2 * HW] = r
        v_ref[0, h] = kv[:, (H + h) * HW:(H + h + 1) * HW].astype(BF16)


def _kvprep_call(p, kv_norm, w_kv, cs_all):
    return pl.pallas_call(
        _kvprep_kernel,
        grid=(B, TILES_A),
        in_specs=[pl.BlockSpec((ROW_T, 1024), lambda b, i: (b * TILES_A + i, 0)),
                  pl.BlockSpec((1, KV_RANK), lambda b, i: (0, 0)),
                  pl.BlockSpec((KV_RANK, 2 * H * HW), lambda b, i: (0, 0)),
                  pl.BlockSpec((ROW_T, LANES), lambda b, i: (i, 0))],
        out_specs=[pl.BlockSpec((1, H, ROW_T, 2 * HW), lambda b, i: (b, 0, i, 0)),
                   pl.BlockSpec((1, H, ROW_T, HW), lambda b, i: (b, 0, i, 0))],
        out_shape=[jax.ShapeDtypeStruct((B, H, NA, 2 * HW), BF16),
                   jax.ShapeDtypeStruct((B, H, NA, HW), BF16)],
        compiler_params=_cparams(("arbitrary", "arbitrary")),
        name="kvprep",
    )(p, kv_norm, w_kv, cs_all)


def _qprep_kernel(p_ref, qn_ref, w_ref, cs_ref, q_ref):
    cq = p_ref[:, 0:Q_RANK].astype(F32)
    cqn = _rms(cq, qn_ref[...]).astype(BF16)
    q = jnp.dot(cqn, w_ref[...], preferred_element_type=F32)
    cs = cs_ref[...]
    for h in range(H):
        q_ref[0, h, :, 0:HW] = (q[:, 2 * h * HW:(2 * h + 1) * HW] * QK_SCALE).astype(BF16)
        a = q[:, (2 * h + 1) * HW:(2 * h + 2) * HW] * cs
        q_ref[0, h, :, HW:2 * HW] = (a + pltpu.roll(a, ROPE, axis=1)).astype(BF16)


def _qprep_call(p, q_norm, w_q, cs_q):
    return pl.pallas_call(
        _qprep_kernel,
        grid=(B, TILES_L),
        in_specs=[pl.BlockSpec((ROW_T, 1024), lambda b, i: (b * TILES_A + 1 + i, 0)),
                  pl.BlockSpec((1, Q_RANK), lambda b, i: (0, 0)),
                  pl.BlockSpec((Q_RANK, 2 * H * HW), lambda b, i: (0, 0)),
                  pl.BlockSpec((ROW_T, LANES), lambda b, i: (i, 0))],
        out_specs=pl.BlockSpec((1, H, ROW_T, 2 * HW), lambda b, i: (b, 0, i, 0)),
        out_shape=jax.ShapeDtypeStruct((B, H, N, 2 * HW), BF16),
        compiler_params=_cparams(("arbitrary", "arbitrary")),
        name="qprep",
    )(p, q_norm, w_q, cs_q)


ATT_TQ = 256


def _attn_kernel(q_ref, k_ref, v_ref, o_ref):
    s = lax.dot_general(q_ref[0, 0], k_ref[0, 0], (((1,), (1,)), ((), ())),
                        preferred_element_type=F32)
    m = jnp.max(s, axis=-1, keepdims=True)
    p = jnp.exp(s - m)
    l = jnp.sum(p, axis=-1, keepdims=True)
    o = jnp.dot(p.astype(BF16), v_ref[0, 0], preferred_element_type=F32)
    o_ref[...] = (o * (1.0 / l)).astype(BF16)


def _attn_call(q, k, v):
    return pl.pallas_call(
        _attn_kernel,
        grid=(B, H, N // ATT_TQ),
        in_specs=[pl.BlockSpec((1, 1, ATT_TQ, 2 * HW), lambda b, h, i: (b, h, i, 0)),
                  pl.BlockSpec((1, 1, NA, 2 * HW), lambda b, h, i: (b, h, 0, 0)),
                  pl.BlockSpec((1, 1, NA, HW), lambda b, h, i: (b, h, 0, 0))],
        out_specs=pl.BlockSpec((ATT_TQ, HW), lambda b, h, i: (b * (N // ATT_TQ) + i, h)),
        out_shape=jax.ShapeDtypeStruct((T, H * HW), BF16),
        compiler_params=_cparams(("arbitrary", "arbitrary", "arbitrary")),
        name="attention",
    )(q, k, v)


def _hgrn_tables(reverse):
    c = HG_C
    t = np.arange(c)[:, None]
    u = np.arange(c)[None, :]
    mats = [(u <= t), (u > t)]
    masks = [(u == t)]
    for m in HG_LEVELS:
        blk_t = t // m
        odd = (blk_t % 2) == 1
        start = blk_t * m
        end = (blk_t + 1) * m - 1
        mats.append(odd & (u >= start) & (u <= t))
        mats.append((~odd) & (u > t) & (u <= end))
        masks.append(odd & ((u // m) == blk_t - 1))
    mats = np.stack(mats).astype(np.float32)
    masks = np.stack(masks).astype(np.float32)
    if reverse:
        mats = mats[:, ::-1, ::-1]
        masks = masks[:, ::-1, ::-1]
    return (jnp.asarray(mats.reshape(-1, c), dtype=BF16),
            jnp.asarray(masks.reshape(-1, c), dtype=F32))


def _hgrn_head(h, hq_ref, hi_ref, fr_ref, lb_ref, dst_ref, mk_ref, s_ref, o_ref, total_row):
    c = HG_C
    col = pl.ds(pl.multiple_of(h * HW, HW), HW)
    xq = hq_ref[:, col].astype(F32)
    q = xq * _sigmoid(xq)
    lb = lb_ref[:, col]
    f = lb + (1.0 - lb) * _sigmoid(fr_ref[:, col])
    g = jnp.log(f)
    k = 1.0 - f
    v = hi_ref[:, col]
    g_hi = g.astype(BF16)
    g_lo = (g - g_hi.astype(F32)).astype(BF16)

    def decay(i):
        d = dst_ref[i * c:(i + 1) * c, :]
        return jnp.exp(jnp.dot(d, g_hi, preferred_element_type=F32)
                       + jnp.dot(d, g_lo, preferred_element_type=F32))

    nt = (((1,), (1,)), ((), ()))
    e_cum = decay(0)
    st = s_ref[h]
    o = lax.dot_general((q * e_cum).astype(BF16), st.astype(BF16), nt, preferred_element_type=F32)
    qb = q.astype(BF16)
    kb = k.astype(BF16)
    sc = lax.dot_general(qb, kb, nt, preferred_element_type=F32) * mk_ref[0:c, :]
    for li in range(len(HG_LEVELS)):
        qs = (q * decay(2 + 2 * li)).astype(BF16)
        ks = (k * decay(3 + 2 * li)).astype(BF16)
        sc = sc + lax.dot_general(qs, ks, nt, preferred_element_type=F32) * mk_ref[(li + 1) * c:(li + 2) * c, :]
    o = o + jnp.dot(sc.astype(BF16), v, preferred_element_type=F32)
    o_ref[:, col] = o.astype(BF16)
    kend = (k * decay(1)).astype(BF16)
    e_tot = e_cum[total_row:total_row + 1, :]
    s_ref[h] = st * e_tot + lax.dot_general(v, kend, (((0,), (0,)), ((), ())), preferred_element_type=F32)


def _hgrn_kernel(hq_f, hi_f, fr_f, hq_b, hi_b, fr_b, lbf_ref, lbb_ref, dstf_ref, mkf_ref, dstb_ref, mkb_ref,
                 of_ref, ob_ref, sf_ref, sb_ref):
    @pl.when(pl.program_id(1) == 0)
    def _():
        sf_ref[...] = jnp.zeros_like(sf_ref)
        sb_ref[...] = jnp.zeros_like(sb_ref)

    def body(h, carry):
        _hgrn_head(h, hq_f, hi_f, fr_f, lbf_ref, dstf_ref, mkf_ref, sf_ref, of_ref, HG_C - 1)
        _hgrn_head(h, hq_b, hi_b, fr_b, lbb_ref, dstb_ref, mkb_ref, sb_ref, ob_ref, 0)
        return carry

    lax.fori_loop(0, H, body, 0)


def _hgrn_call(p, fraw, lb_f, lb_b):
    dst_f, mk_f = _hgrn_tables(False)
    dst_b, mk_b = _hgrn_tables(True)
    wide = H * HW

    def cf(b, s):
        return b * HG_STEPS + s

    def cb(b, s):
        return b * HG_STEPS + jnp.where(s < HG_CTX, HG_CTX - 1 - s, HG_STEPS - 1 + HG_CTX - s)

    const = lambda shape: pl.BlockSpec(shape, lambda b, s: (0, 0))
    return pl.pallas_call(
        _hgrn_kernel,
        grid=(B, HG_STEPS),
        in_specs=[pl.BlockSpec((HG_C, wide), lambda b, s: (cf(b, s), 1)),
                  pl.BlockSpec((HG_C, wide), lambda b, s: (cf(b, s), 2)),
                  pl.BlockSpec((HG_C, wide), lambda b, s: (cf(b, s), 0)),
                  pl.BlockSpec((HG_C, wide), lambda b, s: (cb(b, s), 1)),
                  pl.BlockSpec((HG_C, wide), lambda b, s: (cb(b, s), 2)),
                  pl.BlockSpec((HG_C, wide), lambda b, s: (cb(b, s), 1)),
                  const((1, wide)), const((1, wide)),
                  const(dst_f.shape), const(mk_f.shape), const(dst_b.shape), const(mk_b.shape)],
        out_specs=[pl.BlockSpec((HG_C, wide), lambda b, s: (cf(b, s), 0)),
                   pl.BlockSpec((HG_C, wide), lambda b, s: (cb(b, s), 0))],
        out_shape=[jax.ShapeDtypeStruct((TA, wide), BF16), jax.ShapeDtypeStruct((TA, wide), BF16)],
        scratch_shapes=[pltpu.VMEM((H, HW, HW), F32), pltpu.VMEM((H, HW, HW), F32)],
        compiler_params=_cparams(("arbitrary", "arbitrary")),
        name="hgrn",
    )(p, p, fraw, p, p, fraw, lb_f, lb_b, dst_f, mk_f, dst_b, mk_b)


def _pack_rows(val, dst_ref, row0, rows):
    for s in range(PACK_W):
        lo = val[:, 2 * s * LANES:(2 * s + 1) * LANES]
        hi = val[:, (2 * s + 1) * LANES:(2 * s + 2) * LANES]
        w = pltpu.pack_elementwise([lo, hi], packed_dtype=BF16)
        dst_ref[pl.ds(row0 * PACK_W + s, rows, stride=PACK_W), :] = w


def _unpack_word(w):
    lo = pltpu.unpack_elementwise(w, index=0, packed_dtype=BF16, unpacked_dtype=F32)
    hi = pltpu.unpack_elementwise(w, index=1, packed_dtype=BF16, unpacked_dtype=F32)
    return lo, hi


def _mixout_kernel(att_ref, of_ref, ob_ref, hg_ref, gn_ref, wa_ref, wr_ref, x_ref, gt_ref, shf_ref, scf_ref,
                   lnp_ref, lnf_ref, rw_ref, rb_ref,
                   x1_ref, h2w_ref, ids_ref, gates_ref, cnt_ref, carry_ref):
    first = (pl.program_id(0) == 0) & (pl.program_id(1) == 0)

    @pl.when(first)
    def _():
        carry_ref[...] = jnp.zeros_like(carry_ref)

    o = of_ref[...].astype(F32) + ob_ref[...].astype(F32)
    gn = gn_ref[...]
    recs = []
    for h in range(H):
        oh = o[:, h * HW:(h + 1) * HW]
        gate = hg_ref[:, h * HW:(h + 1) * HW].astype(F32)
        recs.append((_rms(oh, gn) * (gate * _sigmoid(gate))).astype(BF16))
    rec = jnp.concatenate(recs, axis=-1)
    mix = (jnp.dot(att_ref[...], wa_ref[...], preferred_element_type=F32)
           + jnp.dot(rec, wr_ref[...], preferred_element_type=F32))
    x1 = x_ref[0] + gt_ref[0] * _rms(mix, lnp_ref[...])
    x1_ref[0] = x1
    h2 = _rms(x1, lnf_ref[...]) * (1.0 + scf_ref[0]) + shf_ref[0]
    _pack_rows(h2, h2w_ref, 0, ROW_T)

    logits = jnp.dot(h2, rw_ref[...], precision=lax.Precision.HIGHEST, preferred_element_type=F32) + rb_ref[...]
    lane = lax.broadcasted_iota(I32, logits.shape, 1)
    neg = jnp.float32(-jnp.inf)
    lg = jnp.where(lane < NE, logits, neg)
    tops, idxs = [], []
    for _ in range(TOPK):
        m = jnp.max(lg, axis=-1, keepdims=True)
        idx = jnp.min(jnp.where(lg == m, lane, LANES), axis=-1, keepdims=True)
        tops.append(m)
        idxs.append(idx)
        lg = jnp.where(lane == idx, neg, lg)
    exps = [jnp.exp(tk - tops[0]) for tk in tops]
    inv = 1.0 / (exps[0] + exps[1] + exps[2] + exps[3])
    onehot = jnp.zeros(logits.shape, F32)
    for idx in idxs:
        onehot = onehot + jnp.where(lane == idx, 1.0, 0.0)
    r = lax.broadcasted_iota(I32, (ROW_T, ROW_T), 0)
    cidx = lax.broadcasted_iota(I32, (ROW_T, ROW_T), 1)
    lower = jnp.where(cidx < r, 1.0, 0.0).astype(BF16)
    prefix = jnp.dot(lower, onehot.astype(BF16), preferred_element_type=F32) + carry_ref[...]
    ids = jnp.zeros(logits.shape, I32)
    gates = jnp.zeros(logits.shape, F32)
    for kk in range(TOPK):
        rank = jnp.sum(jnp.where(lane == idxs[kk], prefix, 0.0), axis=-1, keepdims=True).astype(I32)
        ids = jnp.where(lane == kk, idxs[kk], ids)
        ids = jnp.where(lane == TOPK + kk, rank, ids)
        gates = jnp.where(lane == kk, exps[kk] * inv, gates)
    ids_ref[...] = ids
    gates_ref[...] = gates
    carry = carry_ref[...] + jnp.sum(onehot, axis=0, keepdims=True)
    carry_ref[...] = carry
    cnt_ref[...] = carry


def _mixout_call(att, o_f, o_b, p, g_norm_t, w_a, w_r, x, mod3, ln_post, ln_ffn, r_w, r_b):
    wide = H * HW
    lat = lambda b, i: b * TILES_A + 1 + i
    tok = lambda b, i: b * TILES_L + i
    const = lambda shape: pl.BlockSpec(shape, lambda b, i: tuple(0 for _ in shape))
    return pl.pallas_call(
        _mixout_kernel,
        grid=(B, TILES_L),
        in_specs=[pl.BlockSpec((ROW_T, wide), lambda b, i: (tok(b, i), 0)),
                  pl.BlockSpec((ROW_T, wide), lambda b, i: (lat(b, i), 0)),
                  pl.BlockSpec((ROW_T, wide), lambda b, i: (lat(b, i), 0)),
                  pl.BlockSpec((ROW_T, wide), lambda b, i: (lat(b, i), 3)),
                  const((1, HW)), const((wide, D)), const((wide, D)),
                  pl.BlockSpec((1, ROW_T, D), lambda b, i: (b, i, 0)),
                  pl.BlockSpec((1, 1, D), lambda b, i: (b, 0, 2)),
                  pl.BlockSpec((1, 1, D), lambda b, i: (b, 0, 3)),
                  pl.BlockSpec((1, 1, D), lambda b, i: (b, 0, 4)),
                  const((1, D)), const((1, D)), const((D, LANES)), const((1, LANES))],
        out_specs=[pl.BlockSpec((1, ROW_T, D), lambda b, i: (b, i, 0)),
                   pl.BlockSpec((ROW_T * PACK_W, LANES), lambda b, i: (tok(b, i), 0)),
                   pl.BlockSpec((ROW_T, LANES), lambda b, i: (tok(b, i), 0)),
                   pl.BlockSpec((ROW_T, LANES), lambda b, i: (tok(b, i), 0)),
                   const((1, LANES))],
        out_shape=[jax.ShapeDtypeStruct((B, N, D), F32),
                   jax.ShapeDtypeStruct((T * PACK_W, LANES), I32),
                   jax.ShapeDtypeStruct((T, LANES), I32),
                   jax.ShapeDtypeStruct((T, LANES), F32),
                   jax.ShapeDtypeStruct((1, LANES), F32)],
        scratch_shapes=[pltpu.VMEM((1, LANES), F32)],
        compiler_params=_cparams(("arbitrary", "arbitrary")),
        name="mixout_router",
    )(att, o_f, o_b, p, g_norm_t, w_a, w_r, x, mod3, mod3, mod3, ln_post, ln_ffn, r_w, r_b)


def _dispatch_copy(src_hbm, dst_hbm, sem, src_row, dst_row, rows):
    return pltpu.make_async_copy(src_hbm.at[pl.ds(src_row * PACK_W, rows * PACK_W)],
                                 dst_hbm.at[pl.ds(dst_row * PACK_W, rows * PACK_W)], sem)


def _dispatch_kernel(dest_ref, h2w_hbm, init_hbm, xs_hbm, sem):
    del init_hbm
    base = pl.program_id(0) * DISP_T

    def issue(t, carry):
        tok = base + t
        for kk in range(TOPK):
            _dispatch_copy(h2w_hbm, xs_hbm, sem, tok, dest_ref[tok * TOPK + kk], 1).start()
        return carry

    lax.fori_loop(0, DISP_T, issue, 0)
    _dispatch_copy(h2w_hbm, xs_hbm, sem, 0, 0, DISP_T * TOPK).wait()


def _dispatch_call(dest, h2w, xs_init):
    gs = pltpu.PrefetchScalarGridSpec(
        num_scalar_prefetch=1, grid=(T // DISP_T,),
        in_specs=[pl.BlockSpec(memory_space=pl.ANY), pl.BlockSpec(memory_space=pl.ANY)],
        out_specs=pl.BlockSpec(memory_space=pl.ANY),
        scratch_shapes=[pltpu.SemaphoreType.DMA(())])
    return pl.pallas_call(
        _dispatch_kernel, grid_spec=gs,
        out_shape=jax.ShapeDtypeStruct(xs_init.shape, xs_init.dtype),
        input_output_aliases={2: 0},
        compiler_params=_cparams(("arbitrary",)),
        name="moe_dispatch",
    )(dest, h2w, xs_init)


def _expert_kernel(sbe_ref, sbn_ref, sbb_ref, xs_ref, wg_ref, wl_ref, bg_ref, bl_ref, wd_ref, bd_ref,
                   out_ref, x_s, act_s, outw_s, wg_s, wl_s, wd_s):
    del sbe_ref, sbb_ref
    s = pl.program_id(0)
    j = pl.program_id(1)
    nblk = sbn_ref[s]

    @pl.when((j == 0) & (nblk > 0))
    def _():
        for i in range(SB_SUBS):
            @pl.when(i < nblk)
            def _():
                for w in range(PACK_W):
                    lo, hi = _unpack_word(xs_ref[pl.ds(i * SUB * PACK_W + w, SUB, stride=PACK_W), :])
                    x_s[i * SUB:(i + 1) * SUB, 2 * w * LANES:(2 * w + 1) * LANES] = lo.astype(BF16)
                    x_s[i * SUB:(i + 1) * SUB, (2 * w + 1) * LANES:(2 * w + 2) * LANES] = hi.astype(BF16)

    @pl.when((j < N1) & (nblk > 0))
    def _():
        wg_s[...] = wg_ref[0].astype(BF16)
        wl_s[...] = wl_ref[0].astype(BF16)
        for i in range(SB_SUBS):
            @pl.when(i < nblk)
            def _():
                x = x_s[i * SUB:(i + 1) * SUB, :]
                gu = jnp.dot(x, wg_s[...], preferred_element_type=F32) + bg_ref[0]
                li = jnp.dot(x, wl_s[...], preferred_element_type=F32) + bl_ref[0]
                glu = jnp.minimum(gu, LIMIT)
                lin = jnp.clip(li, -LIMIT, LIMIT)
                act = glu * _sigmoid(ALPHA * glu) * (lin + 1.0)
                act_s[j, i * SUB:(i + 1) * SUB, :] = act.astype(BF16)

    @pl.when((j >= N1) & (nblk > 0))
    def _():
        wd_s[...] = wd_ref[0].astype(BF16)
        for i in range(SB_SUBS):
            @pl.when(i < nblk)
            def _():
                acc = jnp.zeros((SUB, FF_T), F32) + bd_ref[0]
                for jj in range(N1):
                    acc = acc + jnp.dot(act_s[jj, i * SUB:(i + 1) * SUB, :], wd_s[jj * FF_T:(jj + 1) * FF_T, :],
                                        preferred_element_type=F32)
                for half in range(FF_T // (2 * LANES)):
                    lo = acc[:, 2 * half * LANES:(2 * half + 1) * LANES]
                    hi = acc[:, (2 * half + 1) * LANES:(2 * half + 2) * LANES]
                    outw_s[(j - N1) * (FF_T // (2 * LANES)) + half, i * SUB:(i + 1) * SUB, :] = (
                        pltpu.pack_elementwise([lo, hi], packed_dtype=BF16))

    @pl.when((j == N1 + N2 - 1) & (nblk > 0))
    def _():
        for i in range(SB_SUBS):
            @pl.when(i < nblk)
            def _():
                for w in range(PACK_W):
                    out_ref[pl.ds(i * SUB * PACK_W + w, SUB, stride=PACK_W), :] = outw_s[w, i * SUB:(i + 1) * SUB, :]

            @pl.when(i >= nblk)
            def _():
                out_ref[i * SUB * PACK_W:(i + 1) * SUB * PACK_W, :] = jnp.zeros((SUB * PACK_W, LANES), I32)


def _expert_call(sb_e, sb_n, sb_b, xs, w_gu, b_gu, w_dn, b_dn):
    jg = lambda j: jnp.minimum(j, N1 - 1)
    jd = lambda j: jnp.maximum(j - N1, 0)
    gs = pltpu.PrefetchScalarGridSpec(
        num_scalar_prefetch=3, grid=(S_MAX, N1 + N2),
        in_specs=[pl.BlockSpec((SB_R * PACK_W, LANES), lambda s, j, e, n, b: (b[s], 0)),
                  pl.BlockSpec((1, D, FF_T), lambda s, j, e, n, b: (e[s], 0, jg(j))),
                  pl.BlockSpec((1, D, FF_T), lambda s, j, e, n, b: (e[s], 0, N1 + jg(j))),
                  pl.BlockSpec((1, 1, FF_T), lambda s, j, e, n, b: (e[s], 0, jg(j))),
                  pl.BlockSpec((1, 1, FF_T), lambda s, j, e, n, b: (e[s], 0, N1 + jg(j))),
                  pl.BlockSpec((1, DFF, FF_T), lambda s, j, e, n, b: (e[s], 0, jd(j))),
                  pl.BlockSpec((1, 1, FF_T), lambda s, j, e, n, b: (e[s], 0, jd(j)))],
        out_specs=pl.BlockSpec((SB_R * PACK_W, LANES), lambda s, j, e, n, b: (b[s], 0)),
        scratch_shapes=[pltpu.VMEM((SB_R, D), BF16),
                        pltpu.VMEM((N1, SB_R, FF_T), BF16),
                        pltpu.VMEM((PACK_W, SB_R, LANES), I32),
                        pltpu.VMEM((D, FF_T), BF16),
                        pltpu.VMEM((D, FF_T), BF16),
                        pltpu.VMEM((DFF, FF_T), BF16)])
    return pl.pallas_call(
        _expert_kernel, grid_spec=gs,
        out_shape=jax.ShapeDtypeStruct(xs.shape, xs.dtype),
        input_output_aliases={3: 0},
        compiler_params=_cparams(("arbitrary", "arbitrary")),
        name="moe_experts",
    )(sb_e, sb_n, sb_b, xs, w_gu, w_gu, b_gu, b_gu, w_dn, b_dn)


def _combine_copy(src_hbm, buf, sem, src_row, dst_row, rows):
    return pltpu.make_async_copy(src_hbm.at[pl.ds(src_row * PACK_W, rows * PACK_W)],
                                 buf.at[pl.ds(dst_row * PACK_W, rows * PACK_W)], sem)


def _combine_kernel(dest_ref, ys_hbm, gates_ref, x1_ref, gt_ref, ln_ref, o_ref, buf, y_s, sem):
    base = (pl.program_id(0) * TILES_L + pl.program_id(1)) * ROW_T

    def issue(t, carry):
        for kk in range(TOPK):
            _combine_copy(ys_hbm, buf, sem, dest_ref[(base + t) * TOPK + kk], kk * ROW_T + t, 1).start()
        return carry

    lax.fori_loop(0, ROW_T, issue, 0)
    _combine_copy(ys_hbm, buf, sem, 0, 0, ROW_T * TOPK).wait()

    gates = gates_ref[...]
    gk = [jnp.broadcast_to(gates[:, kk:kk + 1], (ROW_T, LANES)) for kk in range(TOPK)]
    for w in range(PACK_W):
        acc_lo = jnp.zeros((ROW_T, LANES), F32)
        acc_hi = jnp.zeros((ROW_T, LANES), F32)
        for kk in range(TOPK):
            lo, hi = _unpack_word(buf[pl.ds(kk * ROW_T * PACK_W + w, ROW_T, stride=PACK_W), :])
            acc_lo = acc_lo + gk[kk] * lo
            acc_hi = acc_hi + gk[kk] * hi
        y_s[:, 2 * w * LANES:(2 * w + 1) * LANES] = acc_lo
        y_s[:, (2 * w + 1) * LANES:(2 * w + 2) * LANES] = acc_hi
    o_ref[0] = x1_ref[0] + gt_ref[0] * _rms(y_s[...], ln_ref[...])


def _combine_call(dest, ys, gates, x1, mod3, ln_post):
    tok = lambda b, i: b * TILES_L + i
    gs = pltpu.PrefetchScalarGridSpec(
        num_scalar_prefetch=1, grid=(B, TILES_L),
        in_specs=[pl.BlockSpec(memory_space=pl.ANY),
                  pl.BlockSpec((ROW_T, LANES), lambda b, i, d: (tok(b, i), 0)),
                  pl.BlockSpec((1, ROW_T, D), lambda b, i, d: (b, i, 0)),
                  pl.BlockSpec((1, 1, D), lambda b, i, d: (b, 0, 5)),
                  pl.BlockSpec((1, D), lambda b, i, d: (0, 0))],
        out_specs=pl.BlockSpec((1, ROW_T, D), lambda b, i, d: (b, i, 0)),
        scratch_shapes=[pltpu.VMEM((TOPK * ROW_T * PACK_W, LANES), I32),
                        pltpu.VMEM((ROW_T, D), F32),
                        pltpu.SemaphoreType.DMA(())])
    return pl.pallas_call(
        _combine_kernel, grid_spec=gs,
        out_shape=jax.ShapeDtypeStruct((B, N, D), F32),
        compiler_params=_cparams(("arbitrary", "arbitrary")),
        name="moe_combine",
    )(dest, ys, gates, x1, mod3, ln_post)


def _rot_cols(w):
    half = ROPE // 2
    return jnp.concatenate([-w[..., half:], w[..., :half]], axis=-1)


def _rope_tables():
    t = np.arange(N)
    r = (t // GRID_W).astype(np.float64)
    cl = (t % GRID_W).astype(np.float64)
    n_freq = ROPE // 4
    inv = ROPE_THETA ** (-np.arange(n_freq, dtype=np.float64) / n_freq)
    ang = np.concatenate([r[:, None] * inv, cl[:, None] * inv], axis=-1)
    cos, sin = np.cos(ang), np.sin(ang)
    cs = np.concatenate([cos, cos, sin, sin], axis=-1).astype(np.float32)
    ctx = np.concatenate([np.ones((M, ROPE), np.float32), np.zeros((M, ROPE), np.float32)], axis=-1)
    return jnp.asarray(np.concatenate([ctx, cs], axis=0)), jnp.asarray(cs * QK_SCALE)


def _routing_tables(ids, counts_f):
    e = ids[:, :TOPK]
    rank = ids[:, TOPK:2 * TOPK]
    counts = counts_f[0, :NE].astype(I32)
    nsb = (counts + SB_R - 1) // SB_R
    sb_end = jnp.cumsum(nsb)
    sb_start = sb_end - nsb
    onehot = e[..., None] == jnp.arange(NE, dtype=I32)
    dest = jnp.sum(jnp.where(onehot, sb_start * SB_R, 0), axis=-1) + rank
    n_active = sb_end[-1]
    s_idx = jnp.arange(S_MAX, dtype=I32)
    s_eff = jnp.minimum(s_idx, n_active - 1)
    sb_e = jnp.minimum(jnp.searchsorted(sb_end, s_eff, side="right"), NE - 1).astype(I32)
    local = s_eff - sb_start[sb_e]
    valid = jnp.clip(counts[sb_e] - local * SB_R, 0, SB_R)
    sb_n = jnp.where(s_idx < n_active, (valid + SUB - 1) // SUB, 0).astype(I32)
    return dest.reshape(-1).astype(I32), sb_e, sb_n, s_eff.astype(I32)


def kernel(x, c, ctx, c_ctx, w_mod, b_mod, ln_mix_pre, ln_mix_post, ln_ffn_pre, ln_ffn_post, w_in, q_norm, kv_norm,
           w_uq, w_ukv, hgrn_lb, g_norm, w_out, router_w, router_b, w_gate_up, b_gate_up, w_down, b_down):
    c_all = jnp.concatenate([c, c_ctx[None, :], jnp.zeros((8 - B - 1, D), F32)], axis=0)
    mod = _mod_call(c_all, w_mod[0], b_mod)
    mod3 = mod.reshape(8, 1, 6 * D)

    w = w_in[0]
    o_q, o_kv, o_kr = 0, Q_RANK, Q_RANK + KV_RANK
    o_hq = o_kr + ROPE
    wide = H * HW
    o_ff, o_fb, o_hi, o_hg = o_hq + wide, o_hq + 2 * wide, o_hq + 3 * wide, o_hq + 4 * wide
    w_kr = w[:, o_kr:o_kr + ROPE]
    w_p = jnp.concatenate([w[:, o_q:o_kr], w_kr, _rot_cols(w_kr), jnp.zeros((D, LANES), F32),
                           w[:, o_hq:o_hq + wide], w[:, o_hi:o_hi + wide], w[:, o_hg:o_hg + wide]],
                          axis=1).astype(BF16)
    w_f = w[:, o_ff:o_ff + 2 * wide].astype(BF16)
    wq = w_uq[0]
    wq_r = wq[..., HW:]
    w_q = jnp.concatenate([wq[..., :HW], wq_r, _rot_cols(wq_r)], axis=-1).reshape(Q_RANK, 2 * wide).astype(BF16)
    wkv = w_ukv[0]
    w_kv = jnp.concatenate([wkv[..., :HW].reshape(KV_RANK, wide), wkv[..., HW:].reshape(KV_RANK, wide)],
                           axis=1).astype(BF16)
    w_o = w_out[0].astype(BF16)
    cs_all, cs_q = _rope_tables()
    lb = jnp.cumsum(jax.nn.softmax(hgrn_lb.astype(F32), axis=1), axis=1)
    r_w = jnp.concatenate([router_w[0], jnp.zeros((D, LANES - NE), F32)], axis=1)
    r_b = jnp.concatenate([router_b[0], jnp.zeros((LANES - NE,), F32)])[None, :]

    h_all = _normmod_call(x, ctx, mod3, ln_mix_pre)
    h_flat = h_all.reshape(TA, D)
    p = _matmul_call(h_flat, w_p, BF16, "in_proj")
    fraw = _matmul_call(h_flat, w_f, F32, "in_proj_forget")
    k_all, v_all = _kvprep_call(p, kv_norm, w_kv, cs_all)
    q_all = _qprep_call(p, q_norm, w_q, cs_q)
    att = _attn_call(q_all, k_all, v_all)
    o_f, o_b = _hgrn_call(p, fraw, lb[0, 0][None, :], lb[1, 0][None, :])

    g_norm_t = g_norm
    x1, h2w, ids, gates, counts = _mixout_call(att, o_f, o_b, p, g_norm_t, w_o[:wide], w_o[wide:], x, mod3,
                                               ln_mix_post, ln_ffn_pre, r_w, r_b)

    dest, sb_e, sb_n, sb_b = _routing_tables(ids, counts)
    xs = _dispatch_call(dest, h2w, jnp.zeros((S_MAX * SB_R * PACK_W, LANES), I32))
    ys = _expert_call(sb_e, sb_n, sb_b, xs, w_gate_up[0], b_gate_up[0][:, None, :], w_down[0], b_down[0][:, None, :])
    return _combine_call(dest, ys, gates, x1, mod3, ln_ffn_post)
```

```python
import functools

import numpy as np
import jax
import jax.numpy as jnp
from jax import lax
from jax.experimental import pallas as pl
from jax.experimental.pallas import tpu as pltpu

F32 = jnp.float32
BF16 = jnp.bfloat16
I32 = jnp.int32

D = 2048
B = 4
N = 4096
M = 256
NA = N + M
T = B * N
TA = B * NA
EPS = 1e-6
H = 8
HW = 128
Q_RANK = 512
KV_RANK = 256
ROPE = 64
QK_SCALE = float((128 + ROPE) ** -0.5)
GRID_W = 64
ROPE_THETA = 10000.0
NE = 32
TOPK = 4
DFF = 2048
LIMIT = 7.0
ALPHA = 1.702

LANES = 128
SUBLANES = 8
VMEM_LIMIT = 56 * 1024 * 1024

ROW_T = 256
TILES_A = NA // ROW_T
TILES_L = N // ROW_T
MM_TM = 1024
MM_TN = 1024
HG_C = 128
HG_STEPS = NA // HG_C
HG_CTX = M // HG_C
HG_LEVELS = (64, 32, 16, 8, 4, 2, 1)
SUB = 256
SB_R = 1024
SB_SUBS = SB_R // SUB
S_MAX = T * TOPK // SB_R + NE
FF_T = 256
N1 = DFF // FF_T
N2 = D // FF_T
PACK_W = D // 2 // LANES
DISP_T = 512


def _sigmoid(x):
    return 1.0 / (1.0 + jnp.exp(-x))


def _rms(x, w):
    ms = jnp.mean(x * x, axis=-1, keepdims=True)
    return x * lax.rsqrt(ms + EPS) * w


def _cparams(sem, vmem=VMEM_LIMIT):
    return pltpu.CompilerParams(dimension_semantics=sem, vmem_limit_bytes=vmem)


def _mod_kernel(c_ref, w_ref, b_ref, o_ref):
    c = c_ref[...]
    a = (c * _sigmoid(c)).astype(BF16)
    o_ref[...] = jnp.dot(a, w_ref[...].astype(BF16), preferred_element_type=F32) + b_ref[...]


def _mod_call(c_all, w_mod, b_mod):
    tn = 1024
    return pl.pallas_call(
        _mod_kernel,
        grid=(6 * D // tn,),
        in_specs=[pl.BlockSpec((8, D), lambda j: (0, 0)),
                  pl.BlockSpec((D, tn), lambda j: (0, j)),
                  pl.BlockSpec((1, tn), lambda j: (0, j))],
        out_specs=pl.BlockSpec((8, tn), lambda j: (0, j)),
        out_shape=jax.ShapeDtypeStruct((8, 6 * D), F32),
        compiler_params=_cparams(("arbitrary",)),
        name="mod",
    )(c_all, w_mod, b_mod)


def _normmod_kernel(x_ref, ctx_ref, sh_ref, sc_ref, w_ref, o_ref):
    def go(src):
        y = _rms(src, w_ref[...])
        o_ref[0] = (y * (1.0 + sc_ref[0]) + sh_ref[0]).astype(BF16)

    i = pl.program_id(1)

    @pl.when(i == 0)
    def _():
        go(ctx_ref[0])

    @pl.when(i > 0)
    def _():
        go(x_ref[0])


def _normmod_call(x, ctx, mod3, ln_w):
    row = lambda b, i: jnp.where(i == 0, B, b)
    return pl.pallas_call(
        _normmod_kernel,
        grid=(B, TILES_A),
        in_specs=[pl.BlockSpec((1, ROW_T, D), lambda b, i: (b, jnp.maximum(i - 1, 0), 0)),
                  pl.BlockSpec((1, ROW_T, D), lambda b, i: (b, 0, 0)),
                  pl.BlockSpec((1, 1, D), lambda b, i: (row(b, i), 0, 0)),
                  pl.BlockSpec((1, 1, D), lambda b, i: (row(b, i), 0, 1)),
                  pl.BlockSpec((1, D), lambda b, i: (0, 0))],
        out_specs=pl.BlockSpec((1, ROW_T, D), lambda b, i: (b, i, 0)),
        out_shape=jax.ShapeDtypeStruct((B, NA, D), BF16),
        compiler_params=_cparams(("arbitrary", "arbitrary")),
        name="normmod",
    )(x, ctx, mod3, mod3, ln_w)


def _matmul_kernel(a_ref, w_ref, o_ref):
    o_ref[...] = jnp.dot(a_ref[...], w_ref[...], preferred_element_type=F32).astype(o_ref.dtype)


def _matmul_call(a, w, out_dtype, name):
    rows, k = a.shape
    cols = w.shape[1]
    return pl.pallas_call(
        _matmul_kernel,
        grid=(rows // MM_TM, cols // MM_TN),
        in_specs=[pl.BlockSpec((MM_TM, k), lambda i, j: (i, 0)),
                  pl.BlockSpec((k, MM_TN), lambda i, j: (0, j))],
        out_specs=pl.BlockSpec((MM_TM, MM_TN), lambda i, j: (i, j)),
        out_shape=jax.ShapeDtypeStruct((rows, cols), out_dtype),
        compiler_params=_cparams(("arbitrary", "arbitrary")),
        name=name,
    )(a, w)


def _kvprep_kernel(p_ref, kvn_ref, w_ref, cs_ref, k_ref, v_ref):
    ckv = p_ref[:, Q_RANK:Q_RANK + KV_RANK].astype(F32)
    ckvn = _rms(ckv, kvn_ref[...]).astype(BF16)
    kv = jnp.dot(ckvn, w_ref[...], preferred_element_type=F32)
    a = p_ref[:, Q_RANK + KV_RANK:Q_RANK + KV_RANK + LANES].astype(F32) * cs_ref[...]
    r = a + pltpu.roll(a, ROPE, axis=1)
    lane = lax.broadcasted_iota(I32, r.shape, 1)
    r = jnp.where(lane < ROPE, r, 0.0).astype(BF16)
    for h in range(H):
        k_ref[0, h, :, 0:HW] = kv[:, h * HW:(h + 1) * HW].astype(BF16)
        k_ref[0, h, :, HW:2 * HW] = r
        v_ref[0, h] = kv[:, (H + h) * HW:(H + h + 1) * HW].astype(BF16)


def _kvprep_call(p, kv_norm, w_kv, cs_all):
    return pl.pallas_call(
        _kvprep_kernel,
        grid=(B, TILES_A),
        in_specs=[pl.BlockSpec((ROW_T, 1024), lambda b, i: (b * TILES_A + i, 0)),
                  pl.BlockSpec((1, KV_RANK), lambda b, i: (0, 0)),
                  pl.BlockSpec((KV_RANK, 2 * H * HW), lambda b, i: (0, 0)),
                  pl.BlockSpec((ROW_T, LANES), lambda b, i: (i, 0))],
        out_specs=[pl.BlockSpec((1, H, ROW_T, 2 * HW), lambda b, i: (b, 0, i, 0)),
                   pl.BlockSpec((1, H, ROW_T, HW), lambda b, i: (b, 0, i, 0))],
        out_shape=[jax.ShapeDtypeStruct((B, H, NA, 2 * HW), BF16),
                   jax.ShapeDtypeStruct((B, H, NA, HW), BF16)],
        compiler_params=_cparams(("arbitrary", "arbitrary")),
        name="kvprep",
    )(p, kv_norm, w_kv, cs_all)


def _qprep_kernel(p_ref, qn_ref, w_ref, cs_ref, q_ref):
    cq = p_ref[:, 0:Q_RANK].astype(F32)
    cqn = _rms(cq, qn_ref[...]).astype(BF16)
    q = jnp.dot(cqn, w_ref[...], preferred_element_type=F32)
    cs = cs_ref[...]
    for h in range(H):
        q_ref[0, h, :, 0:HW] = (q[:, 2 * h * HW:(2 * h + 1) * HW] * QK_SCALE).astype(BF16)
        a = q[:, (2 * h + 1) * HW:(2 * h + 2) * HW] * cs
        q_ref[0, h, :, HW:2 * HW] = (a + pltpu.roll(a, ROPE, axis=1)).astype(BF16)


def _qprep_call(p, q_norm, w_q, cs_q):
    return pl.pallas_call(
        _qprep_kernel,
        grid=(B, TILES_L),
        in_specs=[pl.BlockSpec((ROW_T, 1024), lambda b, i: (b * TILES_A + 1 + i, 0)),
                  pl.BlockSpec((1, Q_RANK), lambda b, i: (0, 0)),
                  pl.BlockSpec((Q_RANK, 2 * H * HW), lambda b, i: (0, 0)),
                  pl.BlockSpec((ROW_T, LANES), lambda b, i: (i, 0))],
        out_specs=pl.BlockSpec((1, H, ROW_T, 2 * HW), lambda b, i: (b, 0, i, 0)),
        out_shape=jax.ShapeDtypeStruct((B, H, N, 2 * HW), BF16),
        compiler_params=_cparams(("arbitrary", "arbitrary")),
        name="qprep",
    )(p, q_norm, w_q, cs_q)


ATT_TQ = 256


def _attn_kernel(q_ref, k_ref, v_ref, o_ref):
    s = lax.dot_general(q_ref[0, 0], k_ref[0, 0], (((1,), (1,)), ((), ())),
                        preferred_element_type=F32)
    m = jnp.max(s, axis=-1, keepdims=True)
    p = jnp.exp(s - m)
    l = jnp.sum(p, axis=-1, keepdims=True)
    o = jnp.dot(p.astype(BF16), v_ref[0, 0], preferred_element_type=F32)
    o_ref[...] = (o * (1.0 / l)).astype(BF16)


def _attn_call(q, k, v):
    return pl.pallas_call(
        _attn_kernel,
        grid=(B, H, N // ATT_TQ),
        in_specs=[pl.BlockSpec((1, 1, ATT_TQ, 2 * HW), lambda b, h, i: (b, h, i, 0)),
                  pl.BlockSpec((1, 1, NA, 2 * HW), lambda b, h, i: (b, h, 0, 0)),
                  pl.BlockSpec((1, 1, NA, HW), lambda b, h, i: (b, h, 0, 0))],
        out_specs=pl.BlockSpec((ATT_TQ, HW), lambda b, h, i: (b * (N // ATT_TQ) + i, h)),
        out_shape=jax.ShapeDtypeStruct((T, H * HW), BF16),
        compiler_params=_cparams(("arbitrary", "arbitrary", "arbitrary")),
        name="attention",
    )(q, k, v)


def _hgrn_tables(reverse):
    c = HG_C
    t = np.arange(c)[:, None]
    u = np.arange(c)[None, :]
    cum = (u <= t).astype(np.float32)
    masks = [(u == t)]
    for m in HG_LEVELS:
        blk_t = t // m
        masks.append(((blk_t % 2) == 1) & ((u // m) == blk_t - 1))
    masks = np.stack(masks).astype(np.float32)
    if reverse:
        cum = cum[::-1, ::-1]
        masks = masks[:, ::-1, ::-1]
    return jnp.asarray(cum, dtype=BF16), jnp.asarray(masks.reshape(-1, c), dtype=F32)


def _pair_boundary(c_ref, cval, m, reverse):
    def rows_of(r_of_group):
        return jnp.concatenate([jnp.broadcast_to(c_ref[r_of_group(gi):r_of_group(gi) + 1, :], (SUBLANES, HW))
                                for gi in range(HG_C // SUBLANES)], axis=0)

    if m >= 4:
        def boundary(gi):
            pair = (SUBLANES * gi // (2 * m)) * 2 * m
            return pair + m if reverse else pair + m - 1
        return rows_of(boundary)
    row = lax.broadcasted_iota(I32, (HG_C, HW), 0)
    if m == 2:
        first = rows_of(lambda gi: SUBLANES * gi + (2 if reverse else 1))
        second = rows_of(lambda gi: SUBLANES * gi + (6 if reverse else 5))
        return jnp.where((row & 7) < 4, first, second)
    if reverse:
        return jnp.where((row & 1) == 0, pltpu.roll(cval, HG_C - 1, 0), cval)
    return jnp.where((row & 1) == 1, pltpu.roll(cval, 1, 0), cval)


def _hgrn_head(h, hq_ref, hi_ref, fr_ref, lb_ref, cum_ref, mk_ref, s_ref, o_ref, c_ref, reverse):
    c = HG_C
    col = pl.ds(pl.multiple_of(h * HW, HW), HW)
    xq = hq_ref[:, col].astype(F32)
    q = xq * _sigmoid(xq)
    lb = lb_ref[:, col]
    f = lb + (1.0 - lb) * _sigmoid(fr_ref[:, col])
    g = jnp.log(f)
    k = 1.0 - f
    v = hi_ref[:, col]
    g_hi = g.astype(BF16)
    g_lo = (g - g_hi.astype(F32)).astype(BF16)
    cum = cum_ref[...]
    cs = jnp.dot(cum, g_hi, preferred_element_type=F32) + jnp.dot(cum, g_lo, preferred_element_type=F32)
    c_ref[...] = cs
    total_row = 0 if reverse else c - 1
    c_tot = c_ref[total_row:total_row + 1, :]

    nt = (((1,), (1,)), ((), ()))
    st = s_ref[h]
    o = lax.dot_general((q * jnp.exp(cs)).astype(BF16), st.astype(BF16), nt, preferred_element_type=F32)
    sc = lax.dot_general(q.astype(BF16), k.astype(BF16), nt, preferred_element_type=F32) * mk_ref[0:c, :]
    for li, m in enumerate(HG_LEVELS):
        w = jnp.exp(-jnp.abs(cs - _pair_boundary(c_ref, cs, m, reverse)))
        sc = sc + (lax.dot_general((q * w).astype(BF16), (k * w).astype(BF16), nt, preferred_element_type=F32)
                   * mk_ref[(li + 1) * c:(li + 2) * c, :])
    o = o + jnp.dot(sc.astype(BF16), v, preferred_element_type=F32)
    o_ref[:, col] = o.astype(BF16)
    kend = (k * jnp.exp(c_tot - cs)).astype(BF16)
    s_ref[h] = st * jnp.exp(c_tot) + lax.dot_general(v, kend, (((0,), (0,)), ((), ())), preferred_element_type=F32)


HG_UNROLL = 2


def _hgrn_kernel(hq_f, hi_f, fr_f, hq_b, hi_b, fr_b, lbf_ref, lbb_ref, cumf_ref, mkf_ref, cumb_ref, mkb_ref,
                 of_ref, ob_ref, sf_ref, sb_ref, c_s):
    @pl.when(pl.program_id(1) == 0)
    def _():
        sf_ref[...] = jnp.zeros_like(sf_ref)
        sb_ref[...] = jnp.zeros_like(sb_ref)

    def body(i, carry):
        for hh in range(HG_UNROLL):
            h = i * HG_UNROLL + hh
            _hgrn_head(h, hq_f, hi_f, fr_f, lbf_ref, cumf_ref, mkf_ref, sf_ref, of_ref, c_s.at[2 * hh], False)
            _hgrn_head(h, hq_b, hi_b, fr_b, lbb_ref, cumb_ref, mkb_ref, sb_ref, ob_ref, c_s.at[2 * hh + 1], True)
        return carry

    lax.fori_loop(0, H // HG_UNROLL, body, 0)


def _hgrn_call(p, fraw, lb_f, lb_b):
    cum_f, mk_f = _hgrn_tables(False)
    cum_b, mk_b = _hgrn_tables(True)
    wide = H * HW

    def cf(b, s):
        return b * HG_STEPS + s

    def cb(b, s):
        return b * HG_STEPS + jnp.where(s < HG_CTX, HG_CTX - 1 - s, HG_STEPS - 1 + HG_CTX - s)

    const = lambda shape: pl.BlockSpec(shape, lambda b, s: (0, 0))
    return pl.pallas_call(
        _hgrn_kernel,
        grid=(B, HG_STEPS),
        in_specs=[pl.BlockSpec((HG_C, wide), lambda b, s: (cf(b, s), 1)),
                  pl.BlockSpec((HG_C, wide), lambda b, s: (cf(b, s), 2)),
                  pl.BlockSpec((HG_C, wide), lambda b, s: (cf(b, s), 0)),
                  pl.BlockSpec((HG_C, wide), lambda b, s: (cb(b, s), 1)),
                  pl.BlockSpec((HG_C, wide), lambda b, s: (cb(b, s), 2)),
                  pl.BlockSpec((HG_C, wide), lambda b, s: (cb(b, s), 1)),
                  const((1, wide)), const((1, wide)),
                  const(cum_f.shape), const(mk_f.shape), const(cum_b.shape), const(mk_b.shape)],
        out_specs=[pl.BlockSpec((HG_C, wide), lambda b, s: (cf(b, s), 0)),
                   pl.BlockSpec((HG_C, wide), lambda b, s: (cb(b, s), 0))],
        out_shape=[jax.ShapeDtypeStruct((TA, wide), BF16), jax.ShapeDtypeStruct((TA, wide), BF16)],
        scratch_shapes=[pltpu.VMEM((H, HW, HW), F32), pltpu.VMEM((H, HW, HW), F32),
                        pltpu.VMEM((2 * HG_UNROLL, HG_C, HW), F32)],
        compiler_params=_cparams(("arbitrary", "arbitrary")),
        name="hgrn",
    )(p, p, fraw, p, p, fraw, lb_f, lb_b, cum_f, mk_f, cum_b, mk_b)


def _pack_rows(val, dst_ref, row0, rows):
    for s in range(PACK_W):
        lo = val[:, 2 * s * LANES:(2 * s + 1) * LANES]
        hi = val[:, (2 * s + 1) * LANES:(2 * s + 2) * LANES]
        w = pltpu.pack_elementwise([lo, hi], packed_dtype=BF16)
        dst_ref[pl.ds(row0 * PACK_W + s, rows, stride=PACK_W), :] = w


def _unpack_word(w):
    lo = pltpu.unpack_elementwise(w, index=0, packed_dtype=BF16, unpacked_dtype=F32)
    hi = pltpu.unpack_elementwise(w, index=1, packed_dtype=BF16, unpacked_dtype=F32)
    return lo, hi


def _mixout_kernel(att_ref, of_ref, ob_ref, hg_ref, gn_ref, wa_ref, wr_ref, x_ref, gt_ref, shf_ref, scf_ref,
                   lnp_ref, lnf_ref, rw_ref, rb_ref,
                   x1_ref, h2w_ref, ids_ref, gates_ref, cnt_ref, carry_ref):
    first = (pl.program_id(0) == 0) & (pl.program_id(1) == 0)

    @pl.when(first)
    def _():
        carry_ref[...] = jnp.zeros_like(carry_ref)

    o = of_ref[...].astype(F32) + ob_ref[...].astype(F32)
    gn = gn_ref[...]
    recs = []
    for h in range(H):
        oh = o[:, h * HW:(h + 1) * HW]
        gate = hg_ref[:, h * HW:(h + 1) * HW].astype(F32)
        recs.append((_rms(oh, gn) * (gate * _sigmoid(gate))).astype(BF16))
    rec = jnp.concatenate(recs, axis=-1)
    mix = (jnp.dot(att_ref[...], wa_ref[...], preferred_element_type=F32)
           + jnp.dot(rec, wr_ref[...], preferred_element_type=F32))
    x1 = x_ref[0] + gt_ref[0] * _rms(mix, lnp_ref[...])
    x1_ref[0] = x1
    h2 = _rms(x1, lnf_ref[...]) * (1.0 + scf_ref[0]) + shf_ref[0]
    _pack_rows(h2, h2w_ref, 0, ROW_T)

    logits = jnp.dot(h2, rw_ref[...], precision=lax.Precision.HIGHEST, preferred_element_type=F32) + rb_ref[...]
    lane = lax.broadcasted_iota(I32, logits.shape, 1)
    neg = jnp.float32(-jnp.inf)
    lg = jnp.where(lane < NE, logits, neg)
    tops, idxs = [], []
    for _ in range(TOPK):
        m = jnp.max(lg, axis=-1, keepdims=True)
        idx = jnp.min(jnp.where(lg == m, lane, LANES), axis=-1, keepdims=True)
        tops.append(m)
        idxs.append(idx)
        lg = jnp.where(lane == idx, neg, lg)
    exps = [jnp.exp(tk - tops[0]) for tk in tops]
    inv = 1.0 / (exps[0] + exps[1] + exps[2] + exps[3])
    onehot = jnp.zeros(logits.shape, F32)
    for idx in idxs:
        onehot = onehot + jnp.where(lane == idx, 1.0, 0.0)
    r = lax.broadcasted_iota(I32, (ROW_T, ROW_T), 0)
    cidx = lax.broadcasted_iota(I32, (ROW_T, ROW_T), 1)
    lower = jnp.where(cidx < r, 1.0, 0.0).astype(BF16)
    prefix = jnp.dot(lower, onehot.astype(BF16), preferred_element_type=F32) + carry_ref[...]
    ids = jnp.zeros(logits.shape, I32)
    gates = jnp.zeros(logits.shape, F32)
    for kk in range(TOPK):
        rank = jnp.sum(jnp.where(lane == idxs[kk], prefix, 0.0), axis=-1, keepdims=True).astype(I32)
        ids = jnp.where(lane == kk, idxs[kk], ids)
        ids = jnp.where(lane == TOPK + kk, rank, ids)
        gates = jnp.where(lane == kk, exps[kk] * inv, gates)
    ids_ref[...] = ids
    gates_ref[...] = gates
    carry = carry_ref[...] + jnp.sum(onehot, axis=0, keepdims=True)
    carry_ref[...] = carry
    cnt_ref[...] = carry


def _mixout_call(att, o_f, o_b, p, g_norm_t, w_a, w_r, x, mod3, ln_post, ln_ffn, r_w, r_b):
    wide = H * HW
    lat = lambda b, i: b * TILES_A + 1 + i
    tok = lambda b, i: b * TILES_L + i
    const = lambda shape: pl.BlockSpec(shape, lambda b, i: tuple(0 for _ in shape))
    return pl.pallas_call(
        _mixout_kernel,
        grid=(B, TILES_L),
        in_specs=[pl.BlockSpec((ROW_T, wide), lambda b, i: (tok(b, i), 0)),
                  pl.BlockSpec((ROW_T, wide), lambda b, i: (lat(b, i), 0)),
                  pl.BlockSpec((ROW_T, wide), lambda b, i: (lat(b, i), 0)),
                  pl.BlockSpec((ROW_T, wide), lambda b, i: (lat(b, i), 3)),
                  const((1, HW)), const((wide, D)), const((wide, D)),
                  pl.BlockSpec((1, ROW_T, D), lambda b, i: (b, i, 0)),
                  pl.BlockSpec((1, 1, D), lambda b, i: (b, 0, 2)),
                  pl.BlockSpec((1, 1, D), lambda b, i: (b, 0, 3)),
                  pl.BlockSpec((1, 1, D), lambda b, i: (b, 0, 4)),
                  const((1, D)), const((1, D)), const((D, LANES)), const((1, LANES))],
        out_specs=[pl.BlockSpec((1, ROW_T, D), lambda b, i: (b, i, 0)),
                   pl.BlockSpec((ROW_T * PACK_W, LANES), lambda b, i: (tok(b, i), 0)),
                   pl.BlockSpec((ROW_T, LANES), lambda b, i: (tok(b, i), 0)),
                   pl.BlockSpec((ROW_T, LANES), lambda b, i: (tok(b, i), 0)),
                   const((1, LANES))],
        out_shape=[jax.ShapeDtypeStruct((B, N, D), F32),
                   jax.ShapeDtypeStruct((T * PACK_W, LANES), I32),
                   jax.ShapeDtypeStruct((T, LANES), I32),
                   jax.ShapeDtypeStruct((T, LANES), F32),
                   jax.ShapeDtypeStruct((1, LANES), F32)],
        scratch_shapes=[pltpu.VMEM((1, LANES), F32)],
        compiler_params=_cparams(("arbitrary", "arbitrary")),
        name="mixout_router",
    )(att, o_f, o_b, p, g_norm_t, w_a, w_r, x, mod3, mod3, mod3, ln_post, ln_ffn, r_w, r_b)


def _dispatch_copy(src_vmem, dst_hbm, sem, src_row, dst_row, rows):
    return pltpu.make_async_copy(src_vmem.at[pl.ds(src_row * PACK_W, rows * PACK_W)],
                                 dst_hbm.at[pl.ds(dst_row * PACK_W, rows * PACK_W)], sem)


def _dispatch_kernel(dest_ref, h2w_ref, init_hbm, xs_hbm, sem):
    del init_hbm
    base = pl.program_id(0) * DISP_T

    def issue(t, carry):
        for kk in range(TOPK):
            _dispatch_copy(h2w_ref, xs_hbm, sem, t, dest_ref[(base + t) * TOPK + kk], 1).start()
        return carry

    lax.fori_loop(0, DISP_T, issue, 0)
    for _ in range(TOPK):
        _dispatch_copy(h2w_ref, xs_hbm, sem, 0, 0, DISP_T).wait()


def _dispatch_call(dest, h2w, xs_init):
    gs = pltpu.PrefetchScalarGridSpec(
        num_scalar_prefetch=1, grid=(T // DISP_T,),
        in_specs=[pl.BlockSpec((DISP_T * PACK_W, LANES), lambda i, d: (i, 0)),
                  pl.BlockSpec(memory_space=pl.ANY)],
        out_specs=pl.BlockSpec(memory_space=pl.ANY),
        scratch_shapes=[pltpu.SemaphoreType.DMA(())])
    return pl.pallas_call(
        _dispatch_kernel, grid_spec=gs,
        out_shape=jax.ShapeDtypeStruct(xs_init.shape, xs_init.dtype),
        input_output_aliases={2: 0},
        compiler_params=_cparams(("arbitrary",)),
        name="moe_dispatch",
    )(dest, h2w, xs_init)


def _expert_kernel(sbe_ref, sbn_ref, sbb_ref, xs_ref, wg_ref, wl_ref, bg_ref, bl_ref, wd_ref, bd_ref,
                   out_ref, x_s, act_s, outw_s, wg_s, wl_s, wd_s):
    del sbe_ref, sbb_ref
    s = pl.program_id(0)
    j = pl.program_id(1)
    nblk = sbn_ref[s]

    @pl.when((j == 0) & (nblk > 0))
    def _():
        for i in range(SB_SUBS):
            @pl.when(i < nblk)
            def _():
                for w in range(PACK_W):
                    lo, hi = _unpack_word(xs_ref[pl.ds(i * SUB * PACK_W + w, SUB, stride=PACK_W), :])
                    x_s[i * SUB:(i + 1) * SUB, 2 * w * LANES:(2 * w + 1) * LANES] = lo.astype(BF16)
                    x_s[i * SUB:(i + 1) * SUB, (2 * w + 1) * LANES:(2 * w + 2) * LANES] = hi.astype(BF16)

    @pl.when((j < N1) & (nblk > 0))
    def _():
        wg_s[...] = wg_ref[0].astype(BF16)
        wl_s[...] = wl_ref[0].astype(BF16)
        col = pl.ds(pl.multiple_of(j * FF_T, FF_T), FF_T)
        for n in range(1, SB_SUBS + 1):
            @pl.when(nblk == n)
            def _():
                x = x_s[0:n * SUB, :]
                gu = jnp.dot(x, wg_s[...], preferred_element_type=F32) + bg_ref[0]
                li = jnp.dot(x, wl_s[...], preferred_element_type=F32) + bl_ref[0]
                glu = jnp.minimum(gu, LIMIT)
                lin = jnp.clip(li, -LIMIT, LIMIT)
                act = glu * _sigmoid(ALPHA * glu) * (lin + 1.0)
                act_s[0:n * SUB, col] = act.astype(BF16)

    @pl.when((j >= N1) & (nblk > 0))
    def _():
        wd_s[...] = wd_ref[0].astype(BF16)
        for n in range(1, SB_SUBS + 1):
            @pl.when(nblk == n)
            def _():
                acc = jnp.dot(act_s[0:n * SUB, :], wd_s[...], preferred_element_type=F32) + bd_ref[0]
                for half in range(FF_T // (2 * LANES)):
                    lo = acc[:, 2 * half * LANES:(2 * half + 1) * LANES]
                    hi = acc[:, (2 * half + 1) * LANES:(2 * half + 2) * LANES]
                    outw_s[(j - N1) * (FF_T // (2 * LANES)) + half, 0:n * SUB, :] = (
                        pltpu.pack_elementwise([lo, hi], packed_dtype=BF16))

    @pl.when((j == N1 + N2 - 1) & (nblk > 0))
    def _():
        for i in range(SB_SUBS):
            @pl.when(i < nblk)
            def _():
                for w in range(PACK_W):
                    out_ref[pl.ds(i * SUB * PACK_W + w, SUB, stride=PACK_W), :] = outw_s[w, i * SUB:(i + 1) * SUB, :]

            @pl.when(i >= nblk)
            def _():
                out_ref[i * SUB * PACK_W:(i + 1) * SUB * PACK_W, :] = jnp.zeros((SUB * PACK_W, LANES), I32)


def _expert_call(sb_e, sb_n, sb_b, xs, w_gu, b_gu, w_dn, b_dn):
    jg = lambda j: jnp.minimum(j, N1 - 1)
    jd = lambda j: jnp.maximum(j - N1, 0)
    gs = pltpu.PrefetchScalarGridSpec(
        num_scalar_prefetch=3, grid=(S_MAX, N1 + N2),
        in_specs=[pl.BlockSpec((SB_R * PACK_W, LANES), lambda s, j, e, n, b: (b[s], 0)),
                  pl.BlockSpec((1, D, FF_T), lambda s, j, e, n, b: (e[s], 0, jg(j))),
                  pl.BlockSpec((1, D, FF_T), lambda s, j, e, n, b: (e[s], 0, N1 + jg(j))),
                  pl.BlockSpec((1, 1, FF_T), lambda s, j, e, n, b: (e[s], 0, jg(j))),
                  pl.BlockSpec((1, 1, FF_T), lambda s, j, e, n, b: (e[s], 0, N1 + jg(j))),
                  pl.BlockSpec((1, DFF, FF_T), lambda s, j, e, n, b: (e[s], 0, jd(j))),
                  pl.BlockSpec((1, 1, FF_T), lambda s, j, e, n, b: (e[s], 0, jd(j)))],
        out_specs=pl.BlockSpec((SB_R * PACK_W, LANES), lambda s, j, e, n, b: (b[s], 0)),
        scratch_shapes=[pltpu.VMEM((SB_R, D), BF16),
                        pltpu.VMEM((SB_R, DFF), BF16),
                        pltpu.VMEM((PACK_W, SB_R, LANES), I32),
                        pltpu.VMEM((D, FF_T), BF16),
                        pltpu.VMEM((D, FF_T), BF16),
                        pltpu.VMEM((DFF, FF_T), BF16)])
    return pl.pallas_call(
        _expert_kernel, grid_spec=gs,
        out_shape=jax.ShapeDtypeStruct(xs.shape, xs.dtype),
        input_output_aliases={3: 0},
        compiler_params=_cparams(("arbitrary", "arbitrary")),
        name="moe_experts",
    )(sb_e, sb_n, sb_b, xs, w_gu, w_gu, b_gu, b_gu, w_dn, b_dn)


def _combine_copy(src_hbm, buf, sem, src_row, dst_row, rows):
    return pltpu.make_async_copy(src_hbm.at[pl.ds(src_row * PACK_W, rows * PACK_W)],
                                 buf.at[pl.ds(dst_row * PACK_W, rows * PACK_W)], sem)


def _combine_kernel(dest_ref, ys_hbm, gates_ref, x1_ref, gt_ref, ln_ref, o_ref, buf, y_s, sem):
    base = (pl.program_id(0) * TILES_L + pl.program_id(1)) * ROW_T

    def issue(t, carry):
        for kk in range(TOPK):
            _combine_copy(ys_hbm, buf, sem, dest_ref[(base + t) * TOPK + kk], kk * ROW_T + t, 1).start()
        return carry

    lax.fori_loop(0, ROW_T, issue, 0)
    _combine_copy(ys_hbm, buf, sem, 0, 0, ROW_T * TOPK).wait()

    gates = gates_ref[...]
    gk = [jnp.broadcast_to(gates[:, kk:kk + 1], (ROW_T, LANES)) for kk in range(TOPK)]
    for w in range(PACK_W):
        acc_lo = jnp.zeros((ROW_T, LANES), F32)
        acc_hi = jnp.zeros((ROW_T, LANES), F32)
        for kk in range(TOPK):
            lo, hi = _unpack_word(buf[pl.ds(kk * ROW_T * PACK_W + w, ROW_T, stride=PACK_W), :])
            acc_lo = acc_lo + gk[kk] * lo
            acc_hi = acc_hi + gk[kk] * hi
        y_s[:, 2 * w * LANES:(2 * w + 1) * LANES] = acc_lo
        y_s[:, (2 * w + 1) * LANES:(2 * w + 2) * LANES] = acc_hi
    o_ref[0] = x1_ref[0] + gt_ref[0] * _rms(y_s[...], ln_ref[...])


def _combine_call(dest, ys, gates, x1, mod3, ln_post):
    tok = lambda b, i: b * TILES_L + i
    gs = pltpu.PrefetchScalarGridSpec(
        num_scalar_prefetch=1, grid=(B, TILES_L),
        in_specs=[pl.BlockSpec(memory_space=pl.ANY),
                  pl.BlockSpec((ROW_T, LANES), lambda b, i, d: (tok(b, i), 0)),
                  pl.BlockSpec((1, ROW_T, D), lambda b, i, d: (b, i, 0)),
                  pl.BlockSpec((1, 1, D), lambda b, i, d: (b, 0, 5)),
                  pl.BlockSpec((1, D), lambda b, i, d: (0, 0))],
        out_specs=pl.BlockSpec((1, ROW_T, D), lambda b, i, d: (b, i, 0)),
        scratch_shapes=[pltpu.VMEM((TOPK * ROW_T * PACK_W, LANES), I32),
                        pltpu.VMEM((ROW_T, D), F32),
                        pltpu.SemaphoreType.DMA(())])
    return pl.pallas_call(
        _combine_kernel, grid_spec=gs,
        out_shape=jax.ShapeDtypeStruct((B, N, D), F32),
        compiler_params=_cparams(("arbitrary", "arbitrary")),
        name="moe_combine",
    )(dest, ys, gates, x1, mod3, ln_post)


def _rot_cols(w):
    half = ROPE // 2
    return jnp.concatenate([-w[..., half:], w[..., :half]], axis=-1)


def _rope_tables():
    t = np.arange(N)
    r = (t // GRID_W).astype(np.float64)
    cl = (t % GRID_W).astype(np.float64)
    n_freq = ROPE // 4
    inv = ROPE_THETA ** (-np.arange(n_freq, dtype=np.float64) / n_freq)
    ang = np.concatenate([r[:, None] * inv, cl[:, None] * inv], axis=-1)
    cos, sin = np.cos(ang), np.sin(ang)
    cs = np.concatenate([cos, cos, sin, sin], axis=-1).astype(np.float32)
    ctx = np.concatenate([np.ones((M, ROPE), np.float32), np.zeros((M, ROPE), np.float32)], axis=-1)
    return jnp.asarray(np.concatenate([ctx, cs], axis=0)), jnp.asarray(cs * QK_SCALE)


def _routing_tables(ids, counts_f):
    e = ids[:, :TOPK]
    rank = ids[:, TOPK:2 * TOPK]
    counts = counts_f[0, :NE].astype(I32)
    nsb = (counts + SB_R - 1) // SB_R
    sb_end = jnp.cumsum(nsb)
    sb_start = sb_end - nsb
    onehot = e[..., None] == jnp.arange(NE, dtype=I32)
    dest = jnp.sum(jnp.where(onehot, sb_start * SB_R, 0), axis=-1) + rank
    n_active = sb_end[-1]
    s_idx = jnp.arange(S_MAX, dtype=I32)
    s_eff = jnp.minimum(s_idx, n_active - 1)
    sb_e = jnp.minimum(jnp.searchsorted(sb_end, s_eff, side="right"), NE - 1).astype(I32)
    local = s_eff - sb_start[sb_e]
    valid = jnp.clip(counts[sb_e] - local * SB_R, 0, SB_R)
    sb_n = jnp.where(s_idx < n_active, (valid + SUB - 1) // SUB, 0).astype(I32)
    return dest.reshape(-1).astype(I32), sb_e, sb_n, s_eff.astype(I32)


def kernel(x, c, ctx, c_ctx, w_mod, b_mod, ln_mix_pre, ln_mix_post, ln_ffn_pre, ln_ffn_post, w_in, q_norm, kv_norm,
           w_uq, w_ukv, hgrn_lb, g_norm, w_out, router_w, router_b, w_gate_up, b_gate_up, w_down, b_down):
    c_all = jnp.concatenate([c, c_ctx[None, :], jnp.zeros((8 - B - 1, D), F32)], axis=0)
    mod = _mod_call(c_all, w_mod[0], b_mod)
    mod3 = mod.reshape(8, 1, 6 * D)

    w = w_in[0]
    o_q, o_kv, o_kr = 0, Q_RANK, Q_RANK + KV_RANK
    o_hq = o_kr + ROPE
    wide = H * HW
    o_ff, o_fb, o_hi, o_hg = o_hq + wide, o_hq + 2 * wide, o_hq + 3 * wide, o_hq + 4 * wide
    w_kr = w[:, o_kr:o_kr + ROPE]
    w_p = jnp.concatenate([w[:, o_q:o_kr], w_kr, _rot_cols(w_kr), jnp.zeros((D, LANES), F32),
                           w[:, o_hq:o_hq + wide], w[:, o_hi:o_hi + wide], w[:, o_hg:o_hg + wide]],
                          axis=1).astype(BF16)
    w_f = w[:, o_ff:o_ff + 2 * wide].astype(BF16)
    wq = w_uq[0]
    wq_r = wq[..., HW:]
    w_q = jnp.concatenate([wq[..., :HW], wq_r, _rot_cols(wq_r)], axis=-1).reshape(Q_RANK, 2 * wide).astype(BF16)
    wkv = w_ukv[0]
    w_kv = jnp.concatenate([wkv[..., :HW].reshape(KV_RANK, wide), wkv[..., HW:].reshape(KV_RANK, wide)],
                           axis=1).astype(BF16)
    w_o = w_out[0].astype(BF16)
    cs_all, cs_q = _rope_tables()
    lb = jnp.cumsum(jax.nn.softmax(hgrn_lb.astype(F32), axis=1), axis=1)
    r_w = jnp.concatenate([router_w[0], jnp.zeros((D, LANES - NE), F32)], axis=1)
    r_b = jnp.concatenate([router_b[0], jnp.zeros((LANES - NE,), F32)])[None, :]

    h_all = _normmod_call(x, ctx, mod3, ln_mix_pre)
    h_flat = h_all.reshape(TA, D)
    p = _matmul_call(h_flat, w_p, BF16, "in_proj")
    fraw = _matmul_call(h_flat, w_f, F32, "in_proj_forget")
    k_all, v_all = _kvprep_call(p, kv_norm, w_kv, cs_all)
    q_all = _qprep_call(p, q_norm, w_q, cs_q)
    att = _attn_call(q_all, k_all, v_all)
    o_f, o_b = _hgrn_call(p, fraw, lb[0, 0][None, :], lb[1, 0][None, :])

    g_norm_t = g_norm
    x1, h2w, ids, gates, counts = _mixout_call(att, o_f, o_b, p, g_norm_t, w_o[:wide], w_o[wide:], x, mod3,
                                               ln_mix_post, ln_ffn_pre, r_w, r_b)

    dest, sb_e, sb_n, sb_b = _routing_tables(ids, counts)
    xs = _dispatch_call(dest, h2w, jnp.zeros((S_MAX * SB_R * PACK_W, LANES), I32))
    ys = _expert_call(sb_e, sb_n, sb_b, xs, w_gate_up[0], b_gate_up[0][:, None, :], w_down[0], b_down[0][:, None, :])
    return _combine_call(dest, ys, gates, x1, mod3, ln_ffn_post)
```

```python
import functools

import numpy as np
import jax
import jax.numpy as jnp
from jax import lax
from jax.experimental import pallas as pl
from jax.experimental.pallas import tpu as pltpu

F32 = jnp.float32
BF16 = jnp.bfloat16
I32 = jnp.int32

D = 2048
B = 4
N = 4096
M = 256
NA = N + M
T = B * N
TA = B * NA
EPS = 1e-6
H = 8
HW = 128
Q_RANK = 512
KV_RANK = 256
ROPE = 64
QK_SCALE = float((128 + ROPE) ** -0.5)
GRID_W = 64
ROPE_THETA = 10000.0
NE = 32
TOPK = 4
DFF = 2048
LIMIT = 7.0
ALPHA = 1.702

LANES = 128
SUBLANES = 8
VMEM_LIMIT = 56 * 1024 * 1024

ROW_T = 256
TILES_A = NA // ROW_T
TILES_L = N // ROW_T
MM_TM = 1024
MM_TN = 1024
HG_C = 128
HG_STEPS = NA // HG_C
HG_CTX = M // HG_C
HG_LEVELS = (64, 32, 16, 8, 4, 2, 1)
SUB = 256
SB_R = 1024
SB_SUBS = SB_R // SUB
S_MAX = T * TOPK // SB_R + NE
FF_T = 256
N1 = DFF // FF_T
N2 = D // FF_T
PACK_W = D // 2 // LANES
DISP_T = 512


def _sigmoid(x):
    return 1.0 / (1.0 + jnp.exp(-x))


def _rms(x, w):
    ms = jnp.mean(x * x, axis=-1, keepdims=True)
    return x * lax.rsqrt(ms + EPS) * w


def _cparams(sem, vmem=VMEM_LIMIT):
    return pltpu.CompilerParams(dimension_semantics=sem, vmem_limit_bytes=vmem)


def _mod_kernel(c_ref, w_ref, b_ref, o_ref):
    c = c_ref[...]
    a = (c * _sigmoid(c)).astype(BF16)
    o_ref[...] = jnp.dot(a, w_ref[...].astype(BF16), preferred_element_type=F32) + b_ref[...]


def _mod_call(c_all, w_mod, b_mod):
    tn = 1024
    return pl.pallas_call(
        _mod_kernel,
        grid=(6 * D // tn,),
        in_specs=[pl.BlockSpec((8, D), lambda j: (0, 0)),
                  pl.BlockSpec((D, tn), lambda j: (0, j)),
                  pl.BlockSpec((1, tn), lambda j: (0, j))],
        out_specs=pl.BlockSpec((8, tn), lambda j: (0, j)),
        out_shape=jax.ShapeDtypeStruct((8, 6 * D), F32),
        compiler_params=_cparams(("arbitrary",)),
        name="mod",
    )(c_all, w_mod, b_mod)


def _normmod_kernel(x_ref, ctx_ref, sh_ref, sc_ref, w_ref, o_ref):
    def go(src):
        y = _rms(src, w_ref[...])
        o_ref[0] = (y * (1.0 + sc_ref[0]) + sh_ref[0]).astype(BF16)

    i = pl.program_id(1)

    @pl.when(i == 0)
    def _():
        go(ctx_ref[0])

    @pl.when(i > 0)
    def _():
        go(x_ref[0])


def _normmod_call(x, ctx, mod3, ln_w):
    row = lambda b, i: jnp.where(i == 0, B, b)
    return pl.pallas_call(
        _normmod_kernel,
        grid=(B, TILES_A),
        in_specs=[pl.BlockSpec((1, ROW_T, D), lambda b, i: (b, jnp.maximum(i - 1, 0), 0)),
                  pl.BlockSpec((1, ROW_T, D), lambda b, i: (b, 0, 0)),
                  pl.BlockSpec((1, 1, D), lambda b, i: (row(b, i), 0, 0)),
                  pl.BlockSpec((1, 1, D), lambda b, i: (row(b, i), 0, 1)),
                  pl.BlockSpec((1, D), lambda b, i: (0, 0))],
        out_specs=pl.BlockSpec((1, ROW_T, D), lambda b, i: (b, i, 0)),
        out_shape=jax.ShapeDtypeStruct((B, NA, D), BF16),
        compiler_params=_cparams(("arbitrary", "arbitrary")),
        name="normmod",
    )(x, ctx, mod3, mod3, ln_w)


def _matmul_kernel(a_ref, w_ref, o_ref):
    o_ref[...] = jnp.dot(a_ref[...], w_ref[...], preferred_element_type=F32).astype(o_ref.dtype)


def _matmul_call(a, w, out_dtype, name):
    rows, k = a.shape
    cols = w.shape[1]
    return pl.pallas_call(
        _matmul_kernel,
        grid=(rows // MM_TM, cols // MM_TN),
        in_specs=[pl.BlockSpec((MM_TM, k), lambda i, j: (i, 0)),
                  pl.BlockSpec((k, MM_TN), lambda i, j: (0, j))],
        out_specs=pl.BlockSpec((MM_TM, MM_TN), lambda i, j: (i, j)),
        out_shape=jax.ShapeDtypeStruct((rows, cols), out_dtype),
        compiler_params=_cparams(("arbitrary", "arbitrary")),
        name=name,
    )(a, w)


def _kvprep_kernel(p_ref, kvn_ref, w_ref, cs_ref, k_ref, v_ref):
    ckv = p_ref[:, Q_RANK:Q_RANK + KV_RANK].astype(F32)
    ckvn = _rms(ckv, kvn_ref[...]).astype(BF16)
    kv = jnp.dot(ckvn, w_ref[...], preferred_element_type=F32)
    a = p_ref[:, Q_RANK + KV_RANK:Q_RANK + KV_RANK + LANES].astype(F32) * cs_ref[...]
    r = a + pltpu.roll(a, ROPE, axis=1)
    lane = lax.broadcasted_iota(I32, r.shape, 1)
    r = jnp.where(lane < ROPE, r, 0.0).astype(BF16)
    for h in range(H):
        k_ref[0, h, :, 0:HW] = kv[:, h * HW:(h + 1) * HW].astype(BF16)
        k_ref[0, h, :, HW:2 * HW] = r
        v_ref[0, h] = kv[:, (H + h) * HW:(H + h + 1) * HW].astype(BF16)


def _kvprep_call(p, kv_norm, w_kv, cs_all):
    return pl.pallas_call(
        _kvprep_kernel,
        grid=(B, TILES_A),
        in_specs=[pl.BlockSpec((ROW_T, 1024), lambda b, i: (b * TILES_A + i, 0)),
                  pl.BlockSpec((1, KV_RANK), lambda b, i: (0, 0)),
                  pl.BlockSpec((KV_RANK, 2 * H * HW), lambda b, i: (0, 0)),
                  pl.BlockSpec((ROW_T, LANES), lambda b, i: (i, 0))],
        out_specs=[pl.BlockSpec((1, H, ROW_T, 2 * HW), lambda b, i: (b, 0, i, 0)),
                   pl.BlockSpec((1, H, ROW_T, HW), lambda b, i: (b, 0, i, 0))],
        out_shape=[jax.ShapeDtypeStruct((B, H, NA, 2 * HW), BF16),
                   jax.ShapeDtypeStruct((B, H, NA, HW), BF16)],
        compiler_params=_cparams(("arbitrary", "arbitrary")),
        name="kvprep",
    )(p, kv_norm, w_kv, cs_all)


def _qprep_kernel(p_ref, qn_ref, w_ref, cs_ref, q_ref):
    cq = p_ref[:, 0:Q_RANK].astype(F32)
    cqn = _rms(cq, qn_ref[...]).astype(BF16)
    q = jnp.dot(cqn, w_ref[...], preferred_element_type=F32)
    cs = cs_ref[...]
    for h in range(H):
        q_ref[0, h, :, 0:HW] = (q[:, 2 * h * HW:(2 * h + 1) * HW] * QK_SCALE).astype(BF16)
        a = q[:, (2 * h + 1) * HW:(2 * h + 2) * HW] * cs
        q_ref[0, h, :, HW:2 * HW] = (a + pltpu.roll(a, ROPE, axis=1)).astype(BF16)


def _qprep_call(p, q_norm, w_q, cs_q):
    return pl.pallas_call(
        _qprep_kernel,
        grid=(B, TILES_L),
        in_specs=[pl.BlockSpec((ROW_T, 1024), lambda b, i: (b * TILES_A + 1 + i, 0)),
                  pl.BlockSpec((1, Q_RANK), lambda b, i: (0, 0)),
                  pl.BlockSpec((Q_RANK, 2 * H * HW), lambda b, i: (0, 0)),
                  pl.BlockSpec((ROW_T, LANES), lambda b, i: (i, 0))],
        out_specs=pl.BlockSpec((1, H, ROW_T, 2 * HW), lambda b, i: (b, 0, i, 0)),
        out_shape=jax.ShapeDtypeStruct((B, H, N, 2 * HW), BF16),
        compiler_params=_cparams(("arbitrary", "arbitrary")),
        name="qprep",
    )(p, q_norm, w_q, cs_q)


ATT_TQ = 512


def _attn_kernel(q_ref, k_ref, v_ref, o_ref):
    s = lax.dot_general(q_ref[0, 0], k_ref[0, 0], (((1,), (1,)), ((), ())),
                        preferred_element_type=F32)
    m = jnp.max(s, axis=-1, keepdims=True)
    p = jnp.exp(s - m)
    l = jnp.sum(p, axis=-1, keepdims=True)
    o = jnp.dot(p.astype(BF16), v_ref[0, 0], preferred_element_type=F32)
    o_ref[...] = (o * (1.0 / l)).astype(BF16)


def _attn_call(q, k, v):
    return pl.pallas_call(
        _attn_kernel,
        grid=(B, H, N // ATT_TQ),
        in_specs=[pl.BlockSpec((1, 1, ATT_TQ, 2 * HW), lambda b, h, i: (b, h, i, 0)),
                  pl.BlockSpec((1, 1, NA, 2 * HW), lambda b, h, i: (b, h, 0, 0)),
                  pl.BlockSpec((1, 1, NA, HW), lambda b, h, i: (b, h, 0, 0))],
        out_specs=pl.BlockSpec((ATT_TQ, HW), lambda b, h, i: (b * (N // ATT_TQ) + i, h)),
        out_shape=jax.ShapeDtypeStruct((T, H * HW), BF16),
        compiler_params=_cparams(("arbitrary", "arbitrary", "arbitrary")),
        name="attention",
    )(q, k, v)


def _hgrn_tables(reverse):
    c = HG_C
    t = np.arange(c)[:, None]
    u = np.arange(c)[None, :]
    cum = (u <= t).astype(np.float32)
    masks = [(u == t)]
    for m in HG_LEVELS:
        blk_t = t // m
        masks.append(((blk_t % 2) == 1) & ((u // m) == blk_t - 1))
    masks = np.stack(masks).astype(np.float32)
    if reverse:
        cum = cum[::-1, ::-1]
        masks = masks[:, ::-1, ::-1]
    return jnp.asarray(cum, dtype=BF16), jnp.asarray(masks.reshape(-1, c), dtype=F32)


def _pair_boundary(c_ref, cval, m, reverse):
    def rows_of(r_of_group):
        return jnp.concatenate([jnp.broadcast_to(c_ref[r_of_group(gi):r_of_group(gi) + 1, :], (SUBLANES, HW))
                                for gi in range(HG_C // SUBLANES)], axis=0)

    if m >= 4:
        def boundary(gi):
            pair = (SUBLANES * gi // (2 * m)) * 2 * m
            return pair + m if reverse else pair + m - 1
        return rows_of(boundary)
    row = lax.broadcasted_iota(I32, (HG_C, HW), 0)
    if m == 2:
        first = rows_of(lambda gi: SUBLANES * gi + (2 if reverse else 1))
        second = rows_of(lambda gi: SUBLANES * gi + (6 if reverse else 5))
        return jnp.where((row & 7) < 4, first, second)
    if reverse:
        return jnp.where((row & 1) == 0, pltpu.roll(cval, HG_C - 1, 0), cval)
    return jnp.where((row & 1) == 1, pltpu.roll(cval, 1, 0), cval)


def _hgrn_head(h, hq_ref, hi_ref, fr_ref, lb_ref, cum_ref, mk_ref, s_ref, o_ref, c_ref, reverse):
    c = HG_C
    col = pl.ds(pl.multiple_of(h * HW, HW), HW)
    xq = hq_ref[:, col].astype(F32)
    q = xq * _sigmoid(xq)
    lb = lb_ref[:, col]
    f = lb + (1.0 - lb) * _sigmoid(fr_ref[:, col])
    g = jnp.log(f)
    k = 1.0 - f
    v = hi_ref[:, col]
    g_hi = g.astype(BF16)
    g_lo = (g - g_hi.astype(F32)).astype(BF16)
    cum = cum_ref[...]
    cs = jnp.dot(cum, g_hi, preferred_element_type=F32) + jnp.dot(cum, g_lo, preferred_element_type=F32)
    c_ref[...] = cs
    total_row = 0 if reverse else c - 1
    c_tot = c_ref[total_row:total_row + 1, :]

    nt = (((1,), (1,)), ((), ()))
    st = s_ref[h]
    o = lax.dot_general((q * jnp.exp(cs)).astype(BF16), st.astype(BF16), nt, preferred_element_type=F32)
    sc = lax.dot_general(q.astype(BF16), k.astype(BF16), nt, preferred_element_type=F32) * mk_ref[0:c, :]
    for li, m in enumerate(HG_LEVELS):
        w = jnp.exp(-jnp.abs(cs - _pair_boundary(c_ref, cs, m, reverse)))
        sc = sc + (lax.dot_general((q * w).astype(BF16), (k * w).astype(BF16), nt, preferred_element_type=F32)
                   * mk_ref[(li + 1) * c:(li + 2) * c, :])
    o = o + jnp.dot(sc.astype(BF16), v, preferred_element_type=F32)
    o_ref[:, col] = o.astype(BF16)
    kend = (k * jnp.exp(c_tot - cs)).astype(BF16)
    s_ref[h] = st * jnp.exp(c_tot) + lax.dot_general(v, kend, (((0,), (0,)), ((), ())), preferred_element_type=F32)


HG_UNROLL = 2


def _hgrn_kernel(hq_f, hi_f, fr_f, hq_b, hi_b, fr_b, lbf_ref, lbb_ref, cumf_ref, mkf_ref, cumb_ref, mkb_ref,
                 of_ref, ob_ref, sf_ref, sb_ref, c_s):
    @pl.when(pl.program_id(1) == 0)
    def _():
        sf_ref[...] = jnp.zeros_like(sf_ref)
        sb_ref[...] = jnp.zeros_like(sb_ref)

    def body(i, carry):
        for hh in range(HG_UNROLL):
            h = i * HG_UNROLL + hh
            _hgrn_head(h, hq_f, hi_f, fr_f, lbf_ref, cumf_ref, mkf_ref, sf_ref, of_ref, c_s.at[2 * hh], False)
            _hgrn_head(h, hq_b, hi_b, fr_b, lbb_ref, cumb_ref, mkb_ref, sb_ref, ob_ref, c_s.at[2 * hh + 1], True)
        return carry

    lax.fori_loop(0, H // HG_UNROLL, body, 0)


def _hgrn_call(p, fraw, lb_f, lb_b):
    cum_f, mk_f = _hgrn_tables(False)
    cum_b, mk_b = _hgrn_tables(True)
    wide = H * HW

    def cf(b, s):
        return b * HG_STEPS + s

    def cb(b, s):
        return b * HG_STEPS + jnp.where(s < HG_CTX, HG_CTX - 1 - s, HG_STEPS - 1 + HG_CTX - s)

    const = lambda shape: pl.BlockSpec(shape, lambda b, s: (0, 0))
    return pl.pallas_call(
        _hgrn_kernel,
        grid=(B, HG_STEPS),
        in_specs=[pl.BlockSpec((HG_C, wide), lambda b, s: (cf(b, s), 1)),
                  pl.BlockSpec((HG_C, wide), lambda b, s: (cf(b, s), 2)),
                  pl.BlockSpec((HG_C, wide), lambda b, s: (cf(b, s), 0)),
                  pl.BlockSpec((HG_C, wide), lambda b, s: (cb(b, s), 1)),
                  pl.BlockSpec((HG_C, wide), lambda b, s: (cb(b, s), 2)),
                  pl.BlockSpec((HG_C, wide), lambda b, s: (cb(b, s), 1)),
                  const((1, wide)), const((1, wide)),
                  const(cum_f.shape), const(mk_f.shape), const(cum_b.shape), const(mk_b.shape)],
        out_specs=[pl.BlockSpec((HG_C, wide), lambda b, s: (cf(b, s), 0)),
                   pl.BlockSpec((HG_C, wide), lambda b, s: (cb(b, s), 0))],
        out_shape=[jax.ShapeDtypeStruct((TA, wide), BF16), jax.ShapeDtypeStruct((TA, wide), BF16)],
        scratch_shapes=[pltpu.VMEM((H, HW, HW), F32), pltpu.VMEM((H, HW, HW), F32),
                        pltpu.VMEM((2 * HG_UNROLL, HG_C, HW), F32)],
        compiler_params=_cparams(("arbitrary", "arbitrary")),
        name="hgrn",
    )(p, p, fraw, p, p, fraw, lb_f, lb_b, cum_f, mk_f, cum_b, mk_b)


def _pack_rows(val, dst_ref, row0, rows):
    for s in range(PACK_W):
        lo = val[:, 2 * s * LANES:(2 * s + 1) * LANES]
        hi = val[:, (2 * s + 1) * LANES:(2 * s + 2) * LANES]
        w = pltpu.pack_elementwise([lo, hi], packed_dtype=BF16)
        dst_ref[pl.ds(row0 * PACK_W + s, rows, stride=PACK_W), :] = w


def _unpack_word(w):
    lo = pltpu.unpack_elementwise(w, index=0, packed_dtype=BF16, unpacked_dtype=F32)
    hi = pltpu.unpack_elementwise(w, index=1, packed_dtype=BF16, unpacked_dtype=F32)
    return lo, hi


def _mixout_kernel(att_ref, of_ref, ob_ref, hg_ref, gn_ref, wa_ref, wr_ref, x_ref, gt_ref, shf_ref, scf_ref,
                   lnp_ref, lnf_ref, rw_ref, rb_ref,
                   x1_ref, h2w_ref, ids_ref, gates_ref, cnt_ref, carry_ref):
    first = (pl.program_id(0) == 0) & (pl.program_id(1) == 0)

    @pl.when(first)
    def _():
        carry_ref[...] = jnp.zeros_like(carry_ref)

    o = of_ref[...].astype(F32) + ob_ref[...].astype(F32)
    gn = gn_ref[...]
    recs = []
    for h in range(H):
        oh = o[:, h * HW:(h + 1) * HW]
        gate = hg_ref[:, h * HW:(h + 1) * HW].astype(F32)
        recs.append((_rms(oh, gn) * (gate * _sigmoid(gate))).astype(BF16))
    rec = jnp.concatenate(recs, axis=-1)
    mix = (jnp.dot(att_ref[...], wa_ref[...], preferred_element_type=F32)
           + jnp.dot(rec, wr_ref[...], preferred_element_type=F32))
    x1 = x_ref[0] + gt_ref[0] * _rms(mix, lnp_ref[...])
    x1_ref[0] = x1
    h2 = _rms(x1, lnf_ref[...]) * (1.0 + scf_ref[0]) + shf_ref[0]
    _pack_rows(h2, h2w_ref, 0, ROW_T)

    h_hi = h2.astype(BF16)
    h_lo = (h2 - h_hi.astype(F32)).astype(BF16)
    hh = jnp.dot(h_hi, rw_ref[...], preferred_element_type=F32)
    logits = (hh[:, :LANES] + hh[:, LANES:] + jnp.dot(h_lo, rw_ref[:, :LANES], preferred_element_type=F32)
              + rb_ref[...])
    lane = lax.broadcasted_iota(I32, logits.shape, 1)
    neg = jnp.float32(-jnp.inf)
    lg = jnp.where(lane < NE, logits, neg)
    tops, idxs = [], []
    for _ in range(TOPK):
        m = jnp.max(lg, axis=-1, keepdims=True)
        idx = jnp.min(jnp.where(lg == m, lane, LANES), axis=-1, keepdims=True)
        tops.append(m)
        idxs.append(idx)
        lg = jnp.where(lane == idx, neg, lg)
    exps = [jnp.exp(tk - tops[0]) for tk in tops]
    inv = 1.0 / (exps[0] + exps[1] + exps[2] + exps[3])
    onehot = jnp.zeros(logits.shape, F32)
    for idx in idxs:
        onehot = onehot + jnp.where(lane == idx, 1.0, 0.0)
    r = lax.broadcasted_iota(I32, (ROW_T, ROW_T), 0)
    cidx = lax.broadcasted_iota(I32, (ROW_T, ROW_T), 1)
    lower = jnp.where(cidx < r, 1.0, 0.0).astype(BF16)
    prefix = jnp.dot(lower, onehot.astype(BF16), preferred_element_type=F32) + carry_ref[...]
    ids = jnp.zeros(logits.shape, I32)
    gates = jnp.zeros(logits.shape, F32)
    for kk in range(TOPK):
        rank = jnp.sum(jnp.where(lane == idxs[kk], prefix, 0.0), axis=-1, keepdims=True).astype(I32)
        ids = jnp.where(lane == kk, idxs[kk], ids)
        ids = jnp.where(lane == TOPK + kk, rank, ids)
        gates = jnp.where(lane == kk, exps[kk] * inv, gates)
    ids_ref[...] = ids
    gates_ref[...] = gates
    carry = carry_ref[...] + jnp.sum(onehot, axis=0, keepdims=True)
    carry_ref[...] = carry
    cnt_ref[...] = carry


def _mixout_call(att, o_f, o_b, p, g_norm_t, w_a, w_r, x, mod3, ln_post, ln_ffn, r_w, r_b):
    wide = H * HW
    lat = lambda b, i: b * TILES_A + 1 + i
    tok = lambda b, i: b * TILES_L + i
    const = lambda shape: pl.BlockSpec(shape, lambda b, i: tuple(0 for _ in shape))
    return pl.pallas_call(
        _mixout_kernel,
        grid=(B, TILES_L),
        in_specs=[pl.BlockSpec((ROW_T, wide), lambda b, i: (tok(b, i), 0)),
                  pl.BlockSpec((ROW_T, wide), lambda b, i: (lat(b, i), 0)),
                  pl.BlockSpec((ROW_T, wide), lambda b, i: (lat(b, i), 0)),
                  pl.BlockSpec((ROW_T, wide), lambda b, i: (lat(b, i), 3)),
                  const((1, HW)), const((wide, D)), const((wide, D)),
                  pl.BlockSpec((1, ROW_T, D), lambda b, i: (b, i, 0)),
                  pl.BlockSpec((1, 1, D), lambda b, i: (b, 0, 2)),
                  pl.BlockSpec((1, 1, D), lambda b, i: (b, 0, 3)),
                  pl.BlockSpec((1, 1, D), lambda b, i: (b, 0, 4)),
                  const((1, D)), const((1, D)), const((D, 2 * LANES)), const((1, LANES))],
        out_specs=[pl.BlockSpec((1, ROW_T, D), lambda b, i: (b, i, 0)),
                   pl.BlockSpec((ROW_T * PACK_W, LANES), lambda b, i: (tok(b, i), 0)),
                   pl.BlockSpec((ROW_T, LANES), lambda b, i: (tok(b, i), 0)),
                   pl.BlockSpec((ROW_T, LANES), lambda b, i: (tok(b, i), 0)),
                   const((1, LANES))],
        out_shape=[jax.ShapeDtypeStruct((B, N, D), F32),
                   jax.ShapeDtypeStruct((T * PACK_W, LANES), I32),
                   jax.ShapeDtypeStruct((T, LANES), I32),
                   jax.ShapeDtypeStruct((T, LANES), F32),
                   jax.ShapeDtypeStruct((1, LANES), F32)],
        scratch_shapes=[pltpu.VMEM((1, LANES), F32)],
        compiler_params=_cparams(("arbitrary", "arbitrary")),
        name="mixout_router",
    )(att, o_f, o_b, p, g_norm_t, w_a, w_r, x, mod3, mod3, mod3, ln_post, ln_ffn, r_w, r_b)


def _dispatch_copy(src_vmem, dst_hbm, sem, src_row, dst_row, rows):
    return pltpu.make_async_copy(src_vmem.at[pl.ds(src_row * PACK_W, rows * PACK_W)],
                                 dst_hbm.at[pl.ds(dst_row * PACK_W, rows * PACK_W)], sem)


def _dispatch_kernel(dest_ref, h2w_ref, init_hbm, xs_hbm, sem):
    del init_hbm
    base = pl.program_id(0) * DISP_T

    def issue(t, carry):
        for kk in range(TOPK):
            _dispatch_copy(h2w_ref, xs_hbm, sem, t, dest_ref[(base + t) * TOPK + kk], 1).start()
        return carry

    lax.fori_loop(0, DISP_T, issue, 0)
    for _ in range(TOPK):
        _dispatch_copy(h2w_ref, xs_hbm, sem, 0, 0, DISP_T).wait()


def _dispatch_call(dest, h2w, xs_init):
    gs = pltpu.PrefetchScalarGridSpec(
        num_scalar_prefetch=1, grid=(T // DISP_T,),
        in_specs=[pl.BlockSpec((DISP_T * PACK_W, LANES), lambda i, d: (i, 0)),
                  pl.BlockSpec(memory_space=pl.ANY)],
        out_specs=pl.BlockSpec(memory_space=pl.ANY),
        scratch_shapes=[pltpu.SemaphoreType.DMA(())])
    return pl.pallas_call(
        _dispatch_kernel, grid_spec=gs,
        out_shape=jax.ShapeDtypeStruct(xs_init.shape, xs_init.dtype),
        input_output_aliases={2: 0},
        compiler_params=_cparams(("arbitrary",)),
        name="moe_dispatch",
    )(dest, h2w, xs_init)


def _expert_kernel(sbe_ref, sbn_ref, sbb_ref, xs_ref, wg_ref, wl_ref, bg_ref, bl_ref, wd_ref, bd_ref,
                   out_ref, x_s, act_s, outw_s, wg_s, wl_s, wd_s):
    del sbe_ref, sbb_ref
    s = pl.program_id(0)
    j = pl.program_id(1)
    nblk = sbn_ref[s]

    @pl.when((j == 0) & (nblk > 0))
    def _():
        for i in range(SB_SUBS):
            @pl.when(i < nblk)
            def _():
                for w in range(PACK_W):
                    lo, hi = _unpack_word(xs_ref[pl.ds(i * SUB * PACK_W + w, SUB, stride=PACK_W), :])
                    x_s[i * SUB:(i + 1) * SUB, 2 * w * LANES:(2 * w + 1) * LANES] = lo.astype(BF16)
                    x_s[i * SUB:(i + 1) * SUB, (2 * w + 1) * LANES:(2 * w + 2) * LANES] = hi.astype(BF16)

    @pl.when((j < N1) & (nblk > 0))
    def _():
        wg_s[...] = wg_ref[0].astype(BF16)
        wl_s[...] = wl_ref[0].astype(BF16)
        col = pl.ds(pl.multiple_of(j * FF_T, FF_T), FF_T)
        for n in range(1, SB_SUBS + 1):
            @pl.when(nblk == n)
            def _():
                x = x_s[0:n * SUB, :]
                gu = jnp.dot(x, wg_s[...], preferred_element_type=F32) + bg_ref[0]
                li = jnp.dot(x, wl_s[...], preferred_element_type=F32) + bl_ref[0]
                glu = jnp.minimum(gu, LIMIT)
                lin = jnp.clip(li, -LIMIT, LIMIT)
                act = glu * _sigmoid(ALPHA * glu) * (lin + 1.0)
                act_s[0:n * SUB, col] = act.astype(BF16)

    @pl.when((j >= N1) & (nblk > 0))
    def _():
        wd_s[...] = wd_ref[0].astype(BF16)
        for n in range(1, SB_SUBS + 1):
            @pl.when(nblk == n)
            def _():
                acc = jnp.dot(act_s[0:n * SUB, :], wd_s[...], preferred_element_type=F32) + bd_ref[0]
                for half in range(FF_T // (2 * LANES)):
                    lo = acc[:, 2 * half * LANES:(2 * half + 1) * LANES]
                    hi = acc[:, (2 * half + 1) * LANES:(2 * half + 2) * LANES]
                    outw_s[(j - N1) * (FF_T // (2 * LANES)) + half, 0:n * SUB, :] = (
                        pltpu.pack_elementwise([lo, hi], packed_dtype=BF16))

    @pl.when((j == N1 + N2 - 1) & (nblk > 0))
    def _():
        for i in range(SB_SUBS):
            @pl.when(i < nblk)
            def _():
                for w in range(PACK_W):
                    out_ref[pl.ds(i * SUB * PACK_W + w, SUB, stride=PACK_W), :] = outw_s[w, i * SUB:(i + 1) * SUB, :]

            @pl.when(i >= nblk)
            def _():
                out_ref[i * SUB * PACK_W:(i + 1) * SUB * PACK_W, :] = jnp.zeros((SUB * PACK_W, LANES), I32)


def _expert_call(sb_e, sb_n, sb_b, xs, w_gu, b_gu, w_dn, b_dn):
    jg = lambda j: jnp.minimum(j, N1 - 1)
    jd = lambda j: jnp.maximum(j - N1, 0)
    gs = pltpu.PrefetchScalarGridSpec(
        num_scalar_prefetch=3, grid=(S_MAX, N1 + N2),
        in_specs=[pl.BlockSpec((SB_R * PACK_W, LANES), lambda s, j, e, n, b: (b[s], 0)),
                  pl.BlockSpec((1, D, FF_T), lambda s, j, e, n, b: (e[s], 0, jg(j))),
                  pl.BlockSpec((1, D, FF_T), lambda s, j, e, n, b: (e[s], 0, N1 + jg(j))),
                  pl.BlockSpec((1, 1, FF_T), lambda s, j, e, n, b: (e[s], 0, jg(j))),
                  pl.BlockSpec((1, 1, FF_T), lambda s, j, e, n, b: (e[s], 0, N1 + jg(j))),
                  pl.BlockSpec((1, DFF, FF_T), lambda s, j, e, n, b: (e[s], 0, jd(j))),
                  pl.BlockSpec((1, 1, FF_T), lambda s, j, e, n, b: (e[s], 0, jd(j)))],
        out_specs=pl.BlockSpec((SB_R * PACK_W, LANES), lambda s, j, e, n, b: (b[s], 0)),
        scratch_shapes=[pltpu.VMEM((SB_R, D), BF16),
                        pltpu.VMEM((SB_R, DFF), BF16),
                        pltpu.VMEM((PACK_W, SB_R, LANES), I32),
                        pltpu.VMEM((D, FF_T), BF16),
                        pltpu.VMEM((D, FF_T), BF16),
                        pltpu.VMEM((DFF, FF_T), BF16)])
    return pl.pallas_call(
        _expert_kernel, grid_spec=gs,
        out_shape=jax.ShapeDtypeStruct(xs.shape, xs.dtype),
        input_output_aliases={3: 0},
        compiler_params=_cparams(("arbitrary", "arbitrary")),
        name="moe_experts",
    )(sb_e, sb_n, sb_b, xs, w_gu, w_gu, b_gu, b_gu, w_dn, b_dn)


def _combine_copy(src_hbm, buf, sem, src_row, dst_row, rows):
    return pltpu.make_async_copy(src_hbm.at[pl.ds(src_row * PACK_W, rows * PACK_W)],
                                 buf.at[pl.ds(dst_row * PACK_W, rows * PACK_W)], sem)


def _combine_kernel(dest_ref, ys_hbm, gates_ref, x1_ref, gt_ref, ln_ref, o_ref, bufs, y_s, sems):
    step = pl.program_id(0) * TILES_L + pl.program_id(1)

    def issue(tile, slot):
        def body(t, carry):
            for kk in range(TOPK):
                _combine_copy(ys_hbm, bufs.at[slot], sems.at[slot], dest_ref[(tile * ROW_T + t) * TOPK + kk],
                              kk * ROW_T + t, 1).start()
            return carry
        lax.fori_loop(0, ROW_T, body, 0)

    @pl.when(step == 0)
    def _():
        issue(0, 0)

    @pl.when(step + 1 < B * TILES_L)
    def _():
        issue(step + 1, (step + 1) % 2)

    slot = step % 2
    buf = bufs.at[slot]
    _combine_copy(ys_hbm, buf, sems.at[slot], 0, 0, ROW_T * TOPK).wait()

    gates = gates_ref[...]
    gk = [jnp.broadcast_to(gates[:, kk:kk + 1], (ROW_T, LANES)) for kk in range(TOPK)]
    for w in range(PACK_W):
        acc_lo = jnp.zeros((ROW_T, LANES), F32)
        acc_hi = jnp.zeros((ROW_T, LANES), F32)
        for kk in range(TOPK):
            lo, hi = _unpack_word(buf[pl.ds(kk * ROW_T * PACK_W + w, ROW_T, stride=PACK_W), :])
            acc_lo = acc_lo + gk[kk] * lo
            acc_hi = acc_hi + gk[kk] * hi
        y_s[:, 2 * w * LANES:(2 * w + 1) * LANES] = acc_lo
        y_s[:, (2 * w + 1) * LANES:(2 * w + 2) * LANES] = acc_hi
    o_ref[0] = x1_ref[0] + gt_ref[0] * _rms(y_s[...], ln_ref[...])


def _combine_call(dest, ys, gates, x1, mod3, ln_post):
    tok = lambda b, i: b * TILES_L + i
    gs = pltpu.PrefetchScalarGridSpec(
        num_scalar_prefetch=1, grid=(B, TILES_L),
        in_specs=[pl.BlockSpec(memory_space=pl.ANY),
                  pl.BlockSpec((ROW_T, LANES), lambda b, i, d: (tok(b, i), 0)),
                  pl.BlockSpec((1, ROW_T, D), lambda b, i, d: (b, i, 0)),
                  pl.BlockSpec((1, 1, D), lambda b, i, d: (b, 0, 5)),
                  pl.BlockSpec((1, D), lambda b, i, d: (0, 0))],
        out_specs=pl.BlockSpec((1, ROW_T, D), lambda b, i, d: (b, i, 0)),
        scratch_shapes=[pltpu.VMEM((2, TOPK * ROW_T * PACK_W, LANES), I32),
                        pltpu.VMEM((ROW_T, D), F32),
                        pltpu.SemaphoreType.DMA((2,))])
    return pl.pallas_call(
        _combine_kernel, grid_spec=gs,
        out_shape=jax.ShapeDtypeStruct((B, N, D), F32),
        compiler_params=_cparams(("arbitrary", "arbitrary")),
        name="moe_combine",
    )(dest, ys, gates, x1, mod3, ln_post)


def _rot_cols(w):
    half = ROPE // 2
    return jnp.concatenate([-w[..., half:], w[..., :half]], axis=-1)


def _rope_tables():
    t = np.arange(N)
    r = (t // GRID_W).astype(np.float64)
    cl = (t % GRID_W).astype(np.float64)
    n_freq = ROPE // 4
    inv = ROPE_THETA ** (-np.arange(n_freq, dtype=np.float64) / n_freq)
    ang = np.concatenate([r[:, None] * inv, cl[:, None] * inv], axis=-1)
    cos, sin = np.cos(ang), np.sin(ang)
    cs = np.concatenate([cos, cos, sin, sin], axis=-1).astype(np.float32)
    ctx = np.concatenate([np.ones((M, ROPE), np.float32), np.zeros((M, ROPE), np.float32)], axis=-1)
    return jnp.asarray(np.concatenate([ctx, cs], axis=0)), jnp.asarray(cs * QK_SCALE)


def _routing_tables(ids, counts_f):
    e = ids[:, :TOPK]
    rank = ids[:, TOPK:2 * TOPK]
    counts = counts_f[0, :NE].astype(I32)
    nsb = (counts + SB_R - 1) // SB_R
    sb_end = jnp.cumsum(nsb)
    sb_start = sb_end - nsb
    onehot = e[..., None] == jnp.arange(NE, dtype=I32)
    dest = jnp.sum(jnp.where(onehot, sb_start * SB_R, 0), axis=-1) + rank
    n_active = sb_end[-1]
    s_idx = jnp.arange(S_MAX, dtype=I32)
    s_eff = jnp.minimum(s_idx, n_active - 1)
    sb_e = jnp.minimum(jnp.searchsorted(sb_end, s_eff, side="right"), NE - 1).astype(I32)
    local = s_eff - sb_start[sb_e]
    valid = jnp.clip(counts[sb_e] - local * SB_R, 0, SB_R)
    sb_n = jnp.where(s_idx < n_active, (valid + SUB - 1) // SUB, 0).astype(I32)
    return dest.reshape(-1).astype(I32), sb_e, sb_n, s_eff.astype(I32)


def kernel(x, c, ctx, c_ctx, w_mod, b_mod, ln_mix_pre, ln_mix_post, ln_ffn_pre, ln_ffn_post, w_in, q_norm, kv_norm,
           w_uq, w_ukv, hgrn_lb, g_norm, w_out, router_w, router_b, w_gate_up, b_gate_up, w_down, b_down):
    c_all = jnp.concatenate([c, c_ctx[None, :], jnp.zeros((8 - B - 1, D), F32)], axis=0)
    mod = _mod_call(c_all, w_mod[0], b_mod)
    mod3 = mod.reshape(8, 1, 6 * D)

    w = w_in[0]
    o_q, o_kv, o_kr = 0, Q_RANK, Q_RANK + KV_RANK
    o_hq = o_kr + ROPE
    wide = H * HW
    o_ff, o_fb, o_hi, o_hg = o_hq + wide, o_hq + 2 * wide, o_hq + 3 * wide, o_hq + 4 * wide
    w_kr = w[:, o_kr:o_kr + ROPE]
    w_p = jnp.concatenate([w[:, o_q:o_kr], w_kr, _rot_cols(w_kr), jnp.zeros((D, LANES), F32),
                           w[:, o_hq:o_hq + wide], w[:, o_hi:o_hi + wide], w[:, o_hg:o_hg + wide]],
                          axis=1).astype(BF16)
    w_f = w[:, o_ff:o_ff + 2 * wide].astype(BF16)
    wq = w_uq[0]
    wq_r = wq[..., HW:]
    w_q = jnp.concatenate([wq[..., :HW], wq_r, _rot_cols(wq_r)], axis=-1).reshape(Q_RANK, 2 * wide).astype(BF16)
    wkv = w_ukv[0]
    w_kv = jnp.concatenate([wkv[..., :HW].reshape(KV_RANK, wide), wkv[..., HW:].reshape(KV_RANK, wide)],
                           axis=1).astype(BF16)
    w_o = w_out[0].astype(BF16)
    cs_all, cs_q = _rope_tables()
    lb = jnp.cumsum(jax.nn.softmax(hgrn_lb.astype(F32), axis=1), axis=1)
    r_w = jnp.concatenate([router_w[0], jnp.zeros((D, LANES - NE), F32)], axis=1)
    r_w_hi = r_w.astype(BF16)
    r_w = jnp.concatenate([r_w_hi, (r_w - r_w_hi.astype(F32)).astype(BF16)], axis=1)
    r_b = jnp.concatenate([router_b[0], jnp.zeros((LANES - NE,), F32)])[None, :]

    h_all = _normmod_call(x, ctx, mod3, ln_mix_pre)
    h_flat = h_all.reshape(TA, D)
    p = _matmul_call(h_flat, w_p, BF16, "in_proj")
    fraw = _matmul_call(h_flat, w_f, F32, "in_proj_forget")
    k_all, v_all = _kvprep_call(p, kv_norm, w_kv, cs_all)
    q_all = _qprep_call(p, q_norm, w_q, cs_q)
    att = _attn_call(q_all, k_all, v_all)
    o_f, o_b = _hgrn_call(p, fraw, lb[0, 0][None, :], lb[1, 0][None, :])

    g_norm_t = g_norm
    x1, h2w, ids, gates, counts = _mixout_call(att, o_f, o_b, p, g_norm_t, w_o[:wide], w_o[wide:], x, mod3,
                                               ln_mix_post, ln_ffn_pre, r_w, r_b)

    dest, sb_e, sb_n, sb_b = _routing_tables(ids, counts)
    xs = _dispatch_call(dest, h2w, jnp.zeros((S_MAX * SB_R * PACK_W, LANES), I32))
    ys = _expert_call(sb_e, sb_n, sb_b, xs, w_gate_up[0], b_gate_up[0][:, None, :], w_down[0], b_down[0][:, None, :])
    return _combine_call(dest, ys, gates, x1, mod3, ln_ffn_post)
```

```python
import functools

import numpy as np
import jax
import jax.numpy as jnp
from jax import lax
from jax.experimental import pallas as pl
from jax.experimental.pallas import tpu as pltpu

F32 = jnp.float32
BF16 = jnp.bfloat16
I32 = jnp.int32

D = 2048
B = 4
N = 4096
M = 256
NA = N + M
T = B * N
TA = B * NA
EPS = 1e-6
H = 8
HW = 128
Q_RANK = 512
KV_RANK = 256
ROPE = 64
QK_SCALE = float((128 + ROPE) ** -0.5)
GRID_W = 64
ROPE_THETA = 10000.0
NE = 32
TOPK = 4
DFF = 2048
LIMIT = 7.0
ALPHA = 1.702
LOG2E = float(np.log2(np.e))

LANES = 128
SUBLANES = 8
VMEM_LIMIT = 56 * 1024 * 1024

ROW_T = 256
TILES_A = NA // ROW_T
TILES_L = N // ROW_T
MM_TM = 1024
MM_TN = 1024
HG_C = 128
HG_STEPS = NA // HG_C
HG_CTX = M // HG_C
HG_LEVELS = (64, 32, 16, 8, 4, 2, 1)
SUB = 256
N_SUBS = T * TOPK // SUB + NE
SB_SUBS = 10
SB_R = SB_SUBS * SUB
CHUNK_SUBS = 4
S_MAX = -(-N_SUBS // SB_SUBS) + NE
FF_T = 256
N1 = DFF // FF_T
N2 = D // FF_T
PACK_W = D // 2 // LANES
DISP_T = 512


def _sigmoid(x):
    return 1.0 / (1.0 + jnp.exp(-x))


def _rms(x, w):
    ms = jnp.mean(x * x, axis=-1, keepdims=True)
    return x * lax.rsqrt(ms + EPS) * w


def _cparams(sem, vmem=VMEM_LIMIT):
    return pltpu.CompilerParams(dimension_semantics=sem, vmem_limit_bytes=vmem)


def _mod_kernel(c_ref, w_ref, b_ref, o_ref):
    c = c_ref[...]
    a = (c * _sigmoid(c)).astype(BF16)
    o_ref[...] = jnp.dot(a, w_ref[...].astype(BF16), preferred_element_type=F32) + b_ref[...]


def _mod_call(c_all, w_mod, b_mod):
    tn = 1024
    return pl.pallas_call(
        _mod_kernel,
        grid=(6 * D // tn,),
        in_specs=[pl.BlockSpec((8, D), lambda j: (0, 0)),
                  pl.BlockSpec((D, tn), lambda j: (0, j)),
                  pl.BlockSpec((1, tn), lambda j: (0, j))],
        out_specs=pl.BlockSpec((8, tn), lambda j: (0, j)),
        out_shape=jax.ShapeDtypeStruct((8, 6 * D), F32),
        compiler_params=_cparams(("arbitrary",)),
        name="mod",
    )(c_all, w_mod, b_mod)


def _normmod_kernel(x_ref, ctx_ref, sh_ref, sc_ref, w_ref, o_ref):
    def go(src):
        y = _rms(src, w_ref[...])
        o_ref[0] = (y * (1.0 + sc_ref[0]) + sh_ref[0]).astype(BF16)

    i = pl.program_id(1)

    @pl.when(i == 0)
    def _():
        go(ctx_ref[0])

    @pl.when(i > 0)
    def _():
        go(x_ref[0])


def _normmod_call(x, ctx, mod3, ln_w):
    row = lambda b, i: jnp.where(i == 0, B, b)
    return pl.pallas_call(
        _normmod_kernel,
        grid=(B, TILES_A),
        in_specs=[pl.BlockSpec((1, ROW_T, D), lambda b, i: (b, jnp.maximum(i - 1, 0), 0)),
                  pl.BlockSpec((1, ROW_T, D), lambda b, i: (b, 0, 0)),
                  pl.BlockSpec((1, 1, D), lambda b, i: (row(b, i), 0, 0)),
                  pl.BlockSpec((1, 1, D), lambda b, i: (row(b, i), 0, 1)),
                  pl.BlockSpec((1, D), lambda b, i: (0, 0))],
        out_specs=pl.BlockSpec((1, ROW_T, D), lambda b, i: (b, i, 0)),
        out_shape=jax.ShapeDtypeStruct((B, NA, D), BF16),
        compiler_params=_cparams(("arbitrary", "arbitrary")),
        name="normmod",
    )(x, ctx, mod3, mod3, ln_w)


def _matmul_kernel(a_ref, w_ref, o_ref):
    o_ref[...] = jnp.dot(a_ref[...], w_ref[...], preferred_element_type=F32).astype(o_ref.dtype)


def _matmul_call(a, w, out_dtype, name):
    rows, k = a.shape
    cols = w.shape[1]
    return pl.pallas_call(
        _matmul_kernel,
        grid=(rows // MM_TM, cols // MM_TN),
        in_specs=[pl.BlockSpec((MM_TM, k), lambda i, j: (i, 0)),
                  pl.BlockSpec((k, MM_TN), lambda i, j: (0, j))],
        out_specs=pl.BlockSpec((MM_TM, MM_TN), lambda i, j: (i, j)),
        out_shape=jax.ShapeDtypeStruct((rows, cols), out_dtype),
        compiler_params=_cparams(("arbitrary", "arbitrary")),
        name=name,
    )(a, w)


def _kvprep_kernel(p_ref, kvn_ref, w_ref, cs_ref, k_ref, v_ref):
    ckv = p_ref[:, Q_RANK:Q_RANK + KV_RANK].astype(F32)
    ckvn = _rms(ckv, kvn_ref[...]).astype(BF16)
    kv = jnp.dot(ckvn, w_ref[...], preferred_element_type=F32)
    a = p_ref[:, Q_RANK + KV_RANK:Q_RANK + KV_RANK + LANES].astype(F32) * cs_ref[...]
    r = a + pltpu.roll(a, ROPE, axis=1)
    lane = lax.broadcasted_iota(I32, r.shape, 1)
    r = jnp.where(lane < ROPE, r, 0.0).astype(BF16)
    for h in range(H):
        k_ref[0, h, :, 0:HW] = kv[:, h * HW:(h + 1) * HW].astype(BF16)
        k_ref[0, h, :, HW:2 * HW] = r
        v_ref[0, h] = kv[:, (H + h) * HW:(H + h + 1) * HW].astype(BF16)


def _kvprep_call(p, kv_norm, w_kv, cs_all):
    return pl.pallas_call(
        _kvprep_kernel,
        grid=(B, TILES_A),
        in_specs=[pl.BlockSpec((ROW_T, 1024), lambda b, i: (b * TILES_A + i, 0)),
                  pl.BlockSpec((1, KV_RANK), lambda b, i: (0, 0)),
                  pl.BlockSpec((KV_RANK, 2 * H * HW), lambda b, i: (0, 0)),
                  pl.BlockSpec((ROW_T, LANES), lambda b, i: (i, 0))],
        out_specs=[pl.BlockSpec((1, H, ROW_T, 2 * HW), lambda b, i: (b, 0, i, 0)),
                   pl.BlockSpec((1, H, ROW_T, HW), lambda b, i: (b, 0, i, 0))],
        out_shape=[jax.ShapeDtypeStruct((B, H, NA, 2 * HW), BF16),
                   jax.ShapeDtypeStruct((B, H, NA, HW), BF16)],
        compiler_params=_cparams(("arbitrary", "arbitrary")),
        name="kvprep",
    )(p, kv_norm, w_kv, cs_all)


def _qprep_kernel(p_ref, qn_ref, w_ref, cs_ref, q_ref):
    cq = p_ref[:, 0:Q_RANK].astype(F32)
    cqn = _rms(cq, qn_ref[...]).astype(BF16)
    q = jnp.dot(cqn, w_ref[...], preferred_element_type=F32)
    cs = cs_ref[...]
    for h in range(H):
        q_ref[0, h, :, 0:HW] = (q[:, 2 * h * HW:(2 * h + 1) * HW] * QK_SCALE).astype(BF16)
        a = q[:, (2 * h + 1) * HW:(2 * h + 2) * HW] * cs
        q_ref[0, h, :, HW:2 * HW] = (a + pltpu.roll(a, ROPE, axis=1)).astype(BF16)


def _qprep_call(p, q_norm, w_q, cs_q):
    return pl.pallas_call(
        _qprep_kernel,
        grid=(B, TILES_L),
        in_specs=[pl.BlockSpec((ROW_T, 1024), lambda b, i: (b * TILES_A + 1 + i, 0)),
                  pl.BlockSpec((1, Q_RANK), lambda b, i: (0, 0)),
                  pl.BlockSpec((Q_RANK, 2 * H * HW), lambda b, i: (0, 0)),
                  pl.BlockSpec((ROW_T, LANES), lambda b, i: (i, 0))],
        out_specs=pl.BlockSpec((1, H, ROW_T, 2 * HW), lambda b, i: (b, 0, i, 0)),
        out_shape=jax.ShapeDtypeStruct((B, H, N, 2 * HW), BF16),
        compiler_params=_cparams(("arbitrary", "arbitrary")),
        name="qprep",
    )(p, q_norm, w_q, cs_q)


ATT_TQ = 256


def _attn_kernel(q_ref, k_ref, v_ref, o_ref):
    s = lax.dot_general(q_ref[0, 0], k_ref[0, 0], (((1,), (1,)), ((), ())),
                        preferred_element_type=F32)
    m = jnp.max(s, axis=-1, keepdims=True)
    p = jnp.exp(s - m)
    l = jnp.sum(p, axis=-1, keepdims=True)
    o = jnp.dot(p.astype(BF16), v_ref[0, 0], preferred_element_type=F32)
    o_ref[...] = (o * (1.0 / l)).astype(BF16)


def _attn_call(q, k, v):
    return pl.pallas_call(
        _attn_kernel,
        grid=(B, H, N // ATT_TQ),
        in_specs=[pl.BlockSpec((1, 1, ATT_TQ, 2 * HW), lambda b, h, i: (b, h, i, 0)),
                  pl.BlockSpec((1, 1, NA, 2 * HW), lambda b, h, i: (b, h, 0, 0)),
                  pl.BlockSpec((1, 1, NA, HW), lambda b, h, i: (b, h, 0, 0))],
        out_specs=pl.BlockSpec((ATT_TQ, HW), lambda b, h, i: (b * (N // ATT_TQ) + i, h)),
        out_shape=jax.ShapeDtypeStruct((T, H * HW), BF16),
        compiler_params=_cparams(("arbitrary", "arbitrary", "arbitrary")),
        name="attention",
    )(q, k, v)


def _hgrn_tables(reverse):
    c = HG_C
    t = np.arange(c)[:, None]
    u = np.arange(c)[None, :]
    cum = (u <= t).astype(np.float32)
    masks = [(u == t)]
    for m in HG_LEVELS:
        blk_t = t // m
        masks.append(((blk_t % 2) == 1) & ((u // m) == blk_t - 1))
    masks = np.stack(masks).astype(np.float32)
    if reverse:
        cum = cum[::-1, ::-1]
        masks = masks[:, ::-1, ::-1]
    return jnp.asarray(cum, dtype=BF16), jnp.asarray(masks.reshape(-1, c), dtype=F32)


def _pair_boundary(c_ref, cval, m, reverse):
    def rows_of(r_of_group):
        return jnp.concatenate([jnp.broadcast_to(c_ref[r_of_group(gi):r_of_group(gi) + 1, :], (SUBLANES, HW))
                                for gi in range(HG_C // SUBLANES)], axis=0)

    if m >= 4:
        def boundary(gi):
            pair = (SUBLANES * gi // (2 * m)) * 2 * m
            return pair + m if reverse else pair + m - 1
        return rows_of(boundary)
    row = lax.broadcasted_iota(I32, (HG_C, HW), 0)
    if m == 2:
        first = rows_of(lambda gi: SUBLANES * gi + (2 if reverse else 1))
        second = rows_of(lambda gi: SUBLANES * gi + (6 if reverse else 5))
        return jnp.where((row & 7) < 4, first, second)
    if reverse:
        return jnp.where((row & 1) == 0, pltpu.roll(cval, HG_C - 1, 0), cval)
    return jnp.where((row & 1) == 1, pltpu.roll(cval, 1, 0), cval)


def _hgrn_head(h, hq_ref, hi_ref, fr_ref, lb_ref, cum_ref, mk_ref, s_ref, o_ref, c_ref, reverse):
    c = HG_C
    col = pl.ds(pl.multiple_of(h * HW, HW), HW)
    xq = hq_ref[:, col].astype(F32)
    q = xq * _sigmoid(xq)
    lb = lb_ref[:, col]
    f = lb + (1.0 - lb) * _sigmoid(fr_ref[:, col])
    g = jnp.log(f)
    k = 1.0 - f
    v = hi_ref[:, col]
    g_hi = g.astype(BF16)
    g_lo = (g - g_hi.astype(F32)).astype(BF16)
    cum = cum_ref[...]
    cs = (jnp.dot(cum, g_hi, preferred_element_type=F32) + jnp.dot(cum, g_lo, preferred_element_type=F32)) * LOG2E
    c_ref[...] = cs
    total_row = 0 if reverse else c - 1
    c_tot = c_ref[total_row:total_row + 1, :]

    nt = (((1,), (1,)), ((), ()))
    st = s_ref[h]
    o = lax.dot_general((q * jnp.exp2(cs)).astype(BF16), st.astype(BF16), nt, preferred_element_type=F32)
    qb = q.astype(BF16)
    kb = k.astype(BF16)
    sc = lax.dot_general(qb, kb, nt, preferred_element_type=F32) * mk_ref[0:c, :]
    sign = jnp.uint32(0x80000000)
    for li, m in enumerate(HG_LEVELS):
        z = cs - _pair_boundary(c_ref, cs, m, reverse)
        w = jnp.exp2(pltpu.bitcast(pltpu.bitcast(z, jnp.uint32) | sign, F32)).astype(BF16)
        sc = sc + (lax.dot_general(qb * w, kb * w, nt, preferred_element_type=F32)
                   * mk_ref[(li + 1) * c:(li + 2) * c, :])
    o = o + jnp.dot(sc.astype(BF16), v, preferred_element_type=F32)
    o_ref[:, col] = o.astype(BF16)
    kend = (k * jnp.exp2(c_tot - cs)).astype(BF16)
    s_ref[h] = st * jnp.exp2(c_tot) + lax.dot_general(v, kend, (((0,), (0,)), ((), ())), preferred_element_type=F32)


HG_UNROLL = 4


def _hgrn_kernel(hq_f, hi_f, fr_f, hq_b, hi_b, fr_b, lbf_ref, lbb_ref, cumf_ref, mkf_ref, cumb_ref, mkb_ref,
                 of_ref, ob_ref, sf_ref, sb_ref, c_s):
    @pl.when(pl.program_id(1) == 0)
    def _():
        sf_ref[...] = jnp.zeros_like(sf_ref)
        sb_ref[...] = jnp.zeros_like(sb_ref)

    def body(i, carry):
        for hh in range(HG_UNROLL):
            h = i * HG_UNROLL + hh
            _hgrn_head(h, hq_f, hi_f, fr_f, lbf_ref, cumf_ref, mkf_ref, sf_ref, of_ref, c_s.at[2 * hh], False)
            _hgrn_head(h, hq_b, hi_b, fr_b, lbb_ref, cumb_ref, mkb_ref, sb_ref, ob_ref, c_s.at[2 * hh + 1], True)
        return carry

    lax.fori_loop(0, H // HG_UNROLL, body, 0)


def _hgrn_call(p, fraw, lb_f, lb_b):
    cum_f, mk_f = _hgrn_tables(False)
    cum_b, mk_b = _hgrn_tables(True)
    wide = H * HW

    def cf(b, s):
        return b * HG_STEPS + s

    def cb(b, s):
        return b * HG_STEPS + jnp.where(s < HG_CTX, HG_CTX - 1 - s, HG_STEPS - 1 + HG_CTX - s)

    const = lambda shape: pl.BlockSpec(shape, lambda b, s: (0, 0))
    return pl.pallas_call(
        _hgrn_kernel,
        grid=(B, HG_STEPS),
        in_specs=[pl.BlockSpec((HG_C, wide), lambda b, s: (cf(b, s), 1)),
                  pl.BlockSpec((HG_C, wide), lambda b, s: (cf(b, s), 2)),
                  pl.BlockSpec((HG_C, wide), lambda b, s: (cf(b, s), 0)),
                  pl.BlockSpec((HG_C, wide), lambda b, s: (cb(b, s), 1)),
                  pl.BlockSpec((HG_C, wide), lambda b, s: (cb(b, s), 2)),
                  pl.BlockSpec((HG_C, wide), lambda b, s: (cb(b, s), 1)),
                  const((1, wide)), const((1, wide)),
                  const(cum_f.shape), const(mk_f.shape), const(cum_b.shape), const(mk_b.shape)],
        out_specs=[pl.BlockSpec((HG_C, wide), lambda b, s: (cf(b, s), 0)),
                   pl.BlockSpec((HG_C, wide), lambda b, s: (cb(b, s), 0))],
        out_shape=[jax.ShapeDtypeStruct((TA, wide), BF16), jax.ShapeDtypeStruct((TA, wide), BF16)],
        scratch_shapes=[pltpu.VMEM((H, HW, HW), F32), pltpu.VMEM((H, HW, HW), F32),
                        pltpu.VMEM((2 * HG_UNROLL, HG_C, HW), F32)],
        compiler_params=_cparams(("arbitrary", "arbitrary")),
        name="hgrn",
    )(p, p, fraw, p, p, fraw, lb_f, lb_b, cum_f, mk_f, cum_b, mk_b)


def _pack_rows(val, dst_ref, row0, rows):
    for s in range(PACK_W):
        lo = val[:, 2 * s * LANES:(2 * s + 1) * LANES]
        hi = val[:, (2 * s + 1) * LANES:(2 * s + 2) * LANES]
        w = pltpu.pack_elementwise([lo, hi], packed_dtype=BF16)
        dst_ref[pl.ds(row0 * PACK_W + s, rows, stride=PACK_W), :] = w


def _unpack_word(w):
    lo = pltpu.unpack_elementwise(w, index=0, packed_dtype=BF16, unpacked_dtype=F32)
    hi = pltpu.unpack_elementwise(w, index=1, packed_dtype=BF16, unpacked_dtype=F32)
    return lo, hi


def _mixout_kernel(att_ref, of_ref, ob_ref, hg_ref, gn_ref, wa_ref, wr_ref, x_ref, gt_ref, shf_ref, scf_ref,
                   lnp_ref, lnf_ref, rw_ref, rb_ref,
                   x1_ref, h2w_ref, ids_ref, gates_ref, cnt_ref, carry_ref):
    first = (pl.program_id(0) == 0) & (pl.program_id(1) == 0)

    @pl.when(first)
    def _():
        carry_ref[...] = jnp.zeros_like(carry_ref)

    o = of_ref[...].astype(F32) + ob_ref[...].astype(F32)
    gn = gn_ref[...]
    recs = []
    for h in range(H):
        oh = o[:, h * HW:(h + 1) * HW]
        gate = hg_ref[:, h * HW:(h + 1) * HW].astype(F32)
        recs.append((_rms(oh, gn) * (gate * _sigmoid(gate))).astype(BF16))
    rec = jnp.concatenate(recs, axis=-1)
    mix = (jnp.dot(att_ref[...], wa_ref[...], preferred_element_type=F32)
           + jnp.dot(rec, wr_ref[...], preferred_element_type=F32))
    x1 = x_ref[0] + gt_ref[0] * _rms(mix, lnp_ref[...])
    x1_ref[0] = x1
    h2 = _rms(x1, lnf_ref[...]) * (1.0 + scf_ref[0]) + shf_ref[0]
    _pack_rows(h2, h2w_ref, 0, ROW_T)

    h_hi = h2.astype(BF16)
    h_lo = (h2 - h_hi.astype(F32)).astype(BF16)
    hh = jnp.dot(h_hi, rw_ref[...], preferred_element_type=F32)
    logits = (hh[:, :LANES] + hh[:, LANES:] + jnp.dot(h_lo, rw_ref[:, :LANES], preferred_element_type=F32)
              + rb_ref[...])
    lane = lax.broadcasted_iota(I32, logits.shape, 1)
    neg = jnp.float32(-jnp.inf)
    lg = jnp.where(lane < NE, logits, neg)
    tops, idxs = [], []
    for _ in range(TOPK):
        m = jnp.max(lg, axis=-1, keepdims=True)
        idx = jnp.min(jnp.where(lg == m, lane, LANES), axis=-1, keepdims=True)
        tops.append(m)
        idxs.append(idx)
        lg = jnp.where(lane == idx, neg, lg)
    exps = [jnp.exp(tk - tops[0]) for tk in tops]
    inv = 1.0 / (exps[0] + exps[1] + exps[2] + exps[3])
    onehot = jnp.zeros(logits.shape, F32)
    for idx in idxs:
        onehot = onehot + jnp.where(lane == idx, 1.0, 0.0)
    r = lax.broadcasted_iota(I32, (ROW_T, ROW_T), 0)
    cidx = lax.broadcasted_iota(I32, (ROW_T, ROW_T), 1)
    lower = jnp.where(cidx < r, 1.0, 0.0).astype(BF16)
    prefix = jnp.dot(lower, onehot.astype(BF16), preferred_element_type=F32) + carry_ref[...]
    ids = jnp.zeros(logits.shape, I32)
    gates = jnp.zeros(logits.shape, F32)
    for kk in range(TOPK):
        rank = jnp.sum(jnp.where(lane == idxs[kk], prefix, 0.0), axis=-1, keepdims=True).astype(I32)
        ids = jnp.where(lane == kk, idxs[kk], ids)
        ids = jnp.where(lane == TOPK + kk, rank, ids)
        gates = jnp.where(lane == kk, exps[kk] * inv, gates)
    ids_ref[...] = ids
    gates_ref[...] = gates
    carry = carry_ref[...] + jnp.sum(onehot, axis=0, keepdims=True)
    carry_ref[...] = carry
    cnt_ref[...] = carry


def _mixout_call(att, o_f, o_b, p, g_norm_t, w_a, w_r, x, mod3, ln_post, ln_ffn, r_w, r_b):
    wide = H * HW
    lat = lambda b, i: b * TILES_A + 1 + i
    tok = lambda b, i: b * TILES_L + i
    const = lambda shape: pl.BlockSpec(shape, lambda b, i: tuple(0 for _ in shape))
    return pl.pallas_call(
        _mixout_kernel,
        grid=(B, TILES_L),
        in_specs=[pl.BlockSpec((ROW_T, wide), lambda b, i: (tok(b, i), 0)),
                  pl.BlockSpec((ROW_T, wide), lambda b, i: (lat(b, i), 0)),
                  pl.BlockSpec((ROW_T, wide), lambda b, i: (lat(b, i), 0)),
                  pl.BlockSpec((ROW_T, wide), lambda b, i: (lat(b, i), 3)),
                  const((1, HW)), const((wide, D)), const((wide, D)),
                  pl.BlockSpec((1, ROW_T, D), lambda b, i: (b, i, 0)),
                  pl.BlockSpec((1, 1, D), lambda b, i: (b, 0, 2)),
                  pl.BlockSpec((1, 1, D), lambda b, i: (b, 0, 3)),
                  pl.BlockSpec((1, 1, D), lambda b, i: (b, 0, 4)),
                  const((1, D)), const((1, D)), const((D, 2 * LANES)), const((1, LANES))],
        out_specs=[pl.BlockSpec((1, ROW_T, D), lambda b, i: (b, i, 0)),
                   pl.BlockSpec((ROW_T * PACK_W, LANES), lambda b, i: (tok(b, i), 0)),
                   pl.BlockSpec((ROW_T, LANES), lambda b, i: (tok(b, i), 0)),
                   pl.BlockSpec((ROW_T, LANES), lambda b, i: (tok(b, i), 0)),
                   const((1, LANES))],
        out_shape=[jax.ShapeDtypeStruct((B, N, D), F32),
                   jax.ShapeDtypeStruct((T * PACK_W, LANES), I32),
                   jax.ShapeDtypeStruct((T, LANES), I32),
                   jax.ShapeDtypeStruct((T, LANES), F32),
                   jax.ShapeDtypeStruct((1, LANES), F32)],
        scratch_shapes=[pltpu.VMEM((1, LANES), F32)],
        compiler_params=_cparams(("arbitrary", "arbitrary")),
        name="mixout_router",
    )(att, o_f, o_b, p, g_norm_t, w_a, w_r, x, mod3, mod3, mod3, ln_post, ln_ffn, r_w, r_b)


def _dispatch_copy(src_vmem, dst_hbm, sem, src_row, dst_row, rows):
    return pltpu.make_async_copy(src_vmem.at[pl.ds(src_row * PACK_W, rows * PACK_W)],
                                 dst_hbm.at[pl.ds(dst_row * PACK_W, rows * PACK_W)], sem)


def _dispatch_kernel(dest_ref, h2w_ref, init_hbm, xs_hbm, sem):
    del init_hbm
    base = pl.program_id(0) * DISP_T

    def issue(t, carry):
        for kk in range(TOPK):
            _dispatch_copy(h2w_ref, xs_hbm, sem, t, dest_ref[(base + t) * TOPK + kk], 1).start()
        return carry

    lax.fori_loop(0, DISP_T, issue, 0)
    for _ in range(TOPK):
        _dispatch_copy(h2w_ref, xs_hbm, sem, 0, 0, DISP_T).wait()


def _dispatch_call(dest, h2w, xs_init):
    gs = pltpu.PrefetchScalarGridSpec(
        num_scalar_prefetch=1, grid=(T // DISP_T,),
        in_specs=[pl.BlockSpec((DISP_T * PACK_W, LANES), lambda i, d: (i, 0)),
                  pl.BlockSpec(memory_space=pl.ANY)],
        out_specs=pl.BlockSpec(memory_space=pl.ANY),
        scratch_shapes=[pltpu.SemaphoreType.DMA(())])
    return pl.pallas_call(
        _dispatch_kernel, grid_spec=gs,
        out_shape=jax.ShapeDtypeStruct(xs_init.shape, xs_init.dtype),
        input_output_aliases={2: 0},
        compiler_params=_cparams(("arbitrary",)),
        name="moe_dispatch",
    )(dest, h2w, xs_init)


def _sub_in_copy(xs_hbm, stage, sems, blk, slot):
    return pltpu.make_async_copy(xs_hbm.at[pl.ds(blk * (SUB * PACK_W), SUB * PACK_W)], stage.at[slot], sems.at[slot])


def _sub_out_copy(stage, out_hbm, sems, blk, slot):
    return pltpu.make_async_copy(stage.at[slot], out_hbm.at[pl.ds(blk * (SUB * PACK_W), SUB * PACK_W)], sems.at[slot])


def _for_row_chunks(nblk, fn):
    for c in range(-(-SB_SUBS // CHUNK_SUBS)):
        row0 = c * CHUNK_SUBS * SUB
        most = min(CHUNK_SUBS, SB_SUBS - c * CHUNK_SUBS)
        for r in range(1, most + 1):
            full = r == CHUNK_SUBS
            cond = (nblk >= c * CHUNK_SUBS + r) if full else (nblk == c * CHUNK_SUBS + r)
            pl.when(cond)(functools.partial(fn, row0, r * SUB))


def _expert_kernel(sbe_ref, sbn_ref, sbb_ref, xs_hbm, wg_ref, wl_ref, bg_ref, bl_ref, wd_ref, bd_ref,
                   out_hbm, x_s, act_s, outw_s, wg_s, wl_s, wd_s, stage_in, stage_out, sem_in, sem_out):
    del sbe_ref
    s = pl.program_id(0)
    j = pl.program_id(1)
    nblk = sbn_ref[s]
    blk0 = sbb_ref[s]

    @pl.when((j == 0) & (nblk > 0))
    def _():
        _sub_in_copy(xs_hbm, stage_in, sem_in, blk0, 0).start()
        for i in range(SB_SUBS):
            @pl.when(i < nblk)
            def _():
                if i + 1 < SB_SUBS:
                    @pl.when(i + 1 < nblk)
                    def _():
                        _sub_in_copy(xs_hbm, stage_in, sem_in, blk0 + i + 1, (i + 1) % 2).start()
                _sub_in_copy(xs_hbm, stage_in, sem_in, blk0 + i, i % 2).wait()
                for w in range(PACK_W):
                    lo, hi = _unpack_word(stage_in[i % 2, pl.ds(w, SUB, stride=PACK_W), :])
                    x_s[i * SUB:(i + 1) * SUB, 2 * w * LANES:(2 * w + 1) * LANES] = lo.astype(BF16)
                    x_s[i * SUB:(i + 1) * SUB, (2 * w + 1) * LANES:(2 * w + 2) * LANES] = hi.astype(BF16)

    @pl.when((j < N1) & (nblk > 0))
    def _():
        wg_s[...] = wg_ref[0].astype(BF16)
        wl_s[...] = wl_ref[0].astype(BF16)
        col = pl.ds(pl.multiple_of(j * FF_T, FF_T), FF_T)

        def gate_up(row0, rows):
            x = x_s[row0:row0 + rows, :]
            gu = jnp.dot(x, wg_s[...], preferred_element_type=F32) + bg_ref[0]
            li = jnp.dot(x, wl_s[...], preferred_element_type=F32) + bl_ref[0]
            glu = jnp.minimum(gu, LIMIT)
            lin = jnp.clip(li, -LIMIT, LIMIT)
            act = glu * _sigmoid(ALPHA * glu) * (lin + 1.0)
            act_s[row0:row0 + rows, col] = act.astype(BF16)

        _for_row_chunks(nblk, gate_up)

    @pl.when((j >= N1) & (nblk > 0))
    def _():
        wd_s[...] = wd_ref[0].astype(BF16)

        def down(row0, rows):
            acc = jnp.dot(act_s[row0:row0 + rows, :], wd_s[...], preferred_element_type=F32) + bd_ref[0]
            for half in range(FF_T // (2 * LANES)):
                lo = acc[:, 2 * half * LANES:(2 * half + 1) * LANES]
                hi = acc[:, (2 * half + 1) * LANES:(2 * half + 2) * LANES]
                outw_s[(j - N1) * (FF_T // (2 * LANES)) + half, row0:row0 + rows, :] = (
                    pltpu.pack_elementwise([lo, hi], packed_dtype=BF16))

        _for_row_chunks(nblk, down)

    @pl.when((j == N1 + N2 - 1) & (nblk > 0))
    def _():
        for i in range(SB_SUBS):
            @pl.when(i < nblk)
            def _():
                if i >= 2:
                    _sub_out_copy(stage_out, out_hbm, sem_out, blk0 + i - 2, i % 2).wait()
                for w in range(PACK_W):
                    stage_out[i % 2, pl.ds(w, SUB, stride=PACK_W), :] = outw_s[w, i * SUB:(i + 1) * SUB, :]
                _sub_out_copy(stage_out, out_hbm, sem_out, blk0 + i, i % 2).start()
        for i in range(SB_SUBS):
            @pl.when((i < nblk) & (i + 2 >= nblk))
            def _():
                _sub_out_copy(stage_out, out_hbm, sem_out, blk0 + i, i % 2).wait()


def _expert_call(sb_e, sb_n, sb_b, xs, w_gu, b_gu, w_dn, b_dn):
    jg = lambda j: jnp.minimum(j, N1 - 1)
    jd = lambda j: jnp.maximum(j - N1, 0)
    gs = pltpu.PrefetchScalarGridSpec(
        num_scalar_prefetch=3, grid=(S_MAX, N1 + N2),
        in_specs=[pl.BlockSpec(memory_space=pl.ANY),
                  pl.BlockSpec((1, D, FF_T), lambda s, j, e, n, b: (e[s], 0, jg(j))),
                  pl.BlockSpec((1, D, FF_T), lambda s, j, e, n, b: (e[s], 0, N1 + jg(j))),
                  pl.BlockSpec((1, 1, FF_T), lambda s, j, e, n, b: (e[s], 0, jg(j))),
                  pl.BlockSpec((1, 1, FF_T), lambda s, j, e, n, b: (e[s], 0, N1 + jg(j))),
                  pl.BlockSpec((1, DFF, FF_T), lambda s, j, e, n, b: (e[s], 0, jd(j))),
                  pl.BlockSpec((1, 1, FF_T), lambda s, j, e, n, b: (e[s], 0, jd(j)))],
        out_specs=pl.BlockSpec(memory_space=pl.ANY),
        scratch_shapes=[pltpu.VMEM((SB_R, D), BF16),
                        pltpu.VMEM((SB_R, DFF), BF16),
                        pltpu.VMEM((PACK_W, SB_R, LANES), I32),
                        pltpu.VMEM((D, FF_T), BF16),
                        pltpu.VMEM((D, FF_T), BF16),
                        pltpu.VMEM((DFF, FF_T), BF16),
                        pltpu.VMEM((2, SUB * PACK_W, LANES), I32),
                        pltpu.VMEM((2, SUB * PACK_W, LANES), I32),
                        pltpu.SemaphoreType.DMA((2,)),
                        pltpu.SemaphoreType.DMA((2,))])
    return pl.pallas_call(
        _expert_kernel, grid_spec=gs,
        out_shape=jax.ShapeDtypeStruct(xs.shape, xs.dtype),
        input_output_aliases={3: 0},
        compiler_params=_cparams(("arbitrary", "arbitrary")),
        name="moe_experts",
    )(sb_e, sb_n, sb_b, xs, w_gu, w_gu, b_gu, b_gu, w_dn, b_dn)


def _combine_copy(src_hbm, buf, sem, src_row, dst_row, rows):
    return pltpu.make_async_copy(src_hbm.at[pl.ds(src_row * PACK_W, rows * PACK_W)],
                                 buf.at[pl.ds(dst_row * PACK_W, rows * PACK_W)], sem)


def _combine_kernel(dest_ref, ys_hbm, gates_ref, x1_ref, gt_ref, ln_ref, o_ref, bufs, y_s, sems):
    step = pl.program_id(0) * TILES_L + pl.program_id(1)

    def issue(tile, slot):
        def body(t, carry):
            for kk in range(TOPK):
                _combine_copy(ys_hbm, bufs.at[slot], sems.at[slot], dest_ref[(tile * ROW_T + t) * TOPK + kk],
                              kk * ROW_T + t, 1).start()
            return carry
        lax.fori_loop(0, ROW_T, body, 0)

    @pl.when(step == 0)
    def _():
        issue(0, 0)

    @pl.when(step + 1 < B * TILES_L)
    def _():
        issue(step + 1, (step + 1) % 2)

    slot = step % 2
    buf = bufs.at[slot]
    _combine_copy(ys_hbm, buf, sems.at[slot], 0, 0, ROW_T * TOPK).wait()

    gates = gates_ref[...]
    gk = [jnp.broadcast_to(gates[:, kk:kk + 1], (ROW_T, LANES)) for kk in range(TOPK)]
    for w in range(PACK_W):
        acc_lo = jnp.zeros((ROW_T, LANES), F32)
        acc_hi = jnp.zeros((ROW_T, LANES), F32)
        for kk in range(TOPK):
            lo, hi = _unpack_word(buf[pl.ds(kk * ROW_T * PACK_W + w, ROW_T, stride=PACK_W), :])
            acc_lo = acc_lo + gk[kk] * lo
            acc_hi = acc_hi + gk[kk] * hi
        y_s[:, 2 * w * LANES:(2 * w + 1) * LANES] = acc_lo
        y_s[:, (2 * w + 1) * LANES:(2 * w + 2) * LANES] = acc_hi
    o_ref[0] = x1_ref[0] + gt_ref[0] * _rms(y_s[...], ln_ref[...])


def _combine_call(dest, ys, gates, x1, mod3, ln_post):
    tok = lambda b, i: b * TILES_L + i
    gs = pltpu.PrefetchScalarGridSpec(
        num_scalar_prefetch=1, grid=(B, TILES_L),
        in_specs=[pl.BlockSpec(memory_space=pl.ANY),
                  pl.BlockSpec((ROW_T, LANES), lambda b, i, d: (tok(b, i), 0)),
                  pl.BlockSpec((1, ROW_T, D), lambda b, i, d: (b, i, 0)),
                  pl.BlockSpec((1, 1, D), lambda b, i, d: (b, 0, 5)),
                  pl.BlockSpec((1, D), lambda b, i, d: (0, 0))],
        out_specs=pl.BlockSpec((1, ROW_T, D), lambda b, i, d: (b, i, 0)),
        scratch_shapes=[pltpu.VMEM((2, TOPK * ROW_T * PACK_W, LANES), I32),
                        pltpu.VMEM((ROW_T, D), F32),
                        pltpu.SemaphoreType.DMA((2,))])
    return pl.pallas_call(
        _combine_kernel, grid_spec=gs,
        out_shape=jax.ShapeDtypeStruct((B, N, D), F32),
        compiler_params=_cparams(("arbitrary", "arbitrary")),
        name="moe_combine",
    )(dest, ys, gates, x1, mod3, ln_post)


def _rot_cols(w):
    half = ROPE // 2
    return jnp.concatenate([-w[..., half:], w[..., :half]], axis=-1)


def _rope_tables():
    t = np.arange(N)
    r = (t // GRID_W).astype(np.float64)
    cl = (t % GRID_W).astype(np.float64)
    n_freq = ROPE // 4
    inv = ROPE_THETA ** (-np.arange(n_freq, dtype=np.float64) / n_freq)
    ang = np.concatenate([r[:, None] * inv, cl[:, None] * inv], axis=-1)
    cos, sin = np.cos(ang), np.sin(ang)
    cs = np.concatenate([cos, cos, sin, sin], axis=-1).astype(np.float32)
    ctx = np.concatenate([np.ones((M, ROPE), np.float32), np.zeros((M, ROPE), np.float32)], axis=-1)
    return jnp.asarray(np.concatenate([ctx, cs], axis=0)), jnp.asarray(cs * QK_SCALE)


def _routing_tables(ids, counts_f):
    e = ids[:, :TOPK]
    rank = ids[:, TOPK:2 * TOPK]
    counts = counts_f[0, :NE].astype(I32)
    nsub = (counts + SUB - 1) // SUB
    sub_start = jnp.cumsum(nsub) - nsub
    onehot = e[..., None] == jnp.arange(NE, dtype=I32)
    dest = jnp.sum(jnp.where(onehot, sub_start * SUB, 0), axis=-1) + rank
    nsb = (nsub + SB_SUBS - 1) // SB_SUBS
    sb_end = jnp.cumsum(nsb)
    sb_start = sb_end - nsb
    n_active = sb_end[-1]
    s_idx = jnp.arange(S_MAX, dtype=I32)
    s_eff = jnp.minimum(s_idx, n_active - 1)
    sb_e = jnp.minimum(jnp.searchsorted(sb_end, s_eff, side="right"), NE - 1).astype(I32)
    local = s_eff - sb_start[sb_e]
    sb_b = (sub_start[sb_e] + local * SB_SUBS).astype(I32)
    sb_n = jnp.where(s_idx < n_active, jnp.clip(nsub[sb_e] - local * SB_SUBS, 0, SB_SUBS), 0).astype(I32)
    return dest.reshape(-1).astype(I32), sb_e, sb_n, sb_b


def kernel(x, c, ctx, c_ctx, w_mod, b_mod, ln_mix_pre, ln_mix_post, ln_ffn_pre, ln_ffn_post, w_in, q_norm, kv_norm,
           w_uq, w_ukv, hgrn_lb, g_norm, w_out, router_w, router_b, w_gate_up, b_gate_up, w_down, b_down):
    c_all = jnp.concatenate([c, c_ctx[None, :], jnp.zeros((8 - B - 1, D), F32)], axis=0)
    mod = _mod_call(c_all, w_mod[0], b_mod)
    mod3 = mod.reshape(8, 1, 6 * D)

    w = w_in[0]
    o_q, o_kv, o_kr = 0, Q_RANK, Q_RANK + KV_RANK
    o_hq = o_kr + ROPE
    wide = H * HW
    o_ff, o_fb, o_hi, o_hg = o_hq + wide, o_hq + 2 * wide, o_hq + 3 * wide, o_hq + 4 * wide
    w_kr = w[:, o_kr:o_kr + ROPE]
    w_p = jnp.concatenate([w[:, o_q:o_kr], w_kr, _rot_cols(w_kr), jnp.zeros((D, LANES), F32),
                           w[:, o_hq:o_hq + wide], w[:, o_hi:o_hi + wide], w[:, o_hg:o_hg + wide]],
                          axis=1).astype(BF16)
    w_f = w[:, o_ff:o_ff + 2 * wide].astype(BF16)
    wq = w_uq[0]
    wq_r = wq[..., HW:]
    w_q = jnp.concatenate([wq[..., :HW], wq_r, _rot_cols(wq_r)], axis=-1).reshape(Q_RANK, 2 * wide).astype(BF16)
    wkv = w_ukv[0]
    w_kv = jnp.concatenate([wkv[..., :HW].reshape(KV_RANK, wide), wkv[..., HW:].reshape(KV_RANK, wide)],
                           axis=1).astype(BF16)
    w_o = w_out[0].astype(BF16)
    cs_all, cs_q = _rope_tables()
    lb = jnp.cumsum(jax.nn.softmax(hgrn_lb.astype(F32), axis=1), axis=1)
    r_w = jnp.concatenate([router_w[0], jnp.zeros((D, LANES - NE), F32)], axis=1)
    r_w_hi = r_w.astype(BF16)
    r_w = jnp.concatenate([r_w_hi, (r_w - r_w_hi.astype(F32)).astype(BF16)], axis=1)
    r_b = jnp.concatenate([router_b[0], jnp.zeros((LANES - NE,), F32)])[None, :]

    h_all = _normmod_call(x, ctx, mod3, ln_mix_pre)
    h_flat = h_all.reshape(TA, D)
    p = _matmul_call(h_flat, w_p, BF16, "in_proj")
    fraw = _matmul_call(h_flat, w_f, F32, "in_proj_forget")
    k_all, v_all = _kvprep_call(p, kv_norm, w_kv, cs_all)
    q_all = _qprep_call(p, q_norm, w_q, cs_q)
    att = _attn_call(q_all, k_all, v_all)
    o_f, o_b = _hgrn_call(p, fraw, lb[0, 0][None, :], lb[1, 0][None, :])

    g_norm_t = g_norm
    x1, h2w, ids, gates, counts = _mixout_call(att, o_f, o_b, p, g_norm_t, w_o[:wide], w_o[wide:], x, mod3,
                                               ln_mix_post, ln_ffn_pre, r_w, r_b)

    dest, sb_e, sb_n, sb_b = _routing_tables(ids, counts)
    xs = _dispatch_call(dest, h2w, jnp.zeros((N_SUBS * SUB * PACK_W, LANES), I32))
    ys = _expert_call(sb_e, sb_n, sb_b, xs, w_gate_up[0], b_gate_up[0][:, None, :], w_down[0], b_down[0][:, None, :])
    return _combine_call(dest, ys, gates, x1, mod3, ln_ffn_post)
```

```python
import functools

import numpy as np
import jax
import jax.numpy as jnp
from jax import lax
from jax.experimental import pallas as pl
from jax.experimental.pallas import tpu as pltpu

F32 = jnp.float32
BF16 = jnp.bfloat16
I32 = jnp.int32

D = 2048
B = 4
N = 4096
M = 256
NA = N + M
T = B * N
TA = B * NA
EPS = 1e-6
H = 8
HW = 128
Q_RANK = 512
KV_RANK = 256
ROPE = 64
QK_SCALE = float((128 + ROPE) ** -0.5)
GRID_W = 64
ROPE_THETA = 10000.0
NE = 32
TOPK = 4
DFF = 2048
LIMIT = 7.0
ALPHA = 1.702
LOG2E = float(np.log2(np.e))

LANES = 128
SUBLANES = 8
VMEM_LIMIT = 56 * 1024 * 1024

ROW_T = 256
TILES_A = NA // ROW_T
TILES_L = N // ROW_T
MM_TM = 1024
MM_TN = 1024
HG_C = 128
HG_STEPS = NA // HG_C
HG_CTX = M // HG_C
HG_LEVELS = (64, 32, 16, 8, 4, 2, 1)
SUB = 256
N_SUBS = T * TOPK // SUB + NE
SB_SUBS = 10
SB_R = SB_SUBS * SUB
CHUNK_SUBS = 4
S_MAX = -(-N_SUBS // SB_SUBS) + NE
FF_T = 256
N1 = DFF // FF_T
N2 = D // FF_T
PACK_W = D // 2 // LANES
DISP_T = 512
COMBINE_UNROLL = 4


def _sigmoid(x):
    return 1.0 / (1.0 + jnp.exp(-x))


def _rms(x, w):
    ms = jnp.mean(x * x, axis=-1, keepdims=True)
    return x * lax.rsqrt(ms + EPS) * w


def _cparams(sem, vmem=VMEM_LIMIT):
    return pltpu.CompilerParams(dimension_semantics=sem, vmem_limit_bytes=vmem)


def _mod_kernel(c_ref, w_ref, b_ref, o_ref):
    c = c_ref[...]
    a = (c * _sigmoid(c)).astype(BF16)
    o_ref[...] = jnp.dot(a, w_ref[...].astype(BF16), preferred_element_type=F32) + b_ref[...]


def _mod_call(c_all, w_mod, b_mod):
    tn = 1024
    return pl.pallas_call(
        _mod_kernel,
        grid=(6 * D // tn,),
        in_specs=[pl.BlockSpec((8, D), lambda j: (0, 0)),
                  pl.BlockSpec((D, tn), lambda j: (0, j)),
                  pl.BlockSpec((1, tn), lambda j: (0, j))],
        out_specs=pl.BlockSpec((8, tn), lambda j: (0, j)),
        out_shape=jax.ShapeDtypeStruct((8, 6 * D), F32),
        compiler_params=_cparams(("arbitrary",)),
        name="mod",
    )(c_all, w_mod, b_mod)


def _normmod_kernel(x_ref, ctx_ref, sh_ref, sc_ref, w_ref, o_ref):
    def go(src):
        y = _rms(src, w_ref[...])
        o_ref[0] = (y * (1.0 + sc_ref[0]) + sh_ref[0]).astype(BF16)

    i = pl.program_id(1)

    @pl.when(i == 0)
    def _():
        go(ctx_ref[0])

    @pl.when(i > 0)
    def _():
        go(x_ref[0])


def _normmod_call(x, ctx, mod3, ln_w):
    row = lambda b, i: jnp.where(i == 0, B, b)
    return pl.pallas_call(
        _normmod_kernel,
        grid=(B, TILES_A),
        in_specs=[pl.BlockSpec((1, ROW_T, D), lambda b, i: (b, jnp.maximum(i - 1, 0), 0)),
                  pl.BlockSpec((1, ROW_T, D), lambda b, i: (b, 0, 0)),
                  pl.BlockSpec((1, 1, D), lambda b, i: (row(b, i), 0, 0)),
                  pl.BlockSpec((1, 1, D), lambda b, i: (row(b, i), 0, 1)),
                  pl.BlockSpec((1, D), lambda b, i: (0, 0))],
        out_specs=pl.BlockSpec((1, ROW_T, D), lambda b, i: (b, i, 0)),
        out_shape=jax.ShapeDtypeStruct((B, NA, D), BF16),
        compiler_params=_cparams(("arbitrary", "arbitrary")),
        name="normmod",
    )(x, ctx, mod3, mod3, ln_w)


def _matmul_kernel(a_ref, w_ref, o_ref):
    o_ref[...] = jnp.dot(a_ref[...], w_ref[...], preferred_element_type=F32).astype(o_ref.dtype)


def _matmul_call(a, w, out_dtype, name):
    rows, k = a.shape
    cols = w.shape[1]
    return pl.pallas_call(
        _matmul_kernel,
        grid=(rows // MM_TM, cols // MM_TN),
        in_specs=[pl.BlockSpec((MM_TM, k), lambda i, j: (i, 0)),
                  pl.BlockSpec((k, MM_TN), lambda i, j: (0, j))],
        out_specs=pl.BlockSpec((MM_TM, MM_TN), lambda i, j: (i, j)),
        out_shape=jax.ShapeDtypeStruct((rows, cols), out_dtype),
        compiler_params=_cparams(("arbitrary", "arbitrary")),
        name=name,
    )(a, w)


def _kvprep_kernel(p_ref, kvn_ref, w_ref, cs_ref, k_ref, v_ref):
    ckv = p_ref[:, Q_RANK:Q_RANK + KV_RANK].astype(F32)
    ckvn = _rms(ckv, kvn_ref[...]).astype(BF16)
    kv = jnp.dot(ckvn, w_ref[...], preferred_element_type=F32)
    a = p_ref[:, Q_RANK + KV_RANK:Q_RANK + KV_RANK + LANES].astype(F32) * cs_ref[...]
    r = a + pltpu.roll(a, ROPE, axis=1)
    lane = lax.broadcasted_iota(I32, r.shape, 1)
    r = jnp.where(lane < ROPE, r, 0.0).astype(BF16)
    for h in range(H):
        k_ref[0, h, :, 0:HW] = kv[:, h * HW:(h + 1) * HW].astype(BF16)
        k_ref[0, h, :, HW:2 * HW] = r
        v_ref[0, h] = kv[:, (H + h) * HW:(H + h + 1) * HW].astype(BF16)


def _kvprep_call(p, kv_norm, w_kv, cs_all):
    return pl.pallas_call(
        _kvprep_kernel,
        grid=(B, TILES_A),
        in_specs=[pl.BlockSpec((ROW_T, 1024), lambda b, i: (b * TILES_A + i, 0)),
                  pl.BlockSpec((1, KV_RANK), lambda b, i: (0, 0)),
                  pl.BlockSpec((KV_RANK, 2 * H * HW), lambda b, i: (0, 0)),
                  pl.BlockSpec((ROW_T, LANES), lambda b, i: (i, 0))],
        out_specs=[pl.BlockSpec((1, H, ROW_T, 2 * HW), lambda b, i: (b, 0, i, 0)),
                   pl.BlockSpec((1, H, ROW_T, HW), lambda b, i: (b, 0, i, 0))],
        out_shape=[jax.ShapeDtypeStruct((B, H, NA, 2 * HW), BF16),
                   jax.ShapeDtypeStruct((B, H, NA, HW), BF16)],
        compiler_params=_cparams(("arbitrary", "arbitrary")),
        name="kvprep",
    )(p, kv_norm, w_kv, cs_all)


def _qprep_kernel(p_ref, qn_ref, w_ref, cs_ref, q_ref):
    cq = p_ref[:, 0:Q_RANK].astype(F32)
    cqn = _rms(cq, qn_ref[...]).astype(BF16)
    q = jnp.dot(cqn, w_ref[...], preferred_element_type=F32)
    cs = cs_ref[...]
    for h in range(H):
        q_ref[0, h, :, 0:HW] = (q[:, 2 * h * HW:(2 * h + 1) * HW] * QK_SCALE).astype(BF16)
        a = q[:, (2 * h + 1) * HW:(2 * h + 2) * HW] * cs
        q_ref[0, h, :, HW:2 * HW] = (a + pltpu.roll(a, ROPE, axis=1)).astype(BF16)


def _qprep_call(p, q_norm, w_q, cs_q):
    return pl.pallas_call(
        _qprep_kernel,
        grid=(B, TILES_L),
        in_specs=[pl.BlockSpec((ROW_T, 1024), lambda b, i: (b * TILES_A + 1 + i, 0)),
                  pl.BlockSpec((1, Q_RANK), lambda b, i: (0, 0)),
                  pl.BlockSpec((Q_RANK, 2 * H * HW), lambda b, i: (0, 0)),
                  pl.BlockSpec((ROW_T, LANES), lambda b, i: (i, 0))],
        out_specs=pl.BlockSpec((1, H, ROW_T, 2 * HW), lambda b, i: (b, 0, i, 0)),
        out_shape=jax.ShapeDtypeStruct((B, H, N, 2 * HW), BF16),
        compiler_params=_cparams(("arbitrary", "arbitrary")),
        name="qprep",
    )(p, q_norm, w_q, cs_q)


ATT_TQ = 256


def _attn_kernel(q_ref, k_ref, v_ref, o_ref):
    s = lax.dot_general(q_ref[0, 0], k_ref[0, 0], (((1,), (1,)), ((), ())),
                        preferred_element_type=F32)
    m = jnp.max(s, axis=-1, keepdims=True)
    p = jnp.exp(s - m)
    l = jnp.sum(p, axis=-1, keepdims=True)
    o = jnp.dot(p.astype(BF16), v_ref[0, 0], preferred_element_type=F32)
    o_ref[...] = (o * (1.0 / l)).astype(BF16)


def _attn_call(q, k, v):
    return pl.pallas_call(
        _attn_kernel,
        grid=(B, H, N // ATT_TQ),
        in_specs=[pl.BlockSpec((1, 1, ATT_TQ, 2 * HW), lambda b, h, i: (b, h, i, 0)),
                  pl.BlockSpec((1, 1, NA, 2 * HW), lambda b, h, i: (b, h, 0, 0)),
                  pl.BlockSpec((1, 1, NA, HW), lambda b, h, i: (b, h, 0, 0))],
        out_specs=pl.BlockSpec((ATT_TQ, HW), lambda b, h, i: (b * (N // ATT_TQ) + i, h)),
        out_shape=jax.ShapeDtypeStruct((T, H * HW), BF16),
        compiler_params=_cparams(("arbitrary", "arbitrary", "arbitrary")),
        name="attention",
    )(q, k, v)


def _hgrn_tables(reverse):
    c = HG_C
    t = np.arange(c)[:, None]
    u = np.arange(c)[None, :]
    cum = (u <= t).astype(np.float32)
    masks = [(u == t)]
    for m in HG_LEVELS:
        blk_t = t // m
        masks.append(((blk_t % 2) == 1) & ((u // m) == blk_t - 1))
    masks = np.stack(masks).astype(np.float32)
    if reverse:
        cum = cum[::-1, ::-1]
        masks = masks[:, ::-1, ::-1]
    return jnp.asarray(cum, dtype=BF16), jnp.asarray(masks.reshape(-1, c), dtype=F32)


def _pair_boundary(c_ref, cval, m, reverse):
    def rows_of(r_of_group):
        return jnp.concatenate([jnp.broadcast_to(c_ref[r_of_group(gi):r_of_group(gi) + 1, :], (SUBLANES, HW))
                                for gi in range(HG_C // SUBLANES)], axis=0)

    if m >= 4:
        def boundary(gi):
            pair = (SUBLANES * gi // (2 * m)) * 2 * m
            return pair + m if reverse else pair + m - 1
        return rows_of(boundary)
    row = lax.broadcasted_iota(I32, (HG_C, HW), 0)
    if m == 2:
        first = rows_of(lambda gi: SUBLANES * gi + (2 if reverse else 1))
        second = rows_of(lambda gi: SUBLANES * gi + (6 if reverse else 5))
        return jnp.where((row & 7) < 4, first, second)
    if reverse:
        return jnp.where((row & 1) == 0, pltpu.roll(cval, HG_C - 1, 0), cval)
    return jnp.where((row & 1) == 1, pltpu.roll(cval, 1, 0), cval)


def _hgrn_head(h, hq_ref, hi_ref, fr_ref, lb_ref, cum_ref, mk_ref, s_ref, o_ref, c_ref, reverse):
    c = HG_C
    col = pl.ds(pl.multiple_of(h * HW, HW), HW)
    xq = hq_ref[:, col].astype(F32)
    q = xq * _sigmoid(xq)
    lb = lb_ref[:, col]
    f = lb + (1.0 - lb) * _sigmoid(fr_ref[:, col])
    g = jnp.log(f)
    k = 1.0 - f
    v = hi_ref[:, col]
    g_hi = g.astype(BF16)
    g_lo = (g - g_hi.astype(F32)).astype(BF16)
    cum = cum_ref[...]
    cs = jnp.dot(jnp.concatenate([cum, cum], axis=1), jnp.concatenate([g_hi, g_lo], axis=0),
                 preferred_element_type=F32) * LOG2E
    c_ref[...] = cs
    total_row = 0 if reverse else c - 1
    c_tot = c_ref[total_row:total_row + 1, :]

    nt = (((1,), (1,)), ((), ()))
    st = s_ref[h]
    o = lax.dot_general((q * jnp.exp2(cs)).astype(BF16), st.astype(BF16), nt, preferred_element_type=F32)
    qb = q.astype(BF16)
    kb = k.astype(BF16)
    sc = lax.dot_general(qb, kb, nt, preferred_element_type=F32) * mk_ref[0:c, :]
    sign = jnp.uint32(0x80000000)
    for li, m in enumerate(HG_LEVELS):
        z = cs - _pair_boundary(c_ref, cs, m, reverse)
        w = jnp.exp2(pltpu.bitcast(pltpu.bitcast(z, jnp.uint32) | sign, F32)).astype(BF16)
        sc = sc + (lax.dot_general(qb * w, kb * w, nt, preferred_element_type=F32)
                   * mk_ref[(li + 1) * c:(li + 2) * c, :])
    o = o + jnp.dot(sc.astype(BF16), v, preferred_element_type=F32)
    o_ref[:, col] = o.astype(BF16)
    kend = (k * jnp.exp2(c_tot - cs)).astype(BF16)
    s_ref[h] = st * jnp.exp2(c_tot) + lax.dot_general(v, kend, (((0,), (0,)), ((), ())), preferred_element_type=F32)


HG_UNROLL = 8


def _hgrn_kernel(hq_f, hi_f, fr_f, hq_b, hi_b, fr_b, lbf_ref, lbb_ref, cumf_ref, mkf_ref, cumb_ref, mkb_ref,
                 of_ref, ob_ref, sf_ref, sb_ref, c_s):
    @pl.when(pl.program_id(1) == 0)
    def _():
        sf_ref[...] = jnp.zeros_like(sf_ref)
        sb_ref[...] = jnp.zeros_like(sb_ref)

    def body(i, carry):
        for hh in range(HG_UNROLL):
            h = i * HG_UNROLL + hh
            _hgrn_head(h, hq_f, hi_f, fr_f, lbf_ref, cumf_ref, mkf_ref, sf_ref, of_ref, c_s.at[2 * hh], False)
            _hgrn_head(h, hq_b, hi_b, fr_b, lbb_ref, cumb_ref, mkb_ref, sb_ref, ob_ref, c_s.at[2 * hh + 1], True)
        return carry

    lax.fori_loop(0, H // HG_UNROLL, body, 0)


def _hgrn_call(p, fraw, lb_f, lb_b):
    cum_f, mk_f = _hgrn_tables(False)
    cum_b, mk_b = _hgrn_tables(True)
    wide = H * HW

    def cf(b, s):
        return b * HG_STEPS + s

    def cb(b, s):
        return b * HG_STEPS + jnp.where(s < HG_CTX, HG_CTX - 1 - s, HG_STEPS - 1 + HG_CTX - s)

    const = lambda shape: pl.BlockSpec(shape, lambda b, s: (0, 0))
    return pl.pallas_call(
        _hgrn_kernel,
        grid=(B, HG_STEPS),
        in_specs=[pl.BlockSpec((HG_C, wide), lambda b, s: (cf(b, s), 1)),
                  pl.BlockSpec((HG_C, wide), lambda b, s: (cf(b, s), 2)),
                  pl.BlockSpec((HG_C, wide), lambda b, s: (cf(b, s), 0)),
                  pl.BlockSpec((HG_C, wide), lambda b, s: (cb(b, s), 1)),
                  pl.BlockSpec((HG_C, wide), lambda b, s: (cb(b, s), 2)),
                  pl.BlockSpec((HG_C, wide), lambda b, s: (cb(b, s), 1)),
                  const((1, wide)), const((1, wide)),
                  const(cum_f.shape), const(mk_f.shape), const(cum_b.shape), const(mk_b.shape)],
        out_specs=[pl.BlockSpec((HG_C, wide), lambda b, s: (cf(b, s), 0)),
                   pl.BlockSpec((HG_C, wide), lambda b, s: (cb(b, s), 0))],
        out_shape=[jax.ShapeDtypeStruct((TA, wide), BF16), jax.ShapeDtypeStruct((TA, wide), BF16)],
        scratch_shapes=[pltpu.VMEM((H, HW, HW), F32), pltpu.VMEM((H, HW, HW), F32),
                        pltpu.VMEM((2 * HG_UNROLL, HG_C, HW), F32)],
        compiler_params=_cparams(("arbitrary", "arbitrary")),
        name="hgrn",
    )(p, p, fraw, p, p, fraw, lb_f, lb_b, cum_f, mk_f, cum_b, mk_b)


def _pack_rows(val, dst_ref, row0, rows):
    for s in range(PACK_W):
        lo = val[:, 2 * s * LANES:(2 * s + 1) * LANES]
        hi = val[:, (2 * s + 1) * LANES:(2 * s + 2) * LANES]
        w = pltpu.pack_elementwise([lo, hi], packed_dtype=BF16)
        dst_ref[pl.ds(row0 * PACK_W + s, rows, stride=PACK_W), :] = w


def _unpack_word(w):
    lo = pltpu.unpack_elementwise(w, index=0, packed_dtype=BF16, unpacked_dtype=F32)
    hi = pltpu.unpack_elementwise(w, index=1, packed_dtype=BF16, unpacked_dtype=F32)
    return lo, hi


def _mixout_kernel(att_ref, of_ref, ob_ref, hg_ref, gn_ref, wa_ref, wr_ref, x_ref, gt_ref, shf_ref, scf_ref,
                   lnp_ref, lnf_ref, rw_ref, rb_ref,
                   x1_ref, h2w_ref, ids_ref, gates_ref, cnt_ref, carry_ref):
    first = (pl.program_id(0) == 0) & (pl.program_id(1) == 0)

    @pl.when(first)
    def _():
        carry_ref[...] = jnp.zeros_like(carry_ref)

    o = of_ref[...].astype(F32) + ob_ref[...].astype(F32)
    gn = gn_ref[...]
    recs = []
    for h in range(H):
        oh = o[:, h * HW:(h + 1) * HW]
        gate = hg_ref[:, h * HW:(h + 1) * HW].astype(F32)
        recs.append((_rms(oh, gn) * (gate * _sigmoid(gate))).astype(BF16))
    rec = jnp.concatenate(recs, axis=-1)
    mix = (jnp.dot(att_ref[...], wa_ref[...], preferred_element_type=F32)
           + jnp.dot(rec, wr_ref[...], preferred_element_type=F32))
    x1 = x_ref[0] + gt_ref[0] * _rms(mix, lnp_ref[...])
    x1_ref[0] = x1
    h2 = _rms(x1, lnf_ref[...]) * (1.0 + scf_ref[0]) + shf_ref[0]
    _pack_rows(h2, h2w_ref, 0, ROW_T)

    h_hi = h2.astype(BF16)
    h_lo = (h2 - h_hi.astype(F32)).astype(BF16)
    hh = jnp.dot(h_hi, rw_ref[...], preferred_element_type=F32)
    logits = (hh[:, :LANES] + hh[:, LANES:] + jnp.dot(h_lo, rw_ref[:, :LANES], preferred_element_type=F32)
              + rb_ref[...])
    lane = lax.broadcasted_iota(I32, logits.shape, 1)
    neg = jnp.float32(-jnp.inf)
    lg = jnp.where(lane < NE, logits, neg)
    tops, idxs = [], []
    for _ in range(TOPK):
        m = jnp.max(lg, axis=-1, keepdims=True)
        idx = jnp.min(jnp.where(lg == m, lane, LANES), axis=-1, keepdims=True)
        tops.append(m)
        idxs.append(idx)
        lg = jnp.where(lane == idx, neg, lg)
    exps = [jnp.exp(tk - tops[0]) for tk in tops]
    inv = 1.0 / (exps[0] + exps[1] + exps[2] + exps[3])
    onehot = jnp.zeros(logits.shape, F32)
    for idx in idxs:
        onehot = onehot + jnp.where(lane == idx, 1.0, 0.0)
    r = lax.broadcasted_iota(I32, (ROW_T, ROW_T), 0)
    cidx = lax.broadcasted_iota(I32, (ROW_T, ROW_T), 1)
    lower = jnp.where(cidx < r, 1.0, 0.0).astype(BF16)
    prefix = jnp.dot(lower, onehot.astype(BF16), preferred_element_type=F32) + carry_ref[...]
    ids = jnp.zeros(logits.shape, I32)
    gates = jnp.zeros(logits.shape, F32)
    for kk in range(TOPK):
        rank = jnp.sum(jnp.where(lane == idxs[kk], prefix, 0.0), axis=-1, keepdims=True).astype(I32)
        ids = jnp.where(lane == kk, idxs[kk], ids)
        ids = jnp.where(lane == TOPK + kk, rank, ids)
        gates = jnp.where(lane == kk, exps[kk] * inv, gates)
    ids_ref[...] = ids
    gates_ref[...] = gates
    carry = carry_ref[...] + jnp.sum(onehot, axis=0, keepdims=True)
    carry_ref[...] = carry
    cnt_ref[...] = carry


def _mixout_call(att, o_f, o_b, p, g_norm_t, w_a, w_r, x, mod3, ln_post, ln_ffn, r_w, r_b):
    wide = H * HW
    lat = lambda b, i: b * TILES_A + 1 + i
    tok = lambda b, i: b * TILES_L + i
    const = lambda shape: pl.BlockSpec(shape, lambda b, i: tuple(0 for _ in shape))
    return pl.pallas_call(
        _mixout_kernel,
        grid=(B, TILES_L),
        in_specs=[pl.BlockSpec((ROW_T, wide), lambda b, i: (tok(b, i), 0)),
                  pl.BlockSpec((ROW_T, wide), lambda b, i: (lat(b, i), 0)),
                  pl.BlockSpec((ROW_T, wide), lambda b, i: (lat(b, i), 0)),
                  pl.BlockSpec((ROW_T, wide), lambda b, i: (lat(b, i), 3)),
                  const((1, HW)), const((wide, D)), const((wide, D)),
                  pl.BlockSpec((1, ROW_T, D), lambda b, i: (b, i, 0)),
                  pl.BlockSpec((1, 1, D), lambda b, i: (b, 0, 2)),
                  pl.BlockSpec((1, 1, D), lambda b, i: (b, 0, 3)),
                  pl.BlockSpec((1, 1, D), lambda b, i: (b, 0, 4)),
                  const((1, D)), const((1, D)), const((D, 2 * LANES)), const((1, LANES))],
        out_specs=[pl.BlockSpec((1, ROW_T, D), lambda b, i: (b, i, 0)),
                   pl.BlockSpec((ROW_T * PACK_W, LANES), lambda b, i: (tok(b, i), 0)),
                   pl.BlockSpec((ROW_T, LANES), lambda b, i: (tok(b, i), 0)),
                   pl.BlockSpec((ROW_T, LANES), lambda b, i: (tok(b, i), 0)),
                   const((1, LANES))],
        out_shape=[jax.ShapeDtypeStruct((B, N, D), F32),
                   jax.ShapeDtypeStruct((T * PACK_W, LANES), I32),
                   jax.ShapeDtypeStruct((T, LANES), I32),
                   jax.ShapeDtypeStruct((T, LANES), F32),
                   jax.ShapeDtypeStruct((1, LANES), F32)],
        scratch_shapes=[pltpu.VMEM((1, LANES), F32)],
        compiler_params=_cparams(("arbitrary", "arbitrary")),
        name="mixout_router",
    )(att, o_f, o_b, p, g_norm_t, w_a, w_r, x, mod3, mod3, mod3, ln_post, ln_ffn, r_w, r_b)


def _dispatch_copy(src_vmem, dst_hbm, sem, src_row, dst_row, rows):
    return pltpu.make_async_copy(src_vmem.at[pl.ds(src_row * PACK_W, rows * PACK_W)],
                                 dst_hbm.at[pl.ds(dst_row * PACK_W, rows * PACK_W)], sem)


def _dispatch_kernel(dest_ref, h2w_ref, init_hbm, xs_hbm, sem):
    del init_hbm
    base = pl.program_id(0) * DISP_T

    def issue(t, carry):
        for kk in range(TOPK):
            _dispatch_copy(h2w_ref, xs_hbm, sem, t, dest_ref[(base + t) * TOPK + kk], 1).start()
        return carry

    lax.fori_loop(0, DISP_T, issue, 0)
    for _ in range(TOPK):
        _dispatch_copy(h2w_ref, xs_hbm, sem, 0, 0, DISP_T).wait()


def _dispatch_call(dest, h2w, xs_init):
    gs = pltpu.PrefetchScalarGridSpec(
        num_scalar_prefetch=1, grid=(T // DISP_T,),
        in_specs=[pl.BlockSpec((DISP_T * PACK_W, LANES), lambda i, d: (i, 0)),
                  pl.BlockSpec(memory_space=pl.ANY)],
        out_specs=pl.BlockSpec(memory_space=pl.ANY),
        scratch_shapes=[pltpu.SemaphoreType.DMA(())])
    return pl.pallas_call(
        _dispatch_kernel, grid_spec=gs,
        out_shape=jax.ShapeDtypeStruct(xs_init.shape, xs_init.dtype),
        input_output_aliases={2: 0},
        compiler_params=_cparams(("arbitrary",)),
        name="moe_dispatch",
    )(dest, h2w, xs_init)


def _sub_in_copy(xs_hbm, stage, sems, blk, slot):
    return pltpu.make_async_copy(xs_hbm.at[pl.ds(blk * (SUB * PACK_W), SUB * PACK_W)], stage.at[slot], sems.at[slot])


def _sub_out_copy(stage, out_hbm, sems, blk, slot):
    return pltpu.make_async_copy(stage.at[slot], out_hbm.at[pl.ds(blk * (SUB * PACK_W), SUB * PACK_W)], sems.at[slot])


def _for_row_chunks(nblk, fn):
    for c in range(-(-SB_SUBS // CHUNK_SUBS)):
        row0 = c * CHUNK_SUBS * SUB
        most = min(CHUNK_SUBS, SB_SUBS - c * CHUNK_SUBS)
        for r in range(1, most + 1):
            full = r == CHUNK_SUBS
            cond = (nblk >= c * CHUNK_SUBS + r) if full else (nblk == c * CHUNK_SUBS + r)
            pl.when(cond)(functools.partial(fn, row0, r * SUB))


def _wtile_copy(w_hbm, buf, sems, e, col_tile, slot, which):
    return pltpu.make_async_copy(w_hbm.at[e, :, pl.ds(pl.multiple_of(col_tile * FF_T, FF_T), FF_T)],
                                 buf.at[slot, which], sems.at[slot, which])


def _expert_kernel(sbe_ref, sbn_ref, sbb_ref, xs_hbm, wgu_hbm, bgu_ref, wdn_hbm, bdn_ref,
                   out_hbm, x_s, act_s, outw_s, wbuf, stage_in, stage_out,
                   sem_w, sem_in, sem_out):
    s = pl.program_id(0)
    nblk = sbn_ref[s]
    blk0 = sbb_ref[s]
    e = sbe_ref[s]

    def gate_tiles(j, slot):
        return (_wtile_copy(wgu_hbm, wbuf, sem_w, e, j, slot, 0),
                _wtile_copy(wgu_hbm, wbuf, sem_w, e, N1 + j, slot, 1))

    def down_tile(j, slot):
        return _wtile_copy(wdn_hbm, wbuf, sem_w, e, j, slot, 0)

    @pl.when(nblk > 0)
    def _():
        for cp in gate_tiles(0, 0):
            cp.start()

        _sub_in_copy(xs_hbm, stage_in, sem_in, blk0, 0).start()
        for i in range(SB_SUBS):
            @pl.when(i < nblk)
            def _():
                if i + 1 < SB_SUBS:
                    @pl.when(i + 1 < nblk)
                    def _():
                        _sub_in_copy(xs_hbm, stage_in, sem_in, blk0 + i + 1, (i + 1) % 2).start()
                _sub_in_copy(xs_hbm, stage_in, sem_in, blk0 + i, i % 2).wait()
                for w in range(PACK_W):
                    lo, hi = _unpack_word(stage_in[i % 2, pl.ds(w, SUB, stride=PACK_W), :])
                    x_s[i * SUB:(i + 1) * SUB, 2 * w * LANES:(2 * w + 1) * LANES] = lo.astype(BF16)
                    x_s[i * SUB:(i + 1) * SUB, (2 * w + 1) * LANES:(2 * w + 2) * LANES] = hi.astype(BF16)

        def hidden_step(j, carry):
            slot = j % 2
            for cp in gate_tiles(j, slot):
                cp.wait()

            @pl.when(j + 1 < N1)
            def _():
                for cp in gate_tiles(j + 1, 1 - slot):
                    cp.start()

            @pl.when(j + 1 == N1)
            def _():
                down_tile(0, 1 - slot).start()

            col = pl.ds(pl.multiple_of(j * FF_T, FF_T), FF_T)
            bg = bgu_ref[0, :, col]
            bl = bgu_ref[0, :, pl.ds(pl.multiple_of(DFF + j * FF_T, FF_T), FF_T)]

            def gate_up(row0, rows):
                x = x_s[row0:row0 + rows, :]
                gu = jnp.dot(x, wbuf[slot, 0].astype(BF16), preferred_element_type=F32) + bg
                li = jnp.dot(x, wbuf[slot, 1].astype(BF16), preferred_element_type=F32) + bl
                glu = jnp.minimum(gu, LIMIT)
                lin = jnp.clip(li, -LIMIT, LIMIT)
                act = glu * _sigmoid(ALPHA * glu) * (lin + 1.0)
                act_s[row0:row0 + rows, col] = act.astype(BF16)

            _for_row_chunks(nblk, gate_up)
            return carry

        lax.fori_loop(0, N1, hidden_step, 0)

        def out_step(j2, carry):
            slot = (N1 + j2) % 2
            down_tile(j2, slot).wait()

            @pl.when(j2 + 1 < N2)
            def _():
                down_tile(j2 + 1, 1 - slot).start()

            bd = bdn_ref[0, :, pl.ds(pl.multiple_of(j2 * FF_T, FF_T), FF_T)]

            def down(row0, rows):
                acc = jnp.dot(act_s[row0:row0 + rows, :], wbuf[slot, 0].astype(BF16), preferred_element_type=F32) + bd
                for half in range(FF_T // (2 * LANES)):
                    lo = acc[:, 2 * half * LANES:(2 * half + 1) * LANES]
                    hi = acc[:, (2 * half + 1) * LANES:(2 * half + 2) * LANES]
                    outw_s[j2 * (FF_T // (2 * LANES)) + half, row0:row0 + rows, :] = (
                        pltpu.pack_elementwise([lo, hi], packed_dtype=BF16))

            _for_row_chunks(nblk, down)
            return carry

        lax.fori_loop(0, N2, out_step, 0)

        for i in range(SB_SUBS):
            @pl.when(i < nblk)
            def _():
                if i >= 2:
                    _sub_out_copy(stage_out, out_hbm, sem_out, blk0 + i - 2, i % 2).wait()
                for w in range(PACK_W):
                    stage_out[i % 2, pl.ds(w, SUB, stride=PACK_W), :] = outw_s[w, i * SUB:(i + 1) * SUB, :]
                _sub_out_copy(stage_out, out_hbm, sem_out, blk0 + i, i % 2).start()
        for i in range(SB_SUBS):
            @pl.when((i < nblk) & (i + 2 >= nblk))
            def _():
                _sub_out_copy(stage_out, out_hbm, sem_out, blk0 + i, i % 2).wait()


def _expert_call(sb_e, sb_n, sb_b, xs, w_gu, b_gu, w_dn, b_dn):
    gs = pltpu.PrefetchScalarGridSpec(
        num_scalar_prefetch=3, grid=(S_MAX,),
        in_specs=[pl.BlockSpec(memory_space=pl.ANY),
                  pl.BlockSpec(memory_space=pl.ANY),
                  pl.BlockSpec((1, 1, 2 * DFF), lambda s, e, n, b: (e[s], 0, 0)),
                  pl.BlockSpec(memory_space=pl.ANY),
                  pl.BlockSpec((1, 1, D), lambda s, e, n, b: (e[s], 0, 0))],
        out_specs=pl.BlockSpec(memory_space=pl.ANY),
        scratch_shapes=[pltpu.VMEM((SB_R, D), BF16),
                        pltpu.VMEM((SB_R, DFF), BF16),
                        pltpu.VMEM((PACK_W, SB_R, LANES), I32),
                        pltpu.VMEM((2, 2, D, FF_T), F32),
                        pltpu.VMEM((2, SUB * PACK_W, LANES), I32),
                        pltpu.VMEM((2, SUB * PACK_W, LANES), I32),
                        pltpu.SemaphoreType.DMA((2, 2)),
                        pltpu.SemaphoreType.DMA((2,)),
                        pltpu.SemaphoreType.DMA((2,))])
    return pl.pallas_call(
        _expert_kernel, grid_spec=gs,
        out_shape=jax.ShapeDtypeStruct(xs.shape, xs.dtype),
        input_output_aliases={3: 0},
        compiler_params=_cparams(("arbitrary",)),
        name="moe_experts",
    )(sb_e, sb_n, sb_b, xs, w_gu, b_gu, w_dn, b_dn)


def _combine_copy(src_hbm, buf, sem, src_row, dst_row, rows):
    return pltpu.make_async_copy(src_hbm.at[pl.ds(src_row * PACK_W, rows * PACK_W)],
                                 buf.at[pl.ds(dst_row * PACK_W, rows * PACK_W)], sem)


def _combine_kernel(dest_ref, ys_hbm, gates_ref, x1_ref, gt_ref, ln_ref, o_ref, bufs, y_s, sems):
    step = pl.program_id(0) * TILES_L + pl.program_id(1)

    def issue(tile, slot):
        def body(tg, carry):
            for tt in range(COMBINE_UNROLL):
                t = tg * COMBINE_UNROLL + tt
                for kk in range(TOPK):
                    _combine_copy(ys_hbm, bufs.at[slot], sems.at[slot], dest_ref[(tile * ROW_T + t) * TOPK + kk],
                                  kk * ROW_T + t, 1).start()
            return carry
        lax.fori_loop(0, ROW_T // COMBINE_UNROLL, body, 0)

    @pl.when(step == 0)
    def _():
        issue(0, 0)

    @pl.when(step + 1 < B * TILES_L)
    def _():
        issue(step + 1, (step + 1) % 2)

    slot = step % 2
    buf = bufs.at[slot]
    _combine_copy(ys_hbm, buf, sems.at[slot], 0, 0, ROW_T * TOPK).wait()

    gates = gates_ref[...]
    gk = [jnp.broadcast_to(gates[:, kk:kk + 1], (ROW_T, LANES)) for kk in range(TOPK)]
    for w in range(PACK_W):
        acc_lo = jnp.zeros((ROW_T, LANES), F32)
        acc_hi = jnp.zeros((ROW_T, LANES), F32)
        for kk in range(TOPK):
            lo, hi = _unpack_word(buf[pl.ds(kk * ROW_T * PACK_W + w, ROW_T, stride=PACK_W), :])
            acc_lo = acc_lo + gk[kk] * lo
            acc_hi = acc_hi + gk[kk] * hi
        y_s[:, 2 * w * LANES:(2 * w + 1) * LANES] = acc_lo
        y_s[:, (2 * w + 1) * LANES:(2 * w + 2) * LANES] = acc_hi
    o_ref[0] = x1_ref[0] + gt_ref[0] * _rms(y_s[...], ln_ref[...])


def _combine_call(dest, ys, gates, x1, mod3, ln_post):
    tok = lambda b, i: b * TILES_L + i
    gs = pltpu.PrefetchScalarGridSpec(
        num_scalar_prefetch=1, grid=(B, TILES_L),
        in_specs=[pl.BlockSpec(memory_space=pl.ANY),
                  pl.BlockSpec((ROW_T, LANES), lambda b, i, d: (tok(b, i), 0)),
                  pl.BlockSpec((1, ROW_T, D), lambda b, i, d: (b, i, 0)),
                  pl.BlockSpec((1, 1, D), lambda b, i, d: (b, 0, 5)),
                  pl.BlockSpec((1, D), lambda b, i, d: (0, 0))],
        out_specs=pl.BlockSpec((1, ROW_T, D), lambda b, i, d: (b, i, 0)),
        scratch_shapes=[pltpu.VMEM((2, TOPK * ROW_T * PACK_W, LANES), I32),
                        pltpu.VMEM((ROW_T, D), F32),
                        pltpu.SemaphoreType.DMA((2,))])
    return pl.pallas_call(
        _combine_kernel, grid_spec=gs,
        out_shape=jax.ShapeDtypeStruct((B, N, D), F32),
        compiler_params=_cparams(("arbitrary", "arbitrary")),
        name="moe_combine",
    )(dest, ys, gates, x1, mod3, ln_post)


def _rot_cols(w):
    half = ROPE // 2
    return jnp.concatenate([-w[..., half:], w[..., :half]], axis=-1)


def _rope_tables():
    t = np.arange(N)
    r = (t // GRID_W).astype(np.float64)
    cl = (t % GRID_W).astype(np.float64)
    n_freq = ROPE // 4
    inv = ROPE_THETA ** (-np.arange(n_freq, dtype=np.float64) / n_freq)
    ang = np.concatenate([r[:, None] * inv, cl[:, None] * inv], axis=-1)
    cos, sin = np.cos(ang), np.sin(ang)
    cs = np.concatenate([cos, cos, sin, sin], axis=-1).astype(np.float32)
    ctx = np.concatenate([np.ones((M, ROPE), np.float32), np.zeros((M, ROPE), np.float32)], axis=-1)
    return jnp.asarray(np.concatenate([ctx, cs], axis=0)), jnp.asarray(cs * QK_SCALE)


def _routing_tables(ids, counts_f):
    e = ids[:, :TOPK]
    rank = ids[:, TOPK:2 * TOPK]
    counts = counts_f[0, :NE].astype(I32)
    nsub = (counts + SUB - 1) // SUB
    sub_start = jnp.cumsum(nsub) - nsub
    onehot = e[..., None] == jnp.arange(NE, dtype=I32)
    dest = jnp.sum(jnp.where(onehot, sub_start * SUB, 0), axis=-1) + rank
    nsb = (nsub + SB_SUBS - 1) // SB_SUBS
    sb_end = jnp.cumsum(nsb)
    sb_start = sb_end - nsb
    n_active = sb_end[-1]
    s_idx = jnp.arange(S_MAX, dtype=I32)
    s_eff = jnp.minimum(s_idx, n_active - 1)
    sb_e = jnp.minimum(jnp.searchsorted(sb_end, s_eff, side="right"), NE - 1).astype(I32)
    local = s_eff - sb_start[sb_e]
    sb_b = (sub_start[sb_e] + local * SB_SUBS).astype(I32)
    sb_n = jnp.where(s_idx < n_active, jnp.clip(nsub[sb_e] - local * SB_SUBS, 0, SB_SUBS), 0).astype(I32)
    return dest.reshape(-1).astype(I32), sb_e, sb_n, sb_b


def kernel(x, c, ctx, c_ctx, w_mod, b_mod, ln_mix_pre, ln_mix_post, ln_ffn_pre, ln_ffn_post, w_in, q_norm, kv_norm,
           w_uq, w_ukv, hgrn_lb, g_norm, w_out, router_w, router_b, w_gate_up, b_gate_up, w_down, b_down):
    c_all = jnp.concatenate([c, c_ctx[None, :], jnp.zeros((8 - B - 1, D), F32)], axis=0)
    mod = _mod_call(c_all, w_mod[0], b_mod)
    mod3 = mod.reshape(8, 1, 6 * D)

    w = w_in[0]
    o_q, o_kv, o_kr = 0, Q_RANK, Q_RANK + KV_RANK
    o_hq = o_kr + ROPE
    wide = H * HW
    o_ff, o_fb, o_hi, o_hg = o_hq + wide, o_hq + 2 * wide, o_hq + 3 * wide, o_hq + 4 * wide
    w_kr = w[:, o_kr:o_kr + ROPE]
    w_p = jnp.concatenate([w[:, o_q:o_kr], w_kr, _rot_cols(w_kr), jnp.zeros((D, LANES), F32),
                           w[:, o_hq:o_hq + wide], w[:, o_hi:o_hi + wide], w[:, o_hg:o_hg + wide]],
                          axis=1).astype(BF16)
    w_f = w[:, o_ff:o_ff + 2 * wide].astype(BF16)
    wq = w_uq[0]
    wq_r = wq[..., HW:]
    w_q = jnp.concatenate([wq[..., :HW], wq_r, _rot_cols(wq_r)], axis=-1).reshape(Q_RANK, 2 * wide).astype(BF16)
    wkv = w_ukv[0]
    w_kv = jnp.concatenate([wkv[..., :HW].reshape(KV_RANK, wide), wkv[..., HW:].reshape(KV_RANK, wide)],
                           axis=1).astype(BF16)
    w_o = w_out[0].astype(BF16)
    cs_all, cs_q = _rope_tables()
    lb = jnp.cumsum(jax.nn.softmax(hgrn_lb.astype(F32), axis=1), axis=1)
    r_w = jnp.concatenate([router_w[0], jnp.zeros((D, LANES - NE), F32)], axis=1)
    r_w_hi = r_w.astype(BF16)
    r_w = jnp.concatenate([r_w_hi, (r_w - r_w_hi.astype(F32)).astype(BF16)], axis=1)
    r_b = jnp.concatenate([router_b[0], jnp.zeros((LANES - NE,), F32)])[None, :]

    h_all = _normmod_call(x, ctx, mod3, ln_mix_pre)
    h_flat = h_all.reshape(TA, D)
    p = _matmul_call(h_flat, w_p, BF16, "in_proj")
    fraw = _matmul_call(h_flat, w_f, F32, "in_proj_forget")
    k_all, v_all = _kvprep_call(p, kv_norm, w_kv, cs_all)
    q_all = _qprep_call(p, q_norm, w_q, cs_q)
    att = _attn_call(q_all, k_all, v_all)
    o_f, o_b = _hgrn_call(p, fraw, lb[0, 0][None, :], lb[1, 0][None, :])

    g_norm_t = g_norm
    x1, h2w, ids, gates, counts = _mixout_call(att, o_f, o_b, p, g_norm_t, w_o[:wide], w_o[wide:], x, mod3,
                                               ln_mix_post, ln_ffn_pre, r_w, r_b)

    dest, sb_e, sb_n, sb_b = _routing_tables(ids, counts)
    xs = _dispatch_call(dest, h2w, jnp.zeros((N_SUBS * SUB * PACK_W, LANES), I32))
    ys = _expert_call(sb_e, sb_n, sb_b, xs, w_gate_up[0], b_gate_up[0][:, None, :], w_down[0], b_down[0][:, None, :])
    return _combine_call(dest, ys, gates, x1, mod3, ln_ffn_post)
```

```python
import functools

import numpy as np
import jax
import jax.numpy as jnp
from jax import lax
from jax.experimental import pallas as pl
from jax.experimental.pallas import tpu as pltpu

F32 = jnp.float32
BF16 = jnp.bfloat16
I32 = jnp.int32

D = 2048
B = 4
N = 4096
M = 256
NA = N + M
T = B * N
TA = B * NA
EPS = 1e-6
H = 8
HW = 128
Q_RANK = 512
KV_RANK = 256
ROPE = 64
QK_SCALE = float((128 + ROPE) ** -0.5)
GRID_W = 64
ROPE_THETA = 10000.0
NE = 32
TOPK = 4
DFF = 2048
LIMIT = 7.0
ALPHA = 1.702
LOG2E = float(np.log2(np.e))

LANES = 128
SUBLANES = 8
VMEM_LIMIT = 56 * 1024 * 1024

ROW_T = 256
TILES_A = NA // ROW_T
TILES_L = N // ROW_T
MM_TM = 1024
MM_TN = 1024
HG_C = 128
HG_STEPS = NA // HG_C
HG_CTX = M // HG_C
HG_LEVELS = (64, 32, 16, 8, 4, 2, 1)
SUB = 256
N_SUBS = T * TOPK // SUB + NE
SB_SUBS = 10
SB_R = SB_SUBS * SUB
CHUNK_SUBS = 4
S_MAX = -(-N_SUBS // SB_SUBS) + NE
FF_T = 256
N1 = DFF // FF_T
N2 = D // FF_T
PACK_W = D // 2 // LANES
DISP_T = 512
COMBINE_UNROLL = 4


def _sigmoid(x):
    return 1.0 / (1.0 + jnp.exp(-x))


def _rms(x, w):
    ms = jnp.mean(x * x, axis=-1, keepdims=True)
    return x * lax.rsqrt(ms + EPS) * w


def _cparams(sem, vmem=VMEM_LIMIT):
    return pltpu.CompilerParams(dimension_semantics=sem, vmem_limit_bytes=vmem)


def _mod_kernel(c_ref, w_ref, b_ref, o_ref):
    c = c_ref[...]
    a = (c * _sigmoid(c)).astype(BF16)
    o_ref[...] = jnp.dot(a, w_ref[...].astype(BF16), preferred_element_type=F32) + b_ref[...]


def _mod_call(c_all, w_mod, b_mod):
    tn = 1024
    return pl.pallas_call(
        _mod_kernel,
        grid=(6 * D // tn,),
        in_specs=[pl.BlockSpec((8, D), lambda j: (0, 0)),
                  pl.BlockSpec((D, tn), lambda j: (0, j)),
                  pl.BlockSpec((1, tn), lambda j: (0, j))],
        out_specs=pl.BlockSpec((8, tn), lambda j: (0, j)),
        out_shape=jax.ShapeDtypeStruct((8, 6 * D), F32),
        compiler_params=_cparams(("arbitrary",)),
        name="mod",
    )(c_all, w_mod, b_mod)


def _normmod_kernel(x_ref, ctx_ref, sh_ref, sc_ref, w_ref, o_ref):
    def go(src):
        y = _rms(src, w_ref[...])
        o_ref[0] = (y * (1.0 + sc_ref[0]) + sh_ref[0]).astype(BF16)

    i = pl.program_id(1)

    @pl.when(i == 0)
    def _():
        go(ctx_ref[0])

    @pl.when(i > 0)
    def _():
        go(x_ref[0])


def _normmod_call(x, ctx, mod3, ln_w):
    row = lambda b, i: jnp.where(i == 0, B, b)
    return pl.pallas_call(
        _normmod_kernel,
        grid=(B, TILES_A),
        in_specs=[pl.BlockSpec((1, ROW_T, D), lambda b, i: (b, jnp.maximum(i - 1, 0), 0)),
                  pl.BlockSpec((1, ROW_T, D), lambda b, i: (b, 0, 0)),
                  pl.BlockSpec((1, 1, D), lambda b, i: (row(b, i), 0, 0)),
                  pl.BlockSpec((1, 1, D), lambda b, i: (row(b, i), 0, 1)),
                  pl.BlockSpec((1, D), lambda b, i: (0, 0))],
        out_specs=pl.BlockSpec((1, ROW_T, D), lambda b, i: (b, i, 0)),
        out_shape=jax.ShapeDtypeStruct((B, NA, D), BF16),
        compiler_params=_cparams(("arbitrary", "arbitrary")),
        name="normmod",
    )(x, ctx, mod3, mod3, ln_w)


def _matmul_kernel(a_ref, w_ref, o_ref):
    o_ref[...] = jnp.dot(a_ref[...], w_ref[...], preferred_element_type=F32).astype(o_ref.dtype)


def _matmul_call(a, w, out_dtype, name):
    rows, k = a.shape
    cols = w.shape[1]
    return pl.pallas_call(
        _matmul_kernel,
        grid=(rows // MM_TM, cols // MM_TN),
        in_specs=[pl.BlockSpec((MM_TM, k), lambda i, j: (i, 0)),
                  pl.BlockSpec((k, MM_TN), lambda i, j: (0, j))],
        out_specs=pl.BlockSpec((MM_TM, MM_TN), lambda i, j: (i, j)),
        out_shape=jax.ShapeDtypeStruct((rows, cols), out_dtype),
        compiler_params=_cparams(("arbitrary", "arbitrary")),
        name=name,
    )(a, w)


def _kvprep_kernel(p_ref, kvn_ref, w_ref, cs_ref, k_ref, v_ref):
    ckv = p_ref[:, Q_RANK:Q_RANK + KV_RANK].astype(F32)
    ckvn = _rms(ckv, kvn_ref[...]).astype(BF16)
    kv = jnp.dot(ckvn, w_ref[...], preferred_element_type=F32)
    a = p_ref[:, Q_RANK + KV_RANK:Q_RANK + KV_RANK + LANES].astype(F32) * cs_ref[...]
    r = a + pltpu.roll(a, ROPE, axis=1)
    lane = lax.broadcasted_iota(I32, r.shape, 1)
    r = jnp.where(lane < ROPE, r, 0.0).astype(BF16)
    for h in range(H):
        k_ref[0, h, :, 0:HW] = kv[:, h * HW:(h + 1) * HW].astype(BF16)
        k_ref[0, h, :, HW:2 * HW] = r
        v_ref[0, h] = kv[:, (H + h) * HW:(H + h + 1) * HW].astype(BF16)


def _kvprep_call(p, kv_norm, w_kv, cs_all):
    return pl.pallas_call(
        _kvprep_kernel,
        grid=(B, TILES_A),
        in_specs=[pl.BlockSpec((ROW_T, 1024), lambda b, i: (b * TILES_A + i, 0)),
                  pl.BlockSpec((1, KV_RANK), lambda b, i: (0, 0)),
                  pl.BlockSpec((KV_RANK, 2 * H * HW), lambda b, i: (0, 0)),
                  pl.BlockSpec((ROW_T, LANES), lambda b, i: (i, 0))],
        out_specs=[pl.BlockSpec((1, H, ROW_T, 2 * HW), lambda b, i: (b, 0, i, 0)),
                   pl.BlockSpec((1, H, ROW_T, HW), lambda b, i: (b, 0, i, 0))],
        out_shape=[jax.ShapeDtypeStruct((B, H, NA, 2 * HW), BF16),
                   jax.ShapeDtypeStruct((B, H, NA, HW), BF16)],
        compiler_params=_cparams(("arbitrary", "arbitrary")),
        name="kvprep",
    )(p, kv_norm, w_kv, cs_all)


def _qprep_kernel(p_ref, qn_ref, w_ref, cs_ref, q_ref):
    cq = p_ref[:, 0:Q_RANK].astype(F32)
    cqn = _rms(cq, qn_ref[...]).astype(BF16)
    q = jnp.dot(cqn, w_ref[...], preferred_element_type=F32)
    cs = cs_ref[...]
    for h in range(H):
        q_ref[0, h, :, 0:HW] = (q[:, 2 * h * HW:(2 * h + 1) * HW] * QK_SCALE).astype(BF16)
        a = q[:, (2 * h + 1) * HW:(2 * h + 2) * HW] * cs
        q_ref[0, h, :, HW:2 * HW] = (a + pltpu.roll(a, ROPE, axis=1)).astype(BF16)


def _qprep_call(p, q_norm, w_q, cs_q):
    return pl.pallas_call(
        _qprep_kernel,
        grid=(B, TILES_L),
        in_specs=[pl.BlockSpec((ROW_T, 1024), lambda b, i: (b * TILES_A + 1 + i, 0)),
                  pl.BlockSpec((1, Q_RANK), lambda b, i: (0, 0)),
                  pl.BlockSpec((Q_RANK, 2 * H * HW), lambda b, i: (0, 0)),
                  pl.BlockSpec((ROW_T, LANES), lambda b, i: (i, 0))],
        out_specs=pl.BlockSpec((1, H, ROW_T, 2 * HW), lambda b, i: (b, 0, i, 0)),
        out_shape=jax.ShapeDtypeStruct((B, H, N, 2 * HW), BF16),
        compiler_params=_cparams(("arbitrary", "arbitrary")),
        name="qprep",
    )(p, q_norm, w_q, cs_q)


ATT_TQ = 256
ATT_TILES = 8


def _attn_kernel(q_ref, k_ref, v_ref, o_ref):
    for t in range(ATT_TILES):
        rows = slice(t * ATT_TQ, (t + 1) * ATT_TQ)
        s = lax.dot_general(q_ref[0, 0, rows, :], k_ref[0, 0], (((1,), (1,)), ((), ())),
                            preferred_element_type=F32)
        m = jnp.max(s, axis=-1, keepdims=True)
        p = jnp.exp(s - m)
        l = jnp.sum(p, axis=-1, keepdims=True)
        o = jnp.dot(p.astype(BF16), v_ref[0, 0], preferred_element_type=F32)
        o_ref[rows, :] = (o * (1.0 / l)).astype(BF16)


def _attn_call(q, k, v):
    tq = ATT_TQ * ATT_TILES
    return pl.pallas_call(
        _attn_kernel,
        grid=(B, H, N // tq),
        in_specs=[pl.BlockSpec((1, 1, tq, 2 * HW), lambda b, h, i: (b, h, i, 0)),
                  pl.BlockSpec((1, 1, NA, 2 * HW), lambda b, h, i: (b, h, 0, 0)),
                  pl.BlockSpec((1, 1, NA, HW), lambda b, h, i: (b, h, 0, 0))],
        out_specs=pl.BlockSpec((tq, HW), lambda b, h, i: (b * (N // tq) + i, h)),
        out_shape=jax.ShapeDtypeStruct((T, H * HW), BF16),
        compiler_params=_cparams(("arbitrary", "arbitrary", "arbitrary")),
        name="attention",
    )(q, k, v)


def _hgrn_tables(reverse):
    c = HG_C
    t = np.arange(c)[:, None]
    u = np.arange(c)[None, :]
    cum = (u <= t).astype(np.float32)
    masks = [(u == t)]
    for m in HG_LEVELS:
        blk_t = t // m
        masks.append(((blk_t % 2) == 1) & ((u // m) == blk_t - 1))
    masks = np.stack(masks).astype(np.float32)
    if reverse:
        cum = cum[::-1, ::-1]
        masks = masks[:, ::-1, ::-1]
    return jnp.asarray(cum, dtype=BF16), jnp.asarray(masks.reshape(-1, c), dtype=F32)


def _pair_boundary(c_ref, cval, m, reverse):
    def rows_of(r_of_group):
        return jnp.concatenate([jnp.broadcast_to(c_ref[r_of_group(gi):r_of_group(gi) + 1, :], (SUBLANES, HW))
                                for gi in range(HG_C // SUBLANES)], axis=0)

    if m >= 4:
        def boundary(gi):
            pair = (SUBLANES * gi // (2 * m)) * 2 * m
            return pair + m if reverse else pair + m - 1
        return rows_of(boundary)
    row = lax.broadcasted_iota(I32, (HG_C, HW), 0)
    if m == 2:
        first = rows_of(lambda gi: SUBLANES * gi + (2 if reverse else 1))
        second = rows_of(lambda gi: SUBLANES * gi + (6 if reverse else 5))
        return jnp.where((row & 7) < 4, first, second)
    if reverse:
        return jnp.where((row & 1) == 0, pltpu.roll(cval, HG_C - 1, 0), cval)
    return jnp.where((row & 1) == 1, pltpu.roll(cval, 1, 0), cval)


def _hgrn_head(h, hq_ref, hi_ref, fr_ref, lb_ref, cum_ref, mk_ref, s_ref, o_ref, c_ref, reverse):
    c = HG_C
    col = pl.ds(pl.multiple_of(h * HW, HW), HW)
    xq = hq_ref[:, col].astype(F32)
    q = xq * _sigmoid(xq)
    lb = lb_ref[:, col]
    f = lb + (1.0 - lb) * _sigmoid(fr_ref[:, col])
    g = jnp.log(f)
    k = 1.0 - f
    v = hi_ref[:, col]
    g_hi = g.astype(BF16)
    g_lo = (g - g_hi.astype(F32)).astype(BF16)
    cum = cum_ref[...]
    cs = jnp.dot(jnp.concatenate([cum, cum], axis=1), jnp.concatenate([g_hi, g_lo], axis=0),
                 preferred_element_type=F32) * LOG2E
    c_ref[...] = cs
    total_row = 0 if reverse else c - 1
    c_tot = c_ref[total_row:total_row + 1, :]

    nt = (((1,), (1,)), ((), ()))
    st = s_ref[h]
    o = lax.dot_general((q * jnp.exp2(cs)).astype(BF16), st.astype(BF16), nt, preferred_element_type=F32)
    qb = q.astype(BF16)
    kb = k.astype(BF16)
    sc = lax.dot_general(qb, kb, nt, preferred_element_type=F32) * mk_ref[0:c, :]
    sign = jnp.uint32(0x80000000)
    for li, m in enumerate(HG_LEVELS):
        z = cs - _pair_boundary(c_ref, cs, m, reverse)
        w = jnp.exp2(pltpu.bitcast(pltpu.bitcast(z, jnp.uint32) | sign, F32)).astype(BF16)
        sc = sc + (lax.dot_general(qb * w, kb * w, nt, preferred_element_type=F32)
                   * mk_ref[(li + 1) * c:(li + 2) * c, :])
    o = o + jnp.dot(sc.astype(BF16), v, preferred_element_type=F32)
    o_ref[:, col] = o.astype(BF16)
    kend = (k * jnp.exp2(c_tot - cs)).astype(BF16)
    s_ref[h] = st * jnp.exp2(c_tot) + lax.dot_general(v, kend, (((0,), (0,)), ((), ())), preferred_element_type=F32)


HG_UNROLL = 8


def _hgrn_kernel(hq_f, hi_f, fr_f, hq_b, hi_b, fr_b, lbf_ref, lbb_ref, cumf_ref, mkf_ref, cumb_ref, mkb_ref,
                 of_ref, ob_ref, sf_ref, sb_ref, c_s):
    @pl.when(pl.program_id(1) == 0)
    def _():
        sf_ref[...] = jnp.zeros_like(sf_ref)
        sb_ref[...] = jnp.zeros_like(sb_ref)

    def body(i, carry):
        for hh in range(HG_UNROLL):
            h = i * HG_UNROLL + hh
            _hgrn_head(h, hq_f, hi_f, fr_f, lbf_ref, cumf_ref, mkf_ref, sf_ref, of_ref, c_s.at[2 * hh], False)
            _hgrn_head(h, hq_b, hi_b, fr_b, lbb_ref, cumb_ref, mkb_ref, sb_ref, ob_ref, c_s.at[2 * hh + 1], True)
        return carry

    lax.fori_loop(0, H // HG_UNROLL, body, 0)


def _hgrn_call(p, fraw, lb_f, lb_b):
    cum_f, mk_f = _hgrn_tables(False)
    cum_b, mk_b = _hgrn_tables(True)
    wide = H * HW

    def cf(b, s):
        return b * HG_STEPS + s

    def cb(b, s):
        return b * HG_STEPS + jnp.where(s < HG_CTX, HG_CTX - 1 - s, HG_STEPS - 1 + HG_CTX - s)

    const = lambda shape: pl.BlockSpec(shape, lambda b, s: (0, 0))
    return pl.pallas_call(
        _hgrn_kernel,
        grid=(B, HG_STEPS),
        in_specs=[pl.BlockSpec((HG_C, wide), lambda b, s: (cf(b, s), 1)),
                  pl.BlockSpec((HG_C, wide), lambda b, s: (cf(b, s), 2)),
                  pl.BlockSpec((HG_C, wide), lambda b, s: (cf(b, s), 0)),
                  pl.BlockSpec((HG_C, wide), lambda b, s: (cb(b, s), 1)),
                  pl.BlockSpec((HG_C, wide), lambda b, s: (cb(b, s), 2)),
                  pl.BlockSpec((HG_C, wide), lambda b, s: (cb(b, s), 1)),
                  const((1, wide)), const((1, wide)),
                  const(cum_f.shape), const(mk_f.shape), const(cum_b.shape), const(mk_b.shape)],
        out_specs=[pl.BlockSpec((HG_C, wide), lambda b, s: (cf(b, s), 0)),
                   pl.BlockSpec((HG_C, wide), lambda b, s: (cb(b, s), 0))],
        out_shape=[jax.ShapeDtypeStruct((TA, wide), BF16), jax.ShapeDtypeStruct((TA, wide), BF16)],
        scratch_shapes=[pltpu.VMEM((H, HW, HW), F32), pltpu.VMEM((H, HW, HW), F32),
                        pltpu.VMEM((2 * HG_UNROLL, HG_C, HW), F32)],
        compiler_params=_cparams(("arbitrary", "arbitrary")),
        name="hgrn",
    )(p, p, fraw, p, p, fraw, lb_f, lb_b, cum_f, mk_f, cum_b, mk_b)


def _pack_rows(val, dst_ref, row0, rows):
    for s in range(PACK_W):
        lo = val[:, 2 * s * LANES:(2 * s + 1) * LANES]
        hi = val[:, (2 * s + 1) * LANES:(2 * s + 2) * LANES]
        w = pltpu.pack_elementwise([lo, hi], packed_dtype=BF16)
        dst_ref[pl.ds(row0 * PACK_W + s, rows, stride=PACK_W), :] = w


def _unpack_word(w):
    lo = pltpu.unpack_elementwise(w, index=0, packed_dtype=BF16, unpacked_dtype=F32)
    hi = pltpu.unpack_elementwise(w, index=1, packed_dtype=BF16, unpacked_dtype=F32)
    return lo, hi


def _mixout_kernel(att_ref, of_ref, ob_ref, hg_ref, gn_ref, wa_ref, wr_ref, x_ref, gt_ref, shf_ref, scf_ref,
                   lnp_ref, lnf_ref, rw_ref, rb_ref,
                   x1_ref, h2w_ref, ids_ref, gates_ref, cnt_ref, carry_ref):
    first = (pl.program_id(0) == 0) & (pl.program_id(1) == 0)

    @pl.when(first)
    def _():
        carry_ref[...] = jnp.zeros_like(carry_ref)

    o = of_ref[...].astype(F32) + ob_ref[...].astype(F32)
    gn = gn_ref[...]
    recs = []
    for h in range(H):
        oh = o[:, h * HW:(h + 1) * HW]
        gate = hg_ref[:, h * HW:(h + 1) * HW].astype(F32)
        recs.append((_rms(oh, gn) * (gate * _sigmoid(gate))).astype(BF16))
    rec = jnp.concatenate(recs, axis=-1)
    mix = (jnp.dot(att_ref[...], wa_ref[...], preferred_element_type=F32)
           + jnp.dot(rec, wr_ref[...], preferred_element_type=F32))
    x1 = x_ref[0] + gt_ref[0] * _rms(mix, lnp_ref[...])
    x1_ref[0] = x1
    h2 = _rms(x1, lnf_ref[...]) * (1.0 + scf_ref[0]) + shf_ref[0]
    _pack_rows(h2, h2w_ref, 0, ROW_T)

    h_hi = h2.astype(BF16)
    h_lo = (h2 - h_hi.astype(F32)).astype(BF16)
    hh = jnp.dot(h_hi, rw_ref[...], preferred_element_type=F32)
    logits = (hh[:, :LANES] + hh[:, LANES:] + jnp.dot(h_lo, rw_ref[:, :LANES], preferred_element_type=F32)
              + rb_ref[...])
    lane = lax.broadcasted_iota(I32, logits.shape, 1)
    neg = jnp.float32(-jnp.inf)
    lg = jnp.where(lane < NE, logits, neg)
    tops, idxs = [], []
    for _ in range(TOPK):
        m = jnp.max(lg, axis=-1, keepdims=True)
        idx = jnp.min(jnp.where(lg == m, lane, LANES), axis=-1, keepdims=True)
        tops.append(m)
        idxs.append(idx)
        lg = jnp.where(lane == idx, neg, lg)
    exps = [jnp.exp(tk - tops[0]) for tk in tops]
    inv = 1.0 / (exps[0] + exps[1] + exps[2] + exps[3])
    onehot = jnp.zeros(logits.shape, F32)
    for idx in idxs:
        onehot = onehot + jnp.where(lane == idx, 1.0, 0.0)
    r = lax.broadcasted_iota(I32, (ROW_T, ROW_T), 0)
    cidx = lax.broadcasted_iota(I32, (ROW_T, ROW_T), 1)
    lower = jnp.where(cidx < r, 1.0, 0.0).astype(BF16)
    prefix = jnp.dot(lower, onehot.astype(BF16), preferred_element_type=F32) + carry_ref[...]
    ids = jnp.zeros(logits.shape, I32)
    gates = jnp.zeros(logits.shape, F32)
    for kk in range(TOPK):
        rank = jnp.sum(jnp.where(lane == idxs[kk], prefix, 0.0), axis=-1, keepdims=True).astype(I32)
        ids = jnp.where(lane == kk, idxs[kk], ids)
        ids = jnp.where(lane == TOPK + kk, rank, ids)
        gates = jnp.where(lane == kk, exps[kk] * inv, gates)
    ids_ref[...] = ids
    gates_ref[...] = gates
    carry = carry_ref[...] + jnp.sum(onehot, axis=0, keepdims=True)
    carry_ref[...] = carry
    cnt_ref[...] = carry


def _mixout_call(att, o_f, o_b, p, g_norm_t, w_a, w_r, x, mod3, ln_post, ln_ffn, r_w, r_b):
    wide = H * HW
    lat = lambda b, i: b * TILES_A + 1 + i
    tok = lambda b, i: b * TILES_L + i
    const = lambda shape: pl.BlockSpec(shape, lambda b, i: tuple(0 for _ in shape))
    return pl.pallas_call(
        _mixout_kernel,
        grid=(B, TILES_L),
        in_specs=[pl.BlockSpec((ROW_T, wide), lambda b, i: (tok(b, i), 0)),
                  pl.BlockSpec((ROW_T, wide), lambda b, i: (lat(b, i), 0)),
                  pl.BlockSpec((ROW_T, wide), lambda b, i: (lat(b, i), 0)),
                  pl.BlockSpec((ROW_T, wide), lambda b, i: (lat(b, i), 3)),
                  const((1, HW)), const((wide, D)), const((wide, D)),
                  pl.BlockSpec((1, ROW_T, D), lambda b, i: (b, i, 0)),
                  pl.BlockSpec((1, 1, D), lambda b, i: (b, 0, 2)),
                  pl.BlockSpec((1, 1, D), lambda b, i: (b, 0, 3)),
                  pl.BlockSpec((1, 1, D), lambda b, i: (b, 0, 4)),
                  const((1, D)), const((1, D)), const((D, 2 * LANES)), const((1, LANES))],
        out_specs=[pl.BlockSpec((1, ROW_T, D), lambda b, i: (b, i, 0)),
                   pl.BlockSpec((ROW_T * PACK_W, LANES), lambda b, i: (tok(b, i), 0)),
                   pl.BlockSpec((ROW_T, LANES), lambda b, i: (tok(b, i), 0)),
                   pl.BlockSpec((ROW_T, LANES), lambda b, i: (tok(b, i), 0)),
                   const((1, LANES))],
        out_shape=[jax.ShapeDtypeStruct((B, N, D), F32),
                   jax.ShapeDtypeStruct((T * PACK_W, LANES), I32),
                   jax.ShapeDtypeStruct((T, LANES), I32),
                   jax.ShapeDtypeStruct((T, LANES), F32),
                   jax.ShapeDtypeStruct((1, LANES), F32)],
        scratch_shapes=[pltpu.VMEM((1, LANES), F32)],
        compiler_params=_cparams(("arbitrary", "arbitrary")),
        name="mixout_router",
    )(att, o_f, o_b, p, g_norm_t, w_a, w_r, x, mod3, mod3, mod3, ln_post, ln_ffn, r_w, r_b)


def _dispatch_copy(src_vmem, dst_hbm, sem, src_row, dst_row, rows):
    return pltpu.make_async_copy(src_vmem.at[pl.ds(src_row * PACK_W, rows * PACK_W)],
                                 dst_hbm.at[pl.ds(dst_row * PACK_W, rows * PACK_W)], sem)


def _dispatch_kernel(dest_ref, h2w_ref, init_hbm, xs_hbm, sem):
    del init_hbm
    base = pl.program_id(0) * DISP_T

    def issue(t, carry):
        for kk in range(TOPK):
            _dispatch_copy(h2w_ref, xs_hbm, sem, t, dest_ref[(base + t) * TOPK + kk], 1).start()
        return carry

    lax.fori_loop(0, DISP_T, issue, 0)
    for _ in range(TOPK):
        _dispatch_copy(h2w_ref, xs_hbm, sem, 0, 0, DISP_T).wait()


def _dispatch_call(dest, h2w, xs_init):
    gs = pltpu.PrefetchScalarGridSpec(
        num_scalar_prefetch=1, grid=(T // DISP_T,),
        in_specs=[pl.BlockSpec((DISP_T * PACK_W, LANES), lambda i, d: (i, 0)),
                  pl.BlockSpec(memory_space=pl.ANY)],
        out_specs=pl.BlockSpec(memory_space=pl.ANY),
        scratch_shapes=[pltpu.SemaphoreType.DMA(())])
    return pl.pallas_call(
        _dispatch_kernel, grid_spec=gs,
        out_shape=jax.ShapeDtypeStruct(xs_init.shape, xs_init.dtype),
        input_output_aliases={2: 0},
        compiler_params=_cparams(("arbitrary",)),
        name="moe_dispatch",
    )(dest, h2w, xs_init)


def _sub_in_copy(xs_hbm, stage, sems, blk, slot):
    return pltpu.make_async_copy(xs_hbm.at[pl.ds(blk * (SUB * PACK_W), SUB * PACK_W)], stage.at[slot], sems.at[slot])


def _sub_out_copy(stage, out_hbm, sems, blk, slot):
    return pltpu.make_async_copy(stage.at[slot], out_hbm.at[pl.ds(blk * (SUB * PACK_W), SUB * PACK_W)], sems.at[slot])


def _for_row_chunks(nblk, fn):
    pair = 2 * CHUNK_SUBS

    @pl.when(nblk >= pair)
    def _():
        fn(0, CHUNK_SUBS * SUB)
        fn(CHUNK_SUBS * SUB, CHUNK_SUBS * SUB)

    for c in range(-(-SB_SUBS // CHUNK_SUBS)):
        row0 = c * CHUNK_SUBS * SUB
        most = min(CHUNK_SUBS, SB_SUBS - c * CHUNK_SUBS)
        for r in range(1, most + 1):
            full = r == CHUNK_SUBS
            cond = (nblk >= c * CHUNK_SUBS + r) if full else (nblk == c * CHUNK_SUBS + r)
            if c < 2:
                cond = cond & (nblk < pair)
            pl.when(cond)(functools.partial(fn, row0, r * SUB))


def _wtile_copy(w_hbm, buf, sems, e, col_tile, slot, which):
    return pltpu.make_async_copy(w_hbm.at[e, :, pl.ds(pl.multiple_of(col_tile * FF_T, FF_T), FF_T)],
                                 buf.at[slot, which], sems.at[slot, which])


def _expert_kernel(sbe_ref, sbn_ref, sbb_ref, xs_hbm, wgu_hbm, bgu_ref, wdn_hbm, bdn_ref,
                   out_hbm, x_s, act_s, outw_s, wbuf, stage_in, stage_out,
                   sem_w, sem_in, sem_out):
    s = pl.program_id(0)
    nblk = sbn_ref[s]
    blk0 = sbb_ref[s]
    e = sbe_ref[s]

    def gate_tiles(j, slot):
        return (_wtile_copy(wgu_hbm, wbuf, sem_w, e, j, slot, 0),
                _wtile_copy(wgu_hbm, wbuf, sem_w, e, N1 + j, slot, 1))

    def down_tile(j, slot):
        return _wtile_copy(wdn_hbm, wbuf, sem_w, e, j, slot, 0)

    @pl.when(nblk > 0)
    def _():
        for cp in gate_tiles(0, 0):
            cp.start()

        _sub_in_copy(xs_hbm, stage_in, sem_in, blk0, 0).start()
        for i in range(SB_SUBS):
            @pl.when(i < nblk)
            def _():
                if i + 1 < SB_SUBS:
                    @pl.when(i + 1 < nblk)
                    def _():
                        _sub_in_copy(xs_hbm, stage_in, sem_in, blk0 + i + 1, (i + 1) % 2).start()
                _sub_in_copy(xs_hbm, stage_in, sem_in, blk0 + i, i % 2).wait()
                for w in range(PACK_W):
                    lo, hi = _unpack_word(stage_in[i % 2, pl.ds(w, SUB, stride=PACK_W), :])
                    x_s[i * SUB:(i + 1) * SUB, 2 * w * LANES:(2 * w + 1) * LANES] = lo.astype(BF16)
                    x_s[i * SUB:(i + 1) * SUB, (2 * w + 1) * LANES:(2 * w + 2) * LANES] = hi.astype(BF16)

        def hidden_step(j, carry):
            slot = j % 2
            for cp in gate_tiles(j, slot):
                cp.wait()

            @pl.when(j + 1 < N1)
            def _():
                for cp in gate_tiles(j + 1, 1 - slot):
                    cp.start()

            @pl.when(j + 1 == N1)
            def _():
                down_tile(0, 1 - slot).start()

            col = pl.ds(pl.multiple_of(j * FF_T, FF_T), FF_T)
            bg = bgu_ref[0, :, col]
            bl = bgu_ref[0, :, pl.ds(pl.multiple_of(DFF + j * FF_T, FF_T), FF_T)]

            def gate_up(row0, rows):
                x = x_s[row0:row0 + rows, :]
                gu = jnp.dot(x, wbuf[slot, 0].astype(BF16), preferred_element_type=F32) + bg
                li = jnp.dot(x, wbuf[slot, 1].astype(BF16), preferred_element_type=F32) + bl
                glu = jnp.minimum(gu, LIMIT)
                lin = jnp.clip(li, -LIMIT, LIMIT)
                act = glu * _sigmoid(ALPHA * glu) * (lin + 1.0)
                act_s[row0:row0 + rows, col] = act.astype(BF16)

            _for_row_chunks(nblk, gate_up)
            return carry

        lax.fori_loop(0, N1, hidden_step, 0)

        def out_step(j2, carry):
            slot = (N1 + j2) % 2
            down_tile(j2, slot).wait()

            @pl.when(j2 + 1 < N2)
            def _():
                down_tile(j2 + 1, 1 - slot).start()

            bd = bdn_ref[0, :, pl.ds(pl.multiple_of(j2 * FF_T, FF_T), FF_T)]

            def down(row0, rows):
                acc = jnp.dot(act_s[row0:row0 + rows, :], wbuf[slot, 0].astype(BF16), preferred_element_type=F32) + bd
                for half in range(FF_T // (2 * LANES)):
                    lo = acc[:, 2 * half * LANES:(2 * half + 1) * LANES]
                    hi = acc[:, (2 * half + 1) * LANES:(2 * half + 2) * LANES]
                    outw_s[j2 * (FF_T // (2 * LANES)) + half, row0:row0 + rows, :] = (
                        pltpu.pack_elementwise([lo, hi], packed_dtype=BF16))

            _for_row_chunks(nblk, down)
            return carry

        lax.fori_loop(0, N2, out_step, 0)

        for i in range(SB_SUBS):
            @pl.when(i < nblk)
            def _():
                if i >= 2:
                    _sub_out_copy(stage_out, out_hbm, sem_out, blk0 + i - 2, i % 2).wait()
                for w in range(PACK_W):
                    stage_out[i % 2, pl.ds(w, SUB, stride=PACK_W), :] = outw_s[w, i * SUB:(i + 1) * SUB, :]
                _sub_out_copy(stage_out, out_hbm, sem_out, blk0 + i, i % 2).start()
        for i in range(SB_SUBS):
            @pl.when((i < nblk) & (i + 2 >= nblk))
            def _():
                _sub_out_copy(stage_out, out_hbm, sem_out, blk0 + i, i % 2).wait()


def _expert_call(sb_e, sb_n, sb_b, xs, w_gu, b_gu, w_dn, b_dn):
    gs = pltpu.PrefetchScalarGridSpec(
        num_scalar_prefetch=3, grid=(S_MAX,),
        in_specs=[pl.BlockSpec(memory_space=pl.ANY),
                  pl.BlockSpec(memory_space=pl.ANY),
                  pl.BlockSpec((1, 1, 2 * DFF), lambda s, e, n, b: (e[s], 0, 0)),
                  pl.BlockSpec(memory_space=pl.ANY),
                  pl.BlockSpec((1, 1, D), lambda s, e, n, b: (e[s], 0, 0))],
        out_specs=pl.BlockSpec(memory_space=pl.ANY),
        scratch_shapes=[pltpu.VMEM((SB_R, D), BF16),
                        pltpu.VMEM((SB_R, DFF), BF16),
                        pltpu.VMEM((PACK_W, SB_R, LANES), I32),
                        pltpu.VMEM((2, 2, D, FF_T), F32),
                        pltpu.VMEM((2, SUB * PACK_W, LANES), I32),
                        pltpu.VMEM((2, SUB * PACK_W, LANES), I32),
                        pltpu.SemaphoreType.DMA((2, 2)),
                        pltpu.SemaphoreType.DMA((2,)),
                        pltpu.SemaphoreType.DMA((2,))])
    return pl.pallas_call(
        _expert_kernel, grid_spec=gs,
        out_shape=jax.ShapeDtypeStruct(xs.shape, xs.dtype),
        input_output_aliases={3: 0},
        compiler_params=_cparams(("arbitrary",)),
        name="moe_experts",
    )(sb_e, sb_n, sb_b, xs, w_gu, b_gu, w_dn, b_dn)


def _combine_copy(src_hbm, buf, sem, src_row, dst_row, rows):
    return pltpu.make_async_copy(src_hbm.at[pl.ds(src_row * PACK_W, rows * PACK_W)],
                                 buf.at[pl.ds(dst_row * PACK_W, rows * PACK_W)], sem)


def _combine_kernel(dest_ref, ys_hbm, gates_ref, x1_ref, gt_ref, ln_ref, o_ref, bufs, y_s, sems):
    step = pl.program_id(0) * TILES_L + pl.program_id(1)

    def issue(tile, slot):
        def body(tg, carry):
            for tt in range(COMBINE_UNROLL):
                t = tg * COMBINE_UNROLL + tt
                for kk in range(TOPK):
                    _combine_copy(ys_hbm, bufs.at[slot], sems.at[slot], dest_ref[(tile * ROW_T + t) * TOPK + kk],
                                  kk * ROW_T + t, 1).start()
            return carry
        lax.fori_loop(0, ROW_T // COMBINE_UNROLL, body, 0)

    @pl.when(step == 0)
    def _():
        issue(0, 0)

    @pl.when(step + 1 < B * TILES_L)
    def _():
        issue(step + 1, (step + 1) % 2)

    slot = step % 2
    buf = bufs.at[slot]
    _combine_copy(ys_hbm, buf, sems.at[slot], 0, 0, ROW_T * TOPK).wait()

    gates = gates_ref[...]
    gk = [jnp.broadcast_to(gates[:, kk:kk + 1], (ROW_T, LANES)) for kk in range(TOPK)]
    for w in range(PACK_W):
        acc_lo = jnp.zeros((ROW_T, LANES), F32)
        acc_hi = jnp.zeros((ROW_T, LANES), F32)
        for kk in range(TOPK):
            lo, hi = _unpack_word(buf[pl.ds(kk * ROW_T * PACK_W + w, ROW_T, stride=PACK_W), :])
            acc_lo = acc_lo + gk[kk] * lo
            acc_hi = acc_hi + gk[kk] * hi
        y_s[:, 2 * w * LANES:(2 * w + 1) * LANES] = acc_lo
        y_s[:, (2 * w + 1) * LANES:(2 * w + 2) * LANES] = acc_hi
    o_ref[0] = x1_ref[0] + gt_ref[0] * _rms(y_s[...], ln_ref[...])


def _combine_call(dest, ys, gates, x1, mod3, ln_post):
    tok = lambda b, i: b * TILES_L + i
    gs = pltpu.PrefetchScalarGridSpec(
        num_scalar_prefetch=1, grid=(B, TILES_L),
        in_specs=[pl.BlockSpec(memory_space=pl.ANY),
                  pl.BlockSpec((ROW_T, LANES), lambda b, i, d: (tok(b, i), 0)),
                  pl.BlockSpec((1, ROW_T, D), lambda b, i, d: (b, i, 0)),
                  pl.BlockSpec((1, 1, D), lambda b, i, d: (b, 0, 5)),
                  pl.BlockSpec((1, D), lambda b, i, d: (0, 0))],
        out_specs=pl.BlockSpec((1, ROW_T, D), lambda b, i, d: (b, i, 0)),
        scratch_shapes=[pltpu.VMEM((2, TOPK * ROW_T * PACK_W, LANES), I32),
                        pltpu.VMEM((ROW_T, D), F32),
                        pltpu.SemaphoreType.DMA((2,))])
    return pl.pallas_call(
        _combine_kernel, grid_spec=gs,
        out_shape=jax.ShapeDtypeStruct((B, N, D), F32),
        compiler_params=_cparams(("arbitrary", "arbitrary")),
        name="moe_combine",
    )(dest, ys, gates, x1, mod3, ln_post)


def _rot_cols(w):
    half = ROPE // 2
    return jnp.concatenate([-w[..., half:], w[..., :half]], axis=-1)


def _rope_tables():
    t = np.arange(N)
    r = (t // GRID_W).astype(np.float64)
    cl = (t % GRID_W).astype(np.float64)
    n_freq = ROPE // 4
    inv = ROPE_THETA ** (-np.arange(n_freq, dtype=np.float64) / n_freq)
    ang = np.concatenate([r[:, None] * inv, cl[:, None] * inv], axis=-1)
    cos, sin = np.cos(ang), np.sin(ang)
    cs = np.concatenate([cos, cos, sin, sin], axis=-1).astype(np.float32)
    ctx = np.concatenate([np.ones((M, ROPE), np.float32), np.zeros((M, ROPE), np.float32)], axis=-1)
    return jnp.asarray(np.concatenate([ctx, cs], axis=0)), jnp.asarray(cs * QK_SCALE)


def _routing_tables(ids, counts_f):
    e = ids[:, :TOPK]
    rank = ids[:, TOPK:2 * TOPK]
    counts = counts_f[0, :NE].astype(I32)
    nsub = (counts + SUB - 1) // SUB
    sub_start = jnp.cumsum(nsub) - nsub
    onehot = e[..., None] == jnp.arange(NE, dtype=I32)
    dest = jnp.sum(jnp.where(onehot, sub_start * SUB, 0), axis=-1) + rank
    nsb = (nsub + SB_SUBS - 1) // SB_SUBS
    sb_end = jnp.cumsum(nsb)
    sb_start = sb_end - nsb
    n_active = sb_end[-1]
    s_idx = jnp.arange(S_MAX, dtype=I32)
    s_eff = jnp.minimum(s_idx, n_active - 1)
    sb_e = jnp.minimum(jnp.searchsorted(sb_end, s_eff, side="right"), NE - 1).astype(I32)
    local = s_eff - sb_start[sb_e]
    sb_b = (sub_start[sb_e] + local * SB_SUBS).astype(I32)
    sb_n = jnp.where(s_idx < n_active, jnp.clip(nsub[sb_e] - local * SB_SUBS, 0, SB_SUBS), 0).astype(I32)
    return dest.reshape(-1).astype(I32), sb_e, sb_n, sb_b


def kernel(x, c, ctx, c_ctx, w_mod, b_mod, ln_mix_pre, ln_mix_post, ln_ffn_pre, ln_ffn_post, w_in, q_norm, kv_norm,
           w_uq, w_ukv, hgrn_lb, g_norm, w_out, router_w, router_b, w_gate_up, b_gate_up, w_down, b_down):
    c_all = jnp.concatenate([c, c_ctx[None, :], jnp.zeros((8 - B - 1, D), F32)], axis=0)
    mod = _mod_call(c_all, w_mod[0], b_mod)
    mod3 = mod.reshape(8, 1, 6 * D)

    w = w_in[0]
    o_q, o_kv, o_kr = 0, Q_RANK, Q_RANK + KV_RANK
    o_hq = o_kr + ROPE
    wide = H * HW
    o_ff, o_fb, o_hi, o_hg = o_hq + wide, o_hq + 2 * wide, o_hq + 3 * wide, o_hq + 4 * wide
    w_kr = w[:, o_kr:o_kr + ROPE]
    w_p = jnp.concatenate([w[:, o_q:o_kr], w_kr, _rot_cols(w_kr), jnp.zeros((D, LANES), F32),
                           w[:, o_hq:o_hq + wide], w[:, o_hi:o_hi + wide], w[:, o_hg:o_hg + wide]],
                          axis=1).astype(BF16)
    w_f = w[:, o_ff:o_ff + 2 * wide].astype(BF16)
    wq = w_uq[0]
    wq_r = wq[..., HW:]
    w_q = jnp.concatenate([wq[..., :HW], wq_r, _rot_cols(wq_r)], axis=-1).reshape(Q_RANK, 2 * wide).astype(BF16)
    wkv = w_ukv[0]
    w_kv = jnp.concatenate([wkv[..., :HW].reshape(KV_RANK, wide), wkv[..., HW:].reshape(KV_RANK, wide)],
                           axis=1).astype(BF16)
    w_o = w_out[0].astype(BF16)
    cs_all, cs_q = _rope_tables()
    lb = jnp.cumsum(jax.nn.softmax(hgrn_lb.astype(F32), axis=1), axis=1)
    r_w = jnp.concatenate([router_w[0], jnp.zeros((D, LANES - NE), F32)], axis=1)
    r_w_hi = r_w.astype(BF16)
    r_w = jnp.concatenate([r_w_hi, (r_w - r_w_hi.astype(F32)).astype(BF16)], axis=1)
    r_b = jnp.concatenate([router_b[0], jnp.zeros((LANES - NE,), F32)])[None, :]

    h_all = _normmod_call(x, ctx, mod3, ln_mix_pre)
    h_flat = h_all.reshape(TA, D)
    p = _matmul_call(h_flat, w_p, BF16, "in_proj")
    fraw = _matmul_call(h_flat, w_f, F32, "in_proj_forget")
    k_all, v_all = _kvprep_call(p, kv_norm, w_kv, cs_all)
    q_all = _qprep_call(p, q_norm, w_q, cs_q)
    att = _attn_call(q_all, k_all, v_all)
    o_f, o_b = _hgrn_call(p, fraw, lb[0, 0][None, :], lb[1, 0][None, :])

    g_norm_t = g_norm
    x1, h2w, ids, gates, counts = _mixout_call(att, o_f, o_b, p, g_norm_t, w_o[:wide], w_o[wide:], x, mod3,
                                               ln_mix_post, ln_ffn_pre, r_w, r_b)

    dest, sb_e, sb_n, sb_b = _routing_tables(ids, counts)
    xs = _dispatch_call(dest, h2w, jnp.zeros((N_SUBS * SUB * PACK_W, LANES), I32))
    ys = _expert_call(sb_e, sb_n, sb_b, xs, w_gate_up[0], b_gate_up[0][:, None, :], w_down[0], b_down[0][:, None, :])
    return _combine_call(dest, ys, gates, x1, mod3, ln_ffn_post)
```

```python
import functools

import numpy as np
import jax
import jax.numpy as jnp
from jax import lax
from jax.experimental import pallas as pl
from jax.experimental.pallas import tpu as pltpu

F32 = jnp.float32
BF16 = jnp.bfloat16
I32 = jnp.int32

D = 2048
B = 4
N = 4096
M = 256
NA = N + M
T = B * N
TA = B * NA
EPS = 1e-6
H = 8
HW = 128
Q_RANK = 512
KV_RANK = 256
ROPE = 64
QK_SCALE = float((128 + ROPE) ** -0.5)
GRID_W = 64
ROPE_THETA = 10000.0
NE = 32
TOPK = 4
DFF = 2048
LIMIT = 7.0
ALPHA = 1.702
LOG2E = float(np.log2(np.e))

LANES = 128
SUBLANES = 8
VMEM_LIMIT = 56 * 1024 * 1024

ROW_T = 256
TILES_A = NA // ROW_T
TILES_L = N // ROW_T
MM_TM = 1024
MM_TN = 1024
HG_C = 128
HG_STEPS = NA // HG_C
HG_CTX = M // HG_C
HG_LEVELS = (64, 32, 16, 8, 4, 2, 1)
SUB = 256
N_SUBS = T * TOPK // SUB + NE
SB_SUBS = 14
SB_R = SB_SUBS * SUB
CHUNK_SUBS = 4
S_MAX = -(-N_SUBS // SB_SUBS) + NE
FF_T = 256
N1 = DFF // FF_T
N2 = D // FF_T
PACK_W = D // 2 // LANES
DISP_T = 512
COMBINE_UNROLL = 4


def _sigmoid(x):
    return 1.0 / (1.0 + jnp.exp(-x))


def _rms(x, w):
    ms = jnp.mean(x * x, axis=-1, keepdims=True)
    return x * lax.rsqrt(ms + EPS) * w


def _cparams(sem, vmem=VMEM_LIMIT):
    return pltpu.CompilerParams(dimension_semantics=sem, vmem_limit_bytes=vmem)


def _mod_kernel(c_ref, w_ref, b_ref, o_ref):
    c = c_ref[...]
    a = (c * _sigmoid(c)).astype(BF16)
    o_ref[...] = jnp.dot(a, w_ref[...].astype(BF16), preferred_element_type=F32) + b_ref[...]


def _mod_call(c_all, w_mod, b_mod):
    tn = 1024
    return pl.pallas_call(
        _mod_kernel,
        grid=(6 * D // tn,),
        in_specs=[pl.BlockSpec((8, D), lambda j: (0, 0)),
                  pl.BlockSpec((D, tn), lambda j: (0, j)),
                  pl.BlockSpec((1, tn), lambda j: (0, j))],
        out_specs=pl.BlockSpec((8, tn), lambda j: (0, j)),
        out_shape=jax.ShapeDtypeStruct((8, 6 * D), F32),
        compiler_params=_cparams(("arbitrary",)),
        name="mod",
    )(c_all, w_mod, b_mod)


def _normmod_kernel(x_ref, ctx_ref, sh_ref, sc_ref, w_ref, o_ref):
    def go(src):
        y = _rms(src, w_ref[...])
        o_ref[0] = (y * (1.0 + sc_ref[0]) + sh_ref[0]).astype(BF16)

    i = pl.program_id(1)

    @pl.when(i == 0)
    def _():
        go(ctx_ref[0])

    @pl.when(i > 0)
    def _():
        go(x_ref[0])


def _normmod_call(x, ctx, mod3, ln_w):
    row = lambda b, i: jnp.where(i == 0, B, b)
    return pl.pallas_call(
        _normmod_kernel,
        grid=(B, TILES_A),
        in_specs=[pl.BlockSpec((1, ROW_T, D), lambda b, i: (b, jnp.maximum(i - 1, 0), 0)),
                  pl.BlockSpec((1, ROW_T, D), lambda b, i: (b, 0, 0)),
                  pl.BlockSpec((1, 1, D), lambda b, i: (row(b, i), 0, 0)),
                  pl.BlockSpec((1, 1, D), lambda b, i: (row(b, i), 0, 1)),
                  pl.BlockSpec((1, D), lambda b, i: (0, 0))],
        out_specs=pl.BlockSpec((1, ROW_T, D), lambda b, i: (b, i, 0)),
        out_shape=jax.ShapeDtypeStruct((B, NA, D), BF16),
        compiler_params=_cparams(("arbitrary", "arbitrary")),
        name="normmod",
    )(x, ctx, mod3, mod3, ln_w)


def _matmul_kernel(a_ref, w_ref, o_ref):
    o_ref[...] = jnp.dot(a_ref[...], w_ref[...], preferred_element_type=F32).astype(o_ref.dtype)


def _matmul_call(a, w, out_dtype, name):
    rows, k = a.shape
    cols = w.shape[1]
    return pl.pallas_call(
        _matmul_kernel,
        grid=(rows // MM_TM, cols // MM_TN),
        in_specs=[pl.BlockSpec((MM_TM, k), lambda i, j: (i, 0)),
                  pl.BlockSpec((k, MM_TN), lambda i, j: (0, j))],
        out_specs=pl.BlockSpec((MM_TM, MM_TN), lambda i, j: (i, j)),
        out_shape=jax.ShapeDtypeStruct((rows, cols), out_dtype),
        compiler_params=_cparams(("arbitrary", "arbitrary")),
        name=name,
    )(a, w)


def _kvprep_kernel(p_ref, kvn_ref, w_ref, cs_ref, k_ref, v_ref):
    ckv = p_ref[:, Q_RANK:Q_RANK + KV_RANK].astype(F32)
    ckvn = _rms(ckv, kvn_ref[...]).astype(BF16)
    kv = jnp.dot(ckvn, w_ref[...], preferred_element_type=F32)
    a = p_ref[:, Q_RANK + KV_RANK:Q_RANK + KV_RANK + LANES].astype(F32) * cs_ref[...]
    r = a + pltpu.roll(a, ROPE, axis=1)
    lane = lax.broadcasted_iota(I32, r.shape, 1)
    r = jnp.where(lane < ROPE, r, 0.0).astype(BF16)
    for h in range(H):
        k_ref[0, h, :, 0:HW] = kv[:, h * HW:(h + 1) * HW].astype(BF16)
        k_ref[0, h, :, HW:2 * HW] = r
        v_ref[0, h] = kv[:, (H + h) * HW:(H + h + 1) * HW].astype(BF16)


def _kvprep_call(p, kv_norm, w_kv, cs_all):
    return pl.pallas_call(
        _kvprep_kernel,
        grid=(B, TILES_A),
        in_specs=[pl.BlockSpec((ROW_T, 1024), lambda b, i: (b * TILES_A + i, 0)),
                  pl.BlockSpec((1, KV_RANK), lambda b, i: (0, 0)),
                  pl.BlockSpec((KV_RANK, 2 * H * HW), lambda b, i: (0, 0)),
                  pl.BlockSpec((ROW_T, LANES), lambda b, i: (i, 0))],
        out_specs=[pl.BlockSpec((1, H, ROW_T, 2 * HW), lambda b, i: (b, 0, i, 0)),
                   pl.BlockSpec((1, H, ROW_T, HW), lambda b, i: (b, 0, i, 0))],
        out_shape=[jax.ShapeDtypeStruct((B, H, NA, 2 * HW), BF16),
                   jax.ShapeDtypeStruct((B, H, NA, HW), BF16)],
        compiler_params=_cparams(("arbitrary", "arbitrary")),
        name="kvprep",
    )(p, kv_norm, w_kv, cs_all)


def _qprep_kernel(p_ref, qn_ref, w_ref, cs_ref, q_ref):
    cq = p_ref[:, 0:Q_RANK].astype(F32)
    cqn = _rms(cq, qn_ref[...]).astype(BF16)
    q = jnp.dot(cqn, w_ref[...], preferred_element_type=F32)
    cs = cs_ref[...]
    for h in range(H):
        q_ref[0, h, :, 0:HW] = (q[:, 2 * h * HW:(2 * h + 1) * HW] * QK_SCALE).astype(BF16)
        a = q[:, (2 * h + 1) * HW:(2 * h + 2) * HW] * cs
        q_ref[0, h, :, HW:2 * HW] = (a + pltpu.roll(a, ROPE, axis=1)).astype(BF16)


def _qprep_call(p, q_norm, w_q, cs_q):
    return pl.pallas_call(
        _qprep_kernel,
        grid=(B, TILES_L),
        in_specs=[pl.BlockSpec((ROW_T, 1024), lambda b, i: (b * TILES_A + 1 + i, 0)),
                  pl.BlockSpec((1, Q_RANK), lambda b, i: (0, 0)),
                  pl.BlockSpec((Q_RANK, 2 * H * HW), lambda b, i: (0, 0)),
                  pl.BlockSpec((ROW_T, LANES), lambda b, i: (i, 0))],
        out_specs=pl.BlockSpec((1, H, ROW_T, 2 * HW), lambda b, i: (b, 0, i, 0)),
        out_shape=jax.ShapeDtypeStruct((B, H, N, 2 * HW), BF16),
        compiler_params=_cparams(("arbitrary", "arbitrary")),
        name="qprep",
    )(p, q_norm, w_q, cs_q)


ATT_TQ = 256
ATT_TILES = 8


def _attn_kernel(q_ref, k_ref, v_ref, o_ref):
    for t in range(ATT_TILES):
        rows = slice(t * ATT_TQ, (t + 1) * ATT_TQ)
        s = lax.dot_general(q_ref[0, 0, rows, :], k_ref[0, 0], (((1,), (1,)), ((), ())),
                            preferred_element_type=F32)
        m = jnp.max(s, axis=-1, keepdims=True)
        p = jnp.exp(s - m)
        l = jnp.sum(p, axis=-1, keepdims=True)
        o = jnp.dot(p.astype(BF16), v_ref[0, 0], preferred_element_type=F32)
        o_ref[rows, :] = (o * (1.0 / l)).astype(BF16)


def _attn_call(q, k, v):
    tq = ATT_TQ * ATT_TILES
    return pl.pallas_call(
        _attn_kernel,
        grid=(B, H, N // tq),
        in_specs=[pl.BlockSpec((1, 1, tq, 2 * HW), lambda b, h, i: (b, h, i, 0)),
                  pl.BlockSpec((1, 1, NA, 2 * HW), lambda b, h, i: (b, h, 0, 0)),
                  pl.BlockSpec((1, 1, NA, HW), lambda b, h, i: (b, h, 0, 0))],
        out_specs=pl.BlockSpec((tq, HW), lambda b, h, i: (b * (N // tq) + i, h)),
        out_shape=jax.ShapeDtypeStruct((T, H * HW), BF16),
        compiler_params=_cparams(("arbitrary", "arbitrary", "arbitrary")),
        name="attention",
    )(q, k, v)


def _hgrn_tables(reverse):
    c = HG_C
    t = np.arange(c)[:, None]
    u = np.arange(c)[None, :]
    cum = (u <= t).astype(np.float32)
    masks = [(u == t)]
    for m in HG_LEVELS:
        blk_t = t // m
        masks.append(((blk_t % 2) == 1) & ((u // m) == blk_t - 1))
    masks = np.stack(masks).astype(np.float32)
    if reverse:
        cum = cum[::-1, ::-1]
        masks = masks[:, ::-1, ::-1]
    return jnp.asarray(cum, dtype=BF16), jnp.asarray(masks.reshape(-1, c), dtype=F32)


def _pair_boundary(c_ref, cval, m, reverse):
    def rows_of(r_of_group):
        return jnp.concatenate([jnp.broadcast_to(c_ref[r_of_group(gi):r_of_group(gi) + 1, :], (SUBLANES, HW))
                                for gi in range(HG_C // SUBLANES)], axis=0)

    if m >= 4:
        def boundary(gi):
            pair = (SUBLANES * gi // (2 * m)) * 2 * m
            return pair + m if reverse else pair + m - 1
        return rows_of(boundary)
    row = lax.broadcasted_iota(I32, (HG_C, HW), 0)
    if m == 2:
        first = rows_of(lambda gi: SUBLANES * gi + (2 if reverse else 1))
        second = rows_of(lambda gi: SUBLANES * gi + (6 if reverse else 5))
        return jnp.where((row & 7) < 4, first, second)
    if reverse:
        return jnp.where((row & 1) == 0, pltpu.roll(cval, HG_C - 1, 0), cval)
    return jnp.where((row & 1) == 1, pltpu.roll(cval, 1, 0), cval)


def _hgrn_head(h, hq_ref, hi_ref, fr_ref, lb_ref, cum_ref, mk_ref, s_ref, o_ref, c_ref, reverse):
    c = HG_C
    col = pl.ds(pl.multiple_of(h * HW, HW), HW)
    xq = hq_ref[:, col].astype(F32)
    q = xq * _sigmoid(xq)
    lb = lb_ref[:, col]
    f = lb + (1.0 - lb) * _sigmoid(fr_ref[:, col])
    g = jnp.log(f)
    k = 1.0 - f
    v = hi_ref[:, col]
    g_hi = g.astype(BF16)
    g_lo = (g - g_hi.astype(F32)).astype(BF16)
    cum = cum_ref[...]
    cs = jnp.dot(jnp.concatenate([cum, cum], axis=1), jnp.concatenate([g_hi, g_lo], axis=0),
                 preferred_element_type=F32) * LOG2E
    c_ref[...] = cs
    total_row = 0 if reverse else c - 1
    c_tot = c_ref[total_row:total_row + 1, :]

    nt = (((1,), (1,)), ((), ()))
    st = s_ref[h]
    o = lax.dot_general((q * jnp.exp2(cs)).astype(BF16), st.astype(BF16), nt, preferred_element_type=F32)
    qb = q.astype(BF16)
    kb = k.astype(BF16)
    sc = lax.dot_general(qb, kb, nt, preferred_element_type=F32) * mk_ref[0:c, :]
    sign = jnp.uint32(0x80000000)
    for li, m in enumerate(HG_LEVELS):
        z = cs - _pair_boundary(c_ref, cs, m, reverse)
        w = jnp.exp2(pltpu.bitcast(pltpu.bitcast(z, jnp.uint32) | sign, F32)).astype(BF16)
        sc = sc + (lax.dot_general(qb * w, kb * w, nt, preferred_element_type=F32)
                   * mk_ref[(li + 1) * c:(li + 2) * c, :])
    o = o + jnp.dot(sc.astype(BF16), v, preferred_element_type=F32)
    o_ref[:, col] = o.astype(BF16)
    kend = (k * jnp.exp2(c_tot - cs)).astype(BF16)
    s_ref[h] = st * jnp.exp2(c_tot) + lax.dot_general(v, kend, (((0,), (0,)), ((), ())), preferred_element_type=F32)


HG_UNROLL = 8


def _hgrn_kernel(hq_f, hi_f, fr_f, hq_b, hi_b, fr_b, lbf_ref, lbb_ref, cumf_ref, mkf_ref, cumb_ref, mkb_ref,
                 of_ref, ob_ref, sf_ref, sb_ref, c_s):
    @pl.when(pl.program_id(1) == 0)
    def _():
        sf_ref[...] = jnp.zeros_like(sf_ref)
        sb_ref[...] = jnp.zeros_like(sb_ref)

    def body(i, carry):
        for hh in range(HG_UNROLL):
            h = i * HG_UNROLL + hh
            _hgrn_head(h, hq_f, hi_f, fr_f, lbf_ref, cumf_ref, mkf_ref, sf_ref, of_ref, c_s.at[2 * hh], False)
            _hgrn_head(h, hq_b, hi_b, fr_b, lbb_ref, cumb_ref, mkb_ref, sb_ref, ob_ref, c_s.at[2 * hh + 1], True)
        return carry

    lax.fori_loop(0, H // HG_UNROLL, body, 0)


def _hgrn_call(p, fraw, lb_f, lb_b):
    cum_f, mk_f = _hgrn_tables(False)
    cum_b, mk_b = _hgrn_tables(True)
    wide = H * HW

    def cf(b, s):
        return b * HG_STEPS + s

    def cb(b, s):
        return b * HG_STEPS + jnp.where(s < HG_CTX, HG_CTX - 1 - s, HG_STEPS - 1 + HG_CTX - s)

    const = lambda shape: pl.BlockSpec(shape, lambda b, s: (0, 0))
    return pl.pallas_call(
        _hgrn_kernel,
        grid=(B, HG_STEPS),
        in_specs=[pl.BlockSpec((HG_C, wide), lambda b, s: (cf(b, s), 1)),
                  pl.BlockSpec((HG_C, wide), lambda b, s: (cf(b, s), 2)),
                  pl.BlockSpec((HG_C, wide), lambda b, s: (cf(b, s), 0)),
                  pl.BlockSpec((HG_C, wide), lambda b, s: (cb(b, s), 1)),
                  pl.BlockSpec((HG_C, wide), lambda b, s: (cb(b, s), 2)),
                  pl.BlockSpec((HG_C, wide), lambda b, s: (cb(b, s), 1)),
                  const((1, wide)), const((1, wide)),
                  const(cum_f.shape), const(mk_f.shape), const(cum_b.shape), const(mk_b.shape)],
        out_specs=[pl.BlockSpec((HG_C, wide), lambda b, s: (cf(b, s), 0)),
                   pl.BlockSpec((HG_C, wide), lambda b, s: (cb(b, s), 0))],
        out_shape=[jax.ShapeDtypeStruct((TA, wide), BF16), jax.ShapeDtypeStruct((TA, wide), BF16)],
        scratch_shapes=[pltpu.VMEM((H, HW, HW), F32), pltpu.VMEM((H, HW, HW), F32),
                        pltpu.VMEM((2 * HG_UNROLL, HG_C, HW), F32)],
        compiler_params=_cparams(("arbitrary", "arbitrary")),
        name="hgrn",
    )(p, p, fraw, p, p, fraw, lb_f, lb_b, cum_f, mk_f, cum_b, mk_b)


def _pack_rows(val, dst_ref, row0, rows):
    for s in range(PACK_W):
        lo = val[:, 2 * s * LANES:(2 * s + 1) * LANES]
        hi = val[:, (2 * s + 1) * LANES:(2 * s + 2) * LANES]
        w = pltpu.pack_elementwise([lo, hi], packed_dtype=BF16)
        dst_ref[pl.ds(row0 * PACK_W + s, rows, stride=PACK_W), :] = w


def _unpack_word(w):
    lo = pltpu.unpack_elementwise(w, index=0, packed_dtype=BF16, unpacked_dtype=F32)
    hi = pltpu.unpack_elementwise(w, index=1, packed_dtype=BF16, unpacked_dtype=F32)
    return lo, hi


def _mixout_kernel(att_ref, of_ref, ob_ref, hg_ref, gn_ref, wa_ref, wr_ref, x_ref, gt_ref, shf_ref, scf_ref,
                   lnp_ref, lnf_ref, rw_ref, rb_ref,
                   x1_ref, h2w_ref, ids_ref, gates_ref, cnt_ref, carry_ref):
    first = (pl.program_id(0) == 0) & (pl.program_id(1) == 0)

    @pl.when(first)
    def _():
        carry_ref[...] = jnp.zeros_like(carry_ref)

    o = of_ref[...].astype(F32) + ob_ref[...].astype(F32)
    gn = gn_ref[...]
    recs = []
    for h in range(H):
        oh = o[:, h * HW:(h + 1) * HW]
        gate = hg_ref[:, h * HW:(h + 1) * HW].astype(F32)
        recs.append((_rms(oh, gn) * (gate * _sigmoid(gate))).astype(BF16))
    rec = jnp.concatenate(recs, axis=-1)
    mix = (jnp.dot(att_ref[...], wa_ref[...], preferred_element_type=F32)
           + jnp.dot(rec, wr_ref[...], preferred_element_type=F32))
    x1 = x_ref[0] + gt_ref[0] * _rms(mix, lnp_ref[...])
    x1_ref[0] = x1
    h2 = _rms(x1, lnf_ref[...]) * (1.0 + scf_ref[0]) + shf_ref[0]
    _pack_rows(h2, h2w_ref, 0, ROW_T)

    h_hi = h2.astype(BF16)
    h_lo = (h2 - h_hi.astype(F32)).astype(BF16)
    hh = jnp.dot(h_hi, rw_ref[...], preferred_element_type=F32)
    logits = (hh[:, :LANES] + hh[:, LANES:] + jnp.dot(h_lo, rw_ref[:, :LANES], preferred_element_type=F32)
              + rb_ref[...])
    lane = lax.broadcasted_iota(I32, logits.shape, 1)
    neg = jnp.float32(-jnp.inf)
    lg = jnp.where(lane < NE, logits, neg)
    tops, idxs = [], []
    for _ in range(TOPK):
        m = jnp.max(lg, axis=-1, keepdims=True)
        idx = jnp.min(jnp.where(lg == m, lane, LANES), axis=-1, keepdims=True)
        tops.append(m)
        idxs.append(idx)
        lg = jnp.where(lane == idx, neg, lg)
    exps = [jnp.exp(tk - tops[0]) for tk in tops]
    inv = 1.0 / (exps[0] + exps[1] + exps[2] + exps[3])
    onehot = jnp.zeros(logits.shape, F32)
    for idx in idxs:
        onehot = onehot + jnp.where(lane == idx, 1.0, 0.0)
    r = lax.broadcasted_iota(I32, (ROW_T, ROW_T), 0)
    cidx = lax.broadcasted_iota(I32, (ROW_T, ROW_T), 1)
    lower = jnp.where(cidx < r, 1.0, 0.0).astype(BF16)
    prefix = jnp.dot(lower, onehot.astype(BF16), preferred_element_type=F32) + carry_ref[...]
    ids = jnp.zeros(logits.shape, I32)
    gates = jnp.zeros(logits.shape, F32)
    for kk in range(TOPK):
        rank = jnp.sum(jnp.where(lane == idxs[kk], prefix, 0.0), axis=-1, keepdims=True).astype(I32)
        ids = jnp.where(lane == kk, idxs[kk], ids)
        ids = jnp.where(lane == TOPK + kk, rank, ids)
        gates = jnp.where(lane == kk, exps[kk] * inv, gates)
    ids_ref[...] = ids
    gates_ref[...] = gates
    carry = carry_ref[...] + jnp.sum(onehot, axis=0, keepdims=True)
    carry_ref[...] = carry
    cnt_ref[...] = carry


def _mixout_call(att, o_f, o_b, p, g_norm_t, w_a, w_r, x, mod3, ln_post, ln_ffn, r_w, r_b):
    wide = H * HW
    lat = lambda b, i: b * TILES_A + 1 + i
    tok = lambda b, i: b * TILES_L + i
    const = lambda shape: pl.BlockSpec(shape, lambda b, i: tuple(0 for _ in shape))
    return pl.pallas_call(
        _mixout_kernel,
        grid=(B, TILES_L),
        in_specs=[pl.BlockSpec((ROW_T, wide), lambda b, i: (tok(b, i), 0)),
                  pl.BlockSpec((ROW_T, wide), lambda b, i: (lat(b, i), 0)),
                  pl.BlockSpec((ROW_T, wide), lambda b, i: (lat(b, i), 0)),
                  pl.BlockSpec((ROW_T, wide), lambda b, i: (lat(b, i), 3)),
                  const((1, HW)), const((wide, D)), const((wide, D)),
                  pl.BlockSpec((1, ROW_T, D), lambda b, i: (b, i, 0)),
                  pl.BlockSpec((1, 1, D), lambda b, i: (b, 0, 2)),
                  pl.BlockSpec((1, 1, D), lambda b, i: (b, 0, 3)),
                  pl.BlockSpec((1, 1, D), lambda b, i: (b, 0, 4)),
                  const((1, D)), const((1, D)), const((D, 2 * LANES)), const((1, LANES))],
        out_specs=[pl.BlockSpec((1, ROW_T, D), lambda b, i: (b, i, 0)),
                   pl.BlockSpec((ROW_T * PACK_W, LANES), lambda b, i: (tok(b, i), 0)),
                   pl.BlockSpec((ROW_T, LANES), lambda b, i: (tok(b, i), 0)),
                   pl.BlockSpec((ROW_T, LANES), lambda b, i: (tok(b, i), 0)),
                   const((1, LANES))],
        out_shape=[jax.ShapeDtypeStruct((B, N, D), F32),
                   jax.ShapeDtypeStruct((T * PACK_W, LANES), I32),
                   jax.ShapeDtypeStruct((T, LANES), I32),
                   jax.ShapeDtypeStruct((T, LANES), F32),
                   jax.ShapeDtypeStruct((1, LANES), F32)],
        scratch_shapes=[pltpu.VMEM((1, LANES), F32)],
        compiler_params=_cparams(("arbitrary", "arbitrary")),
        name="mixout_router",
    )(att, o_f, o_b, p, g_norm_t, w_a, w_r, x, mod3, mod3, mod3, ln_post, ln_ffn, r_w, r_b)


def _dispatch_copy(src_vmem, dst_hbm, sem, src_row, dst_row, rows):
    return pltpu.make_async_copy(src_vmem.at[pl.ds(src_row * PACK_W, rows * PACK_W)],
                                 dst_hbm.at[pl.ds(dst_row * PACK_W, rows * PACK_W)], sem)


def _zero_copy(zeros_s, dst_hbm, sem, dst_row, rows):
    return pltpu.make_async_copy(zeros_s.at[pl.ds(0, rows * PACK_W)],
                                 dst_hbm.at[pl.ds(dst_row * PACK_W, rows * PACK_W)], sem)


def _dispatch_kernel(dest_ref, pad_ref, h2w_ref, xs_hbm, zeros_s, sem, zsem):
    step = pl.program_id(0)
    base = step * DISP_T

    def issue(t, carry):
        for kk in range(TOPK):
            _dispatch_copy(h2w_ref, xs_hbm, sem, t, dest_ref[(base + t) * TOPK + kk], 1).start()
        return carry

    lax.fori_loop(0, DISP_T, issue, 0)

    @pl.when(step == 0)
    def _():
        zeros_s[...] = jnp.zeros_like(zeros_s)

        def expert_pad(e, carry):
            row0 = pad_ref[e]
            n = pad_ref[NE + e]

            def start_one(r, c):
                _zero_copy(zeros_s, xs_hbm, zsem, row0 + r, 1).start()
                return c

            def wait_one(r, c):
                _zero_copy(zeros_s, xs_hbm, zsem, row0 + r, 1).wait()
                return c

            lax.fori_loop(0, n, start_one, 0)
            lax.fori_loop(0, n, wait_one, 0)
            return carry

        lax.fori_loop(0, NE, expert_pad, 0)
        tail0 = pad_ref[2 * NE]
        tail_n = pad_ref[2 * NE + 1]

        def start_tail(i, c):
            _zero_copy(zeros_s, xs_hbm, zsem, (tail0 + i) * SUB, SUB).start()
            return c

        def wait_tail(i, c):
            _zero_copy(zeros_s, xs_hbm, zsem, (tail0 + i) * SUB, SUB).wait()
            return c

        lax.fori_loop(0, tail_n, start_tail, 0)
        lax.fori_loop(0, tail_n, wait_tail, 0)

    for _ in range(TOPK):
        _dispatch_copy(h2w_ref, xs_hbm, sem, 0, 0, DISP_T).wait()


def _dispatch_call(dest, pad, h2w):
    gs = pltpu.PrefetchScalarGridSpec(
        num_scalar_prefetch=2, grid=(T // DISP_T,),
        in_specs=[pl.BlockSpec((DISP_T * PACK_W, LANES), lambda i, d, z: (i, 0))],
        out_specs=pl.BlockSpec(memory_space=pl.ANY),
        scratch_shapes=[pltpu.VMEM((SUB * PACK_W, LANES), I32),
                        pltpu.SemaphoreType.DMA(()),
                        pltpu.SemaphoreType.DMA(())])
    return pl.pallas_call(
        _dispatch_kernel, grid_spec=gs,
        out_shape=jax.ShapeDtypeStruct((N_SUBS * SUB * PACK_W, LANES), I32),
        compiler_params=_cparams(("arbitrary",)),
        name="moe_dispatch",
    )(dest, pad, h2w)


def _sub_in_copy(xs_hbm, stage, sems, blk, slot):
    return pltpu.make_async_copy(xs_hbm.at[pl.ds(blk * (SUB * PACK_W), SUB * PACK_W)], stage.at[slot], sems.at[slot])


def _sub_out_copy(stage, out_hbm, sems, blk, slot):
    return pltpu.make_async_copy(stage.at[slot], out_hbm.at[pl.ds(blk * (SUB * PACK_W), SUB * PACK_W)], sems.at[slot])


def _for_row_chunks(nblk, fn):
    pair = 2 * CHUNK_SUBS

    @pl.when(nblk >= pair)
    def _():
        fn(0, CHUNK_SUBS * SUB)
        fn(CHUNK_SUBS * SUB, CHUNK_SUBS * SUB)

    for c in range(-(-SB_SUBS // CHUNK_SUBS)):
        row0 = c * CHUNK_SUBS * SUB
        most = min(CHUNK_SUBS, SB_SUBS - c * CHUNK_SUBS)
        for r in range(1, most + 1):
            full = r == CHUNK_SUBS
            cond = (nblk >= c * CHUNK_SUBS + r) if full else (nblk == c * CHUNK_SUBS + r)
            if c < 2:
                cond = cond & (nblk < pair)
            pl.when(cond)(functools.partial(fn, row0, r * SUB))


def _wtile_copy(w_hbm, buf, sems, e, col_tile, slot, which):
    return pltpu.make_async_copy(w_hbm.at[e, :, pl.ds(pl.multiple_of(col_tile * FF_T, FF_T), FF_T)],
                                 buf.at[slot, which], sems.at[slot, which])


def _expert_kernel(sbe_ref, sbn_ref, sbb_ref, xs_hbm, wgu_hbm, bgu_ref, wdn_hbm, bdn_ref,
                   out_hbm, x_s, act_s, wbuf, stage_in, stage_out,
                   sem_w, sem_in, sem_out):
    s = pl.program_id(0)
    nblk = sbn_ref[s]
    blk0 = sbb_ref[s]
    e = sbe_ref[s]

    def gate_tiles(j, slot):
        return (_wtile_copy(wgu_hbm, wbuf, sem_w, e, j, slot, 0),
                _wtile_copy(wgu_hbm, wbuf, sem_w, e, N1 + j, slot, 1))

    def down_tile(j, slot):
        return _wtile_copy(wdn_hbm, wbuf, sem_w, e, j, slot, 0)

    @pl.when(nblk > 0)
    def _():
        for cp in gate_tiles(0, 0):
            cp.start()

        _sub_in_copy(xs_hbm, stage_in, sem_in, blk0, 0).start()
        for i in range(SB_SUBS):
            @pl.when(i < nblk)
            def _():
                if i + 1 < SB_SUBS:
                    @pl.when(i + 1 < nblk)
                    def _():
                        _sub_in_copy(xs_hbm, stage_in, sem_in, blk0 + i + 1, (i + 1) % 2).start()
                _sub_in_copy(xs_hbm, stage_in, sem_in, blk0 + i, i % 2).wait()
                for w in range(PACK_W):
                    lo, hi = _unpack_word(stage_in[i % 2, pl.ds(w, SUB, stride=PACK_W), :])
                    x_s[i * SUB:(i + 1) * SUB, 2 * w * LANES:(2 * w + 1) * LANES] = lo.astype(BF16)
                    x_s[i * SUB:(i + 1) * SUB, (2 * w + 1) * LANES:(2 * w + 2) * LANES] = hi.astype(BF16)

        def hidden_step(j, carry):
            slot = j % 2
            for cp in gate_tiles(j, slot):
                cp.wait()

            @pl.when(j + 1 < N1)
            def _():
                for cp in gate_tiles(j + 1, 1 - slot):
                    cp.start()

            @pl.when(j + 1 == N1)
            def _():
                down_tile(0, 1 - slot).start()

            col = pl.ds(pl.multiple_of(j * FF_T, FF_T), FF_T)
            bg = bgu_ref[0, :, col]
            bl = bgu_ref[0, :, pl.ds(pl.multiple_of(DFF + j * FF_T, FF_T), FF_T)]

            def gate_up(row0, rows):
                x = x_s[row0:row0 + rows, :]
                gu = jnp.dot(x, wbuf[slot, 0].astype(BF16), preferred_element_type=F32) + bg
                li = jnp.dot(x, wbuf[slot, 1].astype(BF16), preferred_element_type=F32) + bl
                glu = jnp.minimum(gu, LIMIT)
                lin = jnp.clip(li, -LIMIT, LIMIT)
                act = glu * _sigmoid(ALPHA * glu) * (lin + 1.0)
                act_s[row0:row0 + rows, col] = act.astype(BF16)

            _for_row_chunks(nblk, gate_up)
            return carry

        lax.fori_loop(0, N1, hidden_step, 0)

        def out_step(j2, carry):
            slot = (N1 + j2) % 2
            down_tile(j2, slot).wait()

            @pl.when(j2 + 1 < N2)
            def _():
                down_tile(j2 + 1, 1 - slot).start()

            bd = bdn_ref[0, :, pl.ds(pl.multiple_of(j2 * FF_T, FF_T), FF_T)]

            def down(row0, rows):
                acc = jnp.dot(act_s[row0:row0 + rows, :], wbuf[slot, 0].astype(BF16), preferred_element_type=F32) + bd
                x_s[row0:row0 + rows, pl.ds(pl.multiple_of(j2 * FF_T, FF_T), FF_T)] = acc.astype(BF16)

            _for_row_chunks(nblk, down)
            return carry

        lax.fori_loop(0, N2, out_step, 0)

        for i in range(SB_SUBS):
            @pl.when(i < nblk)
            def _():
                if i >= 2:
                    _sub_out_copy(stage_out, out_hbm, sem_out, blk0 + i - 2, i % 2).wait()
                _pack_rows(x_s[i * SUB:(i + 1) * SUB, :].astype(F32), stage_out.at[i % 2], 0, SUB)
                _sub_out_copy(stage_out, out_hbm, sem_out, blk0 + i, i % 2).start()
        for i in range(SB_SUBS):
            @pl.when((i < nblk) & (i + 2 >= nblk))
            def _():
                _sub_out_copy(stage_out, out_hbm, sem_out, blk0 + i, i % 2).wait()


def _expert_call(sb_e, sb_n, sb_b, xs, w_gu, b_gu, w_dn, b_dn):
    gs = pltpu.PrefetchScalarGridSpec(
        num_scalar_prefetch=3, grid=(S_MAX,),
        in_specs=[pl.BlockSpec(memory_space=pl.ANY),
                  pl.BlockSpec(memory_space=pl.ANY),
                  pl.BlockSpec((1, 1, 2 * DFF), lambda s, e, n, b: (e[s], 0, 0)),
                  pl.BlockSpec(memory_space=pl.ANY),
                  pl.BlockSpec((1, 1, D), lambda s, e, n, b: (e[s], 0, 0))],
        out_specs=pl.BlockSpec(memory_space=pl.ANY),
        scratch_shapes=[pltpu.VMEM((SB_R, D), BF16),
                        pltpu.VMEM((SB_R, DFF), BF16),
                        pltpu.VMEM((2, 2, D, FF_T), F32),
                        pltpu.VMEM((2, SUB * PACK_W, LANES), I32),
                        pltpu.VMEM((2, SUB * PACK_W, LANES), I32),
                        pltpu.SemaphoreType.DMA((2, 2)),
                        pltpu.SemaphoreType.DMA((2,)),
                        pltpu.SemaphoreType.DMA((2,))])
    return pl.pallas_call(
        _expert_kernel, grid_spec=gs,
        out_shape=jax.ShapeDtypeStruct(xs.shape, xs.dtype),
        input_output_aliases={3: 0},
        compiler_params=_cparams(("arbitrary",)),
        name="moe_experts",
    )(sb_e, sb_n, sb_b, xs, w_gu, b_gu, w_dn, b_dn)


def _combine_copy(src_hbm, buf, sem, src_row, dst_row, rows):
    return pltpu.make_async_copy(src_hbm.at[pl.ds(src_row * PACK_W, rows * PACK_W)],
                                 buf.at[pl.ds(dst_row * PACK_W, rows * PACK_W)], sem)


def _combine_kernel(dest_ref, ys_hbm, gates_ref, x1_ref, gt_ref, ln_ref, o_ref, bufs, y_s, sems):
    step = pl.program_id(0) * TILES_L + pl.program_id(1)

    def issue(tile, slot):
        def body(tg, carry):
            for tt in range(COMBINE_UNROLL):
                t = tg * COMBINE_UNROLL + tt
                for kk in range(TOPK):
                    _combine_copy(ys_hbm, bufs.at[slot], sems.at[slot], dest_ref[(tile * ROW_T + t) * TOPK + kk],
                                  kk * ROW_T + t, 1).start()
            return carry
        lax.fori_loop(0, ROW_T // COMBINE_UNROLL, body, 0)

    @pl.when(step == 0)
    def _():
        issue(0, 0)

    @pl.when(step + 1 < B * TILES_L)
    def _():
        issue(step + 1, (step + 1) % 2)

    slot = step % 2
    buf = bufs.at[slot]
    _combine_copy(ys_hbm, buf, sems.at[slot], 0, 0, ROW_T * TOPK).wait()

    gates = gates_ref[...]
    gk = [jnp.broadcast_to(gates[:, kk:kk + 1], (ROW_T, LANES)) for kk in range(TOPK)]
    for w in range(PACK_W):
        acc_lo = jnp.zeros((ROW_T, LANES), F32)
        acc_hi = jnp.zeros((ROW_T, LANES), F32)
        for kk in range(TOPK):
            lo, hi = _unpack_word(buf[pl.ds(kk * ROW_T * PACK_W + w, ROW_T, stride=PACK_W), :])
            acc_lo = acc_lo + gk[kk] * lo
            acc_hi = acc_hi + gk[kk] * hi
        y_s[:, 2 * w * LANES:(2 * w + 1) * LANES] = acc_lo
        y_s[:, (2 * w + 1) * LANES:(2 * w + 2) * LANES] = acc_hi
    o_ref[0] = x1_ref[0] + gt_ref[0] * _rms(y_s[...], ln_ref[...])


def _combine_call(dest, ys, gates, x1, mod3, ln_post):
    tok = lambda b, i: b * TILES_L + i
    gs = pltpu.PrefetchScalarGridSpec(
        num_scalar_prefetch=1, grid=(B, TILES_L),
        in_specs=[pl.BlockSpec(memory_space=pl.ANY),
                  pl.BlockSpec((ROW_T, LANES), lambda b, i, d: (tok(b, i), 0)),
                  pl.BlockSpec((1, ROW_T, D), lambda b, i, d: (b, i, 0)),
                  pl.BlockSpec((1, 1, D), lambda b, i, d: (b, 0, 5)),
                  pl.BlockSpec((1, D), lambda b, i, d: (0, 0))],
        out_specs=pl.BlockSpec((1, ROW_T, D), lambda b, i, d: (b, i, 0)),
        scratch_shapes=[pltpu.VMEM((2, TOPK * ROW_T * PACK_W, LANES), I32),
                        pltpu.VMEM((ROW_T, D), F32),
                        pltpu.SemaphoreType.DMA((2,))])
    return pl.pallas_call(
        _combine_kernel, grid_spec=gs,
        out_shape=jax.ShapeDtypeStruct((B, N, D), F32),
        compiler_params=_cparams(("arbitrary", "arbitrary")),
        name="moe_combine",
    )(dest, ys, gates, x1, mod3, ln_post)


def _rot_cols(w):
    half = ROPE // 2
    return jnp.concatenate([-w[..., half:], w[..., :half]], axis=-1)


def _rope_tables():
    t = np.arange(N)
    r = (t // GRID_W).astype(np.float64)
    cl = (t % GRID_W).astype(np.float64)
    n_freq = ROPE // 4
    inv = ROPE_THETA ** (-np.arange(n_freq, dtype=np.float64) / n_freq)
    ang = np.concatenate([r[:, None] * inv, cl[:, None] * inv], axis=-1)
    cos, sin = np.cos(ang), np.sin(ang)
    cs = np.concatenate([cos, cos, sin, sin], axis=-1).astype(np.float32)
    ctx = np.concatenate([np.ones((M, ROPE), np.float32), np.zeros((M, ROPE), np.float32)], axis=-1)
    return jnp.asarray(np.concatenate([ctx, cs], axis=0)), jnp.asarray(cs * QK_SCALE)


def _routing_tables(ids, counts_f):
    e = ids[:, :TOPK]
    rank = ids[:, TOPK:2 * TOPK]
    counts = counts_f[0, :NE].astype(I32)
    nsub = (counts + SUB - 1) // SUB
    sub_start = jnp.cumsum(nsub) - nsub
    onehot = e[..., None] == jnp.arange(NE, dtype=I32)
    dest = jnp.sum(jnp.where(onehot, sub_start * SUB, 0), axis=-1) + rank
    nsb = (nsub + SB_SUBS - 1) // SB_SUBS
    sb_end = jnp.cumsum(nsb)
    sb_start = sb_end - nsb
    n_active = sb_end[-1]
    s_idx = jnp.arange(S_MAX, dtype=I32)
    s_eff = jnp.minimum(s_idx, n_active - 1)
    sb_e = jnp.minimum(jnp.searchsorted(sb_end, s_eff, side="right"), NE - 1).astype(I32)
    local = s_eff - sb_start[sb_e]
    sb_b = (sub_start[sb_e] + local * SB_SUBS).astype(I32)
    sb_n = jnp.where(s_idx < n_active, jnp.clip(nsub[sb_e] - local * SB_SUBS, 0, SB_SUBS), 0).astype(I32)
    pad = jnp.concatenate([sub_start * SUB + counts, nsub * SUB - counts,
                           jnp.stack([jnp.sum(nsub), N_SUBS - jnp.sum(nsub)])]).astype(I32)
    return dest.reshape(-1).astype(I32), pad, sb_e, sb_n, sb_b


def kernel(x, c, ctx, c_ctx, w_mod, b_mod, ln_mix_pre, ln_mix_post, ln_ffn_pre, ln_ffn_post, w_in, q_norm, kv_norm,
           w_uq, w_ukv, hgrn_lb, g_norm, w_out, router_w, router_b, w_gate_up, b_gate_up, w_down, b_down):
    c_all = jnp.concatenate([c, c_ctx[None, :], jnp.zeros((8 - B - 1, D), F32)], axis=0)
    mod = _mod_call(c_all, w_mod[0], b_mod)
    mod3 = mod.reshape(8, 1, 6 * D)

    w = w_in[0]
    o_q, o_kv, o_kr = 0, Q_RANK, Q_RANK + KV_RANK
    o_hq = o_kr + ROPE
    wide = H * HW
    o_ff, o_fb, o_hi, o_hg = o_hq + wide, o_hq + 2 * wide, o_hq + 3 * wide, o_hq + 4 * wide
    w_kr = w[:, o_kr:o_kr + ROPE]
    w_p = jnp.concatenate([w[:, o_q:o_kr], w_kr, _rot_cols(w_kr), jnp.zeros((D, LANES), F32),
                           w[:, o_hq:o_hq + wide], w[:, o_hi:o_hi + wide], w[:, o_hg:o_hg + wide]],
                          axis=1).astype(BF16)
    w_f = w[:, o_ff:o_ff + 2 * wide].astype(BF16)
    wq = w_uq[0]
    wq_r = wq[..., HW:]
    w_q = jnp.concatenate([wq[..., :HW], wq_r, _rot_cols(wq_r)], axis=-1).reshape(Q_RANK, 2 * wide).astype(BF16)
    wkv = w_ukv[0]
    w_kv = jnp.concatenate([wkv[..., :HW].reshape(KV_RANK, wide), wkv[..., HW:].reshape(KV_RANK, wide)],
                           axis=1).astype(BF16)
    w_o = w_out[0].astype(BF16)
    cs_all, cs_q = _rope_tables()
    lb = jnp.cumsum(jax.nn.softmax(hgrn_lb.astype(F32), axis=1), axis=1)
    r_w = jnp.concatenate([router_w[0], jnp.zeros((D, LANES - NE), F32)], axis=1)
    r_w_hi = r_w.astype(BF16)
    r_w = jnp.concatenate([r_w_hi, (r_w - r_w_hi.astype(F32)).astype(BF16)], axis=1)
    r_b = jnp.concatenate([router_b[0], jnp.zeros((LANES - NE,), F32)])[None, :]

    h_all = _normmod_call(x, ctx, mod3, ln_mix_pre)
    h_flat = h_all.reshape(TA, D)
    p = _matmul_call(h_flat, w_p, BF16, "in_proj")
    fraw = _matmul_call(h_flat, w_f, F32, "in_proj_forget")
    k_all, v_all = _kvprep_call(p, kv_norm, w_kv, cs_all)
    q_all = _qprep_call(p, q_norm, w_q, cs_q)
    att = _attn_call(q_all, k_all, v_all)
    o_f, o_b = _hgrn_call(p, fraw, lb[0, 0][None, :], lb[1, 0][None, :])

    g_norm_t = g_norm
    x1, h2w, ids, gates, counts = _mixout_call(att, o_f, o_b, p, g_norm_t, w_o[:wide], w_o[wide:], x, mod3,
                                               ln_mix_post, ln_ffn_pre, r_w, r_b)

    dest, pad, sb_e, sb_n, sb_b = _routing_tables(ids, counts)
    xs = _dispatch_call(dest, pad, h2w)
    ys = _expert_call(sb_e, sb_n, sb_b, xs, w_gate_up[0], b_gate_up[0][:, None, :], w_down[0], b_down[0][:, None, :])
    return _combine_call(dest, ys, gates, x1, mod3, ln_ffn_post)
```

```python
import functools

import numpy as np
import jax
import jax.numpy as jnp
from jax import lax
from jax.experimental import pallas as pl
from jax.experimental.pallas import tpu as pltpu

F32 = jnp.float32
BF16 = jnp.bfloat16
I32 = jnp.int32

D = 2048
B = 4
N = 4096
M = 256
NA = N + M
T = B * N
TA = B * NA
EPS = 1e-6
H = 8
HW = 128
Q_RANK = 512
KV_RANK = 256
ROPE = 64
QK_SCALE = float((128 + ROPE) ** -0.5)
GRID_W = 64
ROPE_THETA = 10000.0
NE = 32
TOPK = 4
DFF = 2048
LIMIT = 7.0
ALPHA = 1.702
LOG2E = float(np.log2(np.e))

LANES = 128
SUBLANES = 8
VMEM_LIMIT = 56 * 1024 * 1024

ROW_T = 256
TILES_A = NA // ROW_T
TILES_L = N // ROW_T
MM_TM = 1024
MM_TN = 1024
HG_C = 256
HG_STEPS = NA // HG_C
HG_CTX = M // HG_C
HG_LEVELS = tuple(HG_C >> i for i in range(1, HG_C.bit_length()))
SUB = 256
N_SUBS = T * TOPK // SUB + NE
SB_SUBS = 14
SB_R = SB_SUBS * SUB
CHUNK_SUBS = 4
S_MAX = -(-N_SUBS // SB_SUBS) + NE
FF_T = 256
N1 = DFF // FF_T
N2 = D // FF_T
PACK_W = D // 2 // LANES
DISP_T = 512
COMBINE_UNROLL = 4


def _sigmoid(x):
    return 1.0 / (1.0 + jnp.exp(-x))


def _rms(x, w):
    ms = jnp.mean(x * x, axis=-1, keepdims=True)
    return x * lax.rsqrt(ms + EPS) * w


def _cparams(sem, vmem=VMEM_LIMIT):
    return pltpu.CompilerParams(dimension_semantics=sem, vmem_limit_bytes=vmem)


def _mod_kernel(c_ref, w_ref, b_ref, o_ref):
    c = c_ref[...]
    a = (c * _sigmoid(c)).astype(BF16)
    o_ref[...] = jnp.dot(a, w_ref[...].astype(BF16), preferred_element_type=F32) + b_ref[...]


def _mod_call(c_all, w_mod, b_mod):
    tn = 1024
    return pl.pallas_call(
        _mod_kernel,
        grid=(6 * D // tn,),
        in_specs=[pl.BlockSpec((8, D), lambda j: (0, 0)),
                  pl.BlockSpec((D, tn), lambda j: (0, j)),
                  pl.BlockSpec((1, tn), lambda j: (0, j))],
        out_specs=pl.BlockSpec((8, tn), lambda j: (0, j)),
        out_shape=jax.ShapeDtypeStruct((8, 6 * D), F32),
        compiler_params=_cparams(("arbitrary",)),
        name="mod",
    )(c_all, w_mod, b_mod)


def _normmod_kernel(x_ref, ctx_ref, sh_ref, sc_ref, w_ref, o_ref):
    def go(src):
        y = _rms(src, w_ref[...])
        o_ref[0] = (y * (1.0 + sc_ref[0]) + sh_ref[0]).astype(BF16)

    i = pl.program_id(1)

    @pl.when(i == 0)
    def _():
        go(ctx_ref[0])

    @pl.when(i > 0)
    def _():
        go(x_ref[0])


def _normmod_call(x, ctx, mod3, ln_w):
    row = lambda b, i: jnp.where(i == 0, B, b)
    return pl.pallas_call(
        _normmod_kernel,
        grid=(B, TILES_A),
        in_specs=[pl.BlockSpec((1, ROW_T, D), lambda b, i: (b, jnp.maximum(i - 1, 0), 0)),
                  pl.BlockSpec((1, ROW_T, D), lambda b, i: (b, 0, 0)),
                  pl.BlockSpec((1, 1, D), lambda b, i: (row(b, i), 0, 0)),
                  pl.BlockSpec((1, 1, D), lambda b, i: (row(b, i), 0, 1)),
                  pl.BlockSpec((1, D), lambda b, i: (0, 0))],
        out_specs=pl.BlockSpec((1, ROW_T, D), lambda b, i: (b, i, 0)),
        out_shape=jax.ShapeDtypeStruct((B, NA, D), BF16),
        compiler_params=_cparams(("arbitrary", "arbitrary")),
        name="normmod",
    )(x, ctx, mod3, mod3, ln_w)


def _matmul_kernel(a_ref, w_ref, o_ref):
    o_ref[...] = jnp.dot(a_ref[...], w_ref[...], preferred_element_type=F32).astype(o_ref.dtype)


def _matmul_call(a, w, out_dtype, name):
    rows, k = a.shape
    cols = w.shape[1]
    return pl.pallas_call(
        _matmul_kernel,
        grid=(rows // MM_TM, cols // MM_TN),
        in_specs=[pl.BlockSpec((MM_TM, k), lambda i, j: (i, 0)),
                  pl.BlockSpec((k, MM_TN), lambda i, j: (0, j))],
        out_specs=pl.BlockSpec((MM_TM, MM_TN), lambda i, j: (i, j)),
        out_shape=jax.ShapeDtypeStruct((rows, cols), out_dtype),
        compiler_params=_cparams(("arbitrary", "arbitrary")),
        name=name,
    )(a, w)


def _kvprep_kernel(p_ref, kvn_ref, w_ref, cs_ref, k_ref, v_ref):
    ckv = p_ref[:, Q_RANK:Q_RANK + KV_RANK].astype(F32)
    ckvn = _rms(ckv, kvn_ref[...]).astype(BF16)
    kv = jnp.dot(ckvn, w_ref[...], preferred_element_type=F32)
    a = p_ref[:, Q_RANK + KV_RANK:Q_RANK + KV_RANK + LANES].astype(F32) * cs_ref[...]
    r = a + pltpu.roll(a, ROPE, axis=1)
    lane = lax.broadcasted_iota(I32, r.shape, 1)
    r = jnp.where(lane < ROPE, r, 0.0).astype(BF16)
    for h in range(H):
        k_ref[0, h, :, 0:HW] = kv[:, h * HW:(h + 1) * HW].astype(BF16)
        k_ref[0, h, :, HW:2 * HW] = r
        v_ref[0, h] = kv[:, (H + h) * HW:(H + h + 1) * HW].astype(BF16)


def _kvprep_call(p, kv_norm, w_kv, cs_all):
    return pl.pallas_call(
        _kvprep_kernel,
        grid=(B, TILES_A),
        in_specs=[pl.BlockSpec((ROW_T, 1024), lambda b, i: (b * TILES_A + i, 0)),
                  pl.BlockSpec((1, KV_RANK), lambda b, i: (0, 0)),
                  pl.BlockSpec((KV_RANK, 2 * H * HW), lambda b, i: (0, 0)),
                  pl.BlockSpec((ROW_T, LANES), lambda b, i: (i, 0))],
        out_specs=[pl.BlockSpec((1, H, ROW_T, 2 * HW), lambda b, i: (b, 0, i, 0)),
                   pl.BlockSpec((1, H, ROW_T, HW), lambda b, i: (b, 0, i, 0))],
        out_shape=[jax.ShapeDtypeStruct((B, H, NA, 2 * HW), BF16),
                   jax.ShapeDtypeStruct((B, H, NA, HW), BF16)],
        compiler_params=_cparams(("arbitrary", "arbitrary")),
        name="kvprep",
    )(p, kv_norm, w_kv, cs_all)


def _qprep_kernel(p_ref, qn_ref, w_ref, cs_ref, q_ref):
    cq = p_ref[:, 0:Q_RANK].astype(F32)
    cqn = _rms(cq, qn_ref[...]).astype(BF16)
    q = jnp.dot(cqn, w_ref[...], preferred_element_type=F32)
    cs = cs_ref[...]
    for h in range(H):
        q_ref[0, h, :, 0:HW] = (q[:, 2 * h * HW:(2 * h + 1) * HW] * QK_SCALE).astype(BF16)
        a = q[:, (2 * h + 1) * HW:(2 * h + 2) * HW] * cs
        q_ref[0, h, :, HW:2 * HW] = (a + pltpu.roll(a, ROPE, axis=1)).astype(BF16)


def _qprep_call(p, q_norm, w_q, cs_q):
    return pl.pallas_call(
        _qprep_kernel,
        grid=(B, TILES_L),
        in_specs=[pl.BlockSpec((ROW_T, 1024), lambda b, i: (b * TILES_A + 1 + i, 0)),
                  pl.BlockSpec((1, Q_RANK), lambda b, i: (0, 0)),
                  pl.BlockSpec((Q_RANK, 2 * H * HW), lambda b, i: (0, 0)),
                  pl.BlockSpec((ROW_T, LANES), lambda b, i: (i, 0))],
        out_specs=pl.BlockSpec((1, H, ROW_T, 2 * HW), lambda b, i: (b, 0, i, 0)),
        out_shape=jax.ShapeDtypeStruct((B, H, N, 2 * HW), BF16),
        compiler_params=_cparams(("arbitrary", "arbitrary")),
        name="qprep",
    )(p, q_norm, w_q, cs_q)


ATT_TQ = 256
ATT_TILES = 8


def _attn_kernel(q_ref, k_ref, v_ref, o_ref):
    for t in range(ATT_TILES):
        rows = slice(t * ATT_TQ, (t + 1) * ATT_TQ)
        s = lax.dot_general(q_ref[0, 0, rows, :], k_ref[0, 0], (((1,), (1,)), ((), ())),
                            preferred_element_type=F32)
        m = jnp.max(s, axis=-1, keepdims=True)
        p = jnp.exp(s - m)
        l = jnp.sum(p, axis=-1, keepdims=True)
        o = jnp.dot(p.astype(BF16), v_ref[0, 0], preferred_element_type=F32)
        o_ref[rows, :] = (o * (1.0 / l)).astype(BF16)


def _attn_call(q, k, v):
    tq = ATT_TQ * ATT_TILES
    return pl.pallas_call(
        _attn_kernel,
        grid=(B, H, N // tq),
        in_specs=[pl.BlockSpec((1, 1, tq, 2 * HW), lambda b, h, i: (b, h, i, 0)),
                  pl.BlockSpec((1, 1, NA, 2 * HW), lambda b, h, i: (b, h, 0, 0)),
                  pl.BlockSpec((1, 1, NA, HW), lambda b, h, i: (b, h, 0, 0))],
        out_specs=pl.BlockSpec((tq, HW), lambda b, h, i: (b * (N // tq) + i, h)),
        out_shape=jax.ShapeDtypeStruct((T, H * HW), BF16),
        compiler_params=_cparams(("arbitrary", "arbitrary", "arbitrary")),
        name="attention",
    )(q, k, v)


def _hgrn_tables(reverse):
    c = HG_C
    t = np.arange(c)[:, None]
    u = np.arange(c)[None, :]
    cum = (u <= t).astype(np.float32)
    masks = [(u == t)]
    for m in HG_LEVELS:
        blk_t = t // m
        masks.append(((blk_t % 2) == 1) & ((u // m) == blk_t - 1))
    masks = np.stack(masks).astype(np.float32)
    if reverse:
        cum = cum[::-1, ::-1]
        masks = masks[:, ::-1, ::-1]
    return jnp.asarray(cum, dtype=BF16), jnp.asarray(masks.reshape(-1, c), dtype=F32)


def _pair_boundary(c_ref, cval, m, reverse):
    def rows_of(r_of_group):
        return jnp.concatenate([jnp.broadcast_to(c_ref[r_of_group(gi):r_of_group(gi) + 1, :], (SUBLANES, HW))
                                for gi in range(HG_C // SUBLANES)], axis=0)

    if m >= 4:
        def boundary(gi):
            pair = (SUBLANES * gi // (2 * m)) * 2 * m
            return pair + m if reverse else pair + m - 1
        return rows_of(boundary)
    row = lax.broadcasted_iota(I32, (HG_C, HW), 0)
    if m == 2:
        first = rows_of(lambda gi: SUBLANES * gi + (2 if reverse else 1))
        second = rows_of(lambda gi: SUBLANES * gi + (6 if reverse else 5))
        return jnp.where((row & 7) < 4, first, second)
    if reverse:
        return jnp.where((row & 1) == 0, pltpu.roll(cval, HG_C - 1, 0), cval)
    return jnp.where((row & 1) == 1, pltpu.roll(cval, 1, 0), cval)


def _hgrn_head(h, hq_ref, hi_ref, fr_ref, lb_ref, cum_ref, mk_ref, s_ref, o_ref, c_ref, reverse):
    c = HG_C
    col = pl.ds(pl.multiple_of(h * HW, HW), HW)
    xq = hq_ref[:, col].astype(F32)
    q = xq * _sigmoid(xq)
    lb = lb_ref[:, col]
    f = lb + (1.0 - lb) * _sigmoid(fr_ref[:, col])
    g = jnp.log(f)
    k = 1.0 - f
    v = hi_ref[:, col]
    g_hi = g.astype(BF16)
    g_lo = (g - g_hi.astype(F32)).astype(BF16)
    cum = cum_ref[...]
    cs = jnp.dot(jnp.concatenate([cum, cum], axis=1), jnp.concatenate([g_hi, g_lo], axis=0),
                 preferred_element_type=F32) * LOG2E
    c_ref[...] = cs
    total_row = 0 if reverse else c - 1
    c_tot = c_ref[total_row:total_row + 1, :]

    nt = (((1,), (1,)), ((), ()))
    st = s_ref[h]
    o = lax.dot_general((q * jnp.exp2(cs)).astype(BF16), st.astype(BF16), nt, preferred_element_type=F32)
    qb = q.astype(BF16)
    kb = k.astype(BF16)
    sc = lax.dot_general(qb, kb, nt, preferred_element_type=F32) * mk_ref[0:c, :]
    sign = jnp.uint32(0x80000000)
    for li, m in enumerate(HG_LEVELS):
        z = cs - _pair_boundary(c_ref, cs, m, reverse)
        w = jnp.exp2(pltpu.bitcast(pltpu.bitcast(z, jnp.uint32) | sign, F32)).astype(BF16)
        sc = sc + (lax.dot_general(qb * w, kb * w, nt, preferred_element_type=F32)
                   * mk_ref[(li + 1) * c:(li + 2) * c, :])
    o = o + jnp.dot(sc.astype(BF16), v, preferred_element_type=F32)
    o_ref[:, col] = o.astype(BF16)
    kend = (k * jnp.exp2(c_tot - cs)).astype(BF16)
    s_ref[h] = st * jnp.exp2(c_tot) + lax.dot_general(v, kend, (((0,), (0,)), ((), ())), preferred_element_type=F32)


HG_UNROLL = 8


def _hgrn_kernel(hq_f, hi_f, fr_f, hq_b, hi_b, fr_b, lbf_ref, lbb_ref, cumf_ref, mkf_ref, cumb_ref, mkb_ref,
                 of_ref, ob_ref, sf_ref, sb_ref, c_s):
    @pl.when(pl.program_id(1) == 0)
    def _():
        sf_ref[...] = jnp.zeros_like(sf_ref)
        sb_ref[...] = jnp.zeros_like(sb_ref)

    def body(i, carry):
        for hh in range(HG_UNROLL):
            h = i * HG_UNROLL + hh
            _hgrn_head(h, hq_f, hi_f, fr_f, lbf_ref, cumf_ref, mkf_ref, sf_ref, of_ref, c_s.at[2 * hh], False)
            _hgrn_head(h, hq_b, hi_b, fr_b, lbb_ref, cumb_ref, mkb_ref, sb_ref, ob_ref, c_s.at[2 * hh + 1], True)
        return carry

    lax.fori_loop(0, H // HG_UNROLL, body, 0)


def _hgrn_call(p, fraw, lb_f, lb_b):
    cum_f, mk_f = _hgrn_tables(False)
    cum_b, mk_b = _hgrn_tables(True)
    wide = H * HW

    def cf(b, s):
        return b * HG_STEPS + s

    def cb(b, s):
        return b * HG_STEPS + jnp.where(s < HG_CTX, HG_CTX - 1 - s, HG_STEPS - 1 + HG_CTX - s)

    const = lambda shape: pl.BlockSpec(shape, lambda b, s: (0, 0))
    return pl.pallas_call(
        _hgrn_kernel,
        grid=(B, HG_STEPS),
        in_specs=[pl.BlockSpec((HG_C, wide), lambda b, s: (cf(b, s), 1)),
                  pl.BlockSpec((HG_C, wide), lambda b, s: (cf(b, s), 2)),
                  pl.BlockSpec((HG_C, wide), lambda b, s: (cf(b, s), 0)),
                  pl.BlockSpec((HG_C, wide), lambda b, s: (cb(b, s), 1)),
                  pl.BlockSpec((HG_C, wide), lambda b, s: (cb(b, s), 2)),
                  pl.BlockSpec((HG_C, wide), lambda b, s: (cb(b, s), 1)),
                  const((1, wide)), const((1, wide)),
                  const(cum_f.shape), const(mk_f.shape), const(cum_b.shape), const(mk_b.shape)],
        out_specs=[pl.BlockSpec((HG_C, wide), lambda b, s: (cf(b, s), 0)),
                   pl.BlockSpec((HG_C, wide), lambda b, s: (cb(b, s), 0))],
        out_shape=[jax.ShapeDtypeStruct((TA, wide), BF16), jax.ShapeDtypeStruct((TA, wide), BF16)],
        scratch_shapes=[pltpu.VMEM((H, HW, HW), F32), pltpu.VMEM((H, HW, HW), F32),
                        pltpu.VMEM((2 * HG_UNROLL, HG_C, HW), F32)],
        compiler_params=_cparams(("arbitrary", "arbitrary")),
        name="hgrn",
    )(p, p, fraw, p, p, fraw, lb_f, lb_b, cum_f, mk_f, cum_b, mk_b)


def _pack_rows(val, dst_ref, row0, rows):
    for s in range(PACK_W):
        lo = val[:, 2 * s * LANES:(2 * s + 1) * LANES]
        hi = val[:, (2 * s + 1) * LANES:(2 * s + 2) * LANES]
        w = pltpu.pack_elementwise([lo, hi], packed_dtype=BF16)
        dst_ref[pl.ds(row0 * PACK_W + s, rows, stride=PACK_W), :] = w


def _unpack_word(w):
    lo = pltpu.unpack_elementwise(w, index=0, packed_dtype=BF16, unpacked_dtype=F32)
    hi = pltpu.unpack_elementwise(w, index=1, packed_dtype=BF16, unpacked_dtype=F32)
    return lo, hi


def _mixout_kernel(att_ref, of_ref, ob_ref, hg_ref, gn_ref, wa_ref, wr_ref, x_ref, gt_ref, shf_ref, scf_ref,
                   lnp_ref, lnf_ref, rw_ref, rb_ref,
                   x1_ref, h2w_ref, ids_ref, gates_ref, cnt_ref, carry_ref):
    first = (pl.program_id(0) == 0) & (pl.program_id(1) == 0)

    @pl.when(first)
    def _():
        carry_ref[...] = jnp.zeros_like(carry_ref)

    o = of_ref[...].astype(F32) + ob_ref[...].astype(F32)
    gn = gn_ref[...]
    recs = []
    for h in range(H):
        oh = o[:, h * HW:(h + 1) * HW]
        gate = hg_ref[:, h * HW:(h + 1) * HW].astype(F32)
        recs.append((_rms(oh, gn) * (gate * _sigmoid(gate))).astype(BF16))
    rec = jnp.concatenate(recs, axis=-1)
    mix = (jnp.dot(att_ref[...], wa_ref[...], preferred_element_type=F32)
           + jnp.dot(rec, wr_ref[...], preferred_element_type=F32))
    x1 = x_ref[0] + gt_ref[0] * _rms(mix, lnp_ref[...])
    x1_ref[0] = x1
    h2 = _rms(x1, lnf_ref[...]) * (1.0 + scf_ref[0]) + shf_ref[0]
    _pack_rows(h2, h2w_ref, 0, ROW_T)

    h_hi = h2.astype(BF16)
    h_lo = (h2 - h_hi.astype(F32)).astype(BF16)
    hh = jnp.dot(h_hi, rw_ref[...], preferred_element_type=F32)
    logits = (hh[:, :LANES] + hh[:, LANES:] + jnp.dot(h_lo, rw_ref[:, :LANES], preferred_element_type=F32)
              + rb_ref[...])
    lane = lax.broadcasted_iota(I32, logits.shape, 1)
    neg = jnp.float32(-jnp.inf)
    lg = jnp.where(lane < NE, logits, neg)
    tops, idxs = [], []
    for _ in range(TOPK):
        m = jnp.max(lg, axis=-1, keepdims=True)
        idx = jnp.min(jnp.where(lg == m, lane, LANES), axis=-1, keepdims=True)
        tops.append(m)
        idxs.append(idx)
        lg = jnp.where(lane == idx, neg, lg)
    exps = [jnp.exp(tk - tops[0]) for tk in tops]
    inv = 1.0 / (exps[0] + exps[1] + exps[2] + exps[3])
    onehot = jnp.zeros(logits.shape, F32)
    for idx in idxs:
        onehot = onehot + jnp.where(lane == idx, 1.0, 0.0)
    r = lax.broadcasted_iota(I32, (ROW_T, ROW_T), 0)
    cidx = lax.broadcasted_iota(I32, (ROW_T, ROW_T), 1)
    lower = jnp.where(cidx < r, 1.0, 0.0).astype(BF16)
    prefix = jnp.dot(lower, onehot.astype(BF16), preferred_element_type=F32) + carry_ref[...]
    ids = jnp.zeros(logits.shape, I32)
    gates = jnp.zeros(logits.shape, F32)
    for kk in range(TOPK):
        rank = jnp.sum(jnp.where(lane == idxs[kk], prefix, 0.0), axis=-1, keepdims=True).astype(I32)
        ids = jnp.where(lane == kk, idxs[kk], ids)
        ids = jnp.where(lane == TOPK + kk, rank, ids)
        gates = jnp.where(lane == kk, exps[kk] * inv, gates)
    ids_ref[...] = ids
    gates_ref[...] = gates
    carry = carry_ref[...] + jnp.sum(onehot, axis=0, keepdims=True)
    carry_ref[...] = carry
    cnt_ref[...] = carry


def _mixout_call(att, o_f, o_b, p, g_norm_t, w_a, w_r, x, mod3, ln_post, ln_ffn, r_w, r_b):
    wide = H * HW
    lat = lambda b, i: b * TILES_A + 1 + i
    tok = lambda b, i: b * TILES_L + i
    const = lambda shape: pl.BlockSpec(shape, lambda b, i: tuple(0 for _ in shape))
    return pl.pallas_call(
        _mixout_kernel,
        grid=(B, TILES_L),
        in_specs=[pl.BlockSpec((ROW_T, wide), lambda b, i: (tok(b, i), 0)),
                  pl.BlockSpec((ROW_T, wide), lambda b, i: (lat(b, i), 0)),
                  pl.BlockSpec((ROW_T, wide), lambda b, i: (lat(b, i), 0)),
                  pl.BlockSpec((ROW_T, wide), lambda b, i: (lat(b, i), 3)),
                  const((1, HW)), const((wide, D)), const((wide, D)),
                  pl.BlockSpec((1, ROW_T, D), lambda b, i: (b, i, 0)),
                  pl.BlockSpec((1, 1, D), lambda b, i: (b, 0, 2)),
                  pl.BlockSpec((1, 1, D), lambda b, i: (b, 0, 3)),
                  pl.BlockSpec((1, 1, D), lambda b, i: (b, 0, 4)),
                  const((1, D)), const((1, D)), const((D, 2 * LANES)), const((1, LANES))],
        out_specs=[pl.BlockSpec((1, ROW_T, D), lambda b, i: (b, i, 0)),
                   pl.BlockSpec((ROW_T * PACK_W, LANES), lambda b, i: (tok(b, i), 0)),
                   pl.BlockSpec((ROW_T, LANES), lambda b, i: (tok(b, i), 0)),
                   pl.BlockSpec((ROW_T, LANES), lambda b, i: (tok(b, i), 0)),
                   const((1, LANES))],
        out_shape=[jax.ShapeDtypeStruct((B, N, D), F32),
                   jax.ShapeDtypeStruct((T * PACK_W, LANES), I32),
                   jax.ShapeDtypeStruct((T, LANES), I32),
                   jax.ShapeDtypeStruct((T, LANES), F32),
                   jax.ShapeDtypeStruct((1, LANES), F32)],
        scratch_shapes=[pltpu.VMEM((1, LANES), F32)],
        compiler_params=_cparams(("arbitrary", "arbitrary")),
        name="mixout_router",
    )(att, o_f, o_b, p, g_norm_t, w_a, w_r, x, mod3, mod3, mod3, ln_post, ln_ffn, r_w, r_b)


def _dispatch_copy(src_vmem, dst_hbm, sem, src_row, dst_row, rows):
    return pltpu.make_async_copy(src_vmem.at[pl.ds(src_row * PACK_W, rows * PACK_W)],
                                 dst_hbm.at[pl.ds(dst_row * PACK_W, rows * PACK_W)], sem)


def _zero_copy(zeros_s, dst_hbm, sem, dst_row, rows):
    return pltpu.make_async_copy(zeros_s.at[pl.ds(0, rows * PACK_W)],
                                 dst_hbm.at[pl.ds(dst_row * PACK_W, rows * PACK_W)], sem)


def _dispatch_kernel(dest_ref, pad_ref, h2w_ref, xs_hbm, zeros_s, sem, zsem):
    step = pl.program_id(0)
    base = step * DISP_T

    def issue(t, carry):
        for kk in range(TOPK):
            _dispatch_copy(h2w_ref, xs_hbm, sem, t, dest_ref[(base + t) * TOPK + kk], 1).start()
        return carry

    tail0 = pad_ref[2 * NE]
    tail_n = pad_ref[2 * NE + 1]

    def zero_rows(wait):
        def expert_pad(e, carry):
            def one(r, c):
                cp = _zero_copy(zeros_s, xs_hbm, zsem, pad_ref[e] + r, 1)
                cp.wait() if wait else cp.start()
                return c
            lax.fori_loop(0, pad_ref[NE + e], one, 0)
            return carry

        def tail(i, c):
            cp = _zero_copy(zeros_s, xs_hbm, zsem, (tail0 + i) * SUB, SUB)
            cp.wait() if wait else cp.start()
            return c

        lax.fori_loop(0, NE, expert_pad, 0)
        lax.fori_loop(0, tail_n, tail, 0)

    @pl.when(step == 0)
    def _():
        zeros_s[...] = jnp.zeros_like(zeros_s)
        zero_rows(wait=False)

    lax.fori_loop(0, DISP_T, issue, 0)

    @pl.when(step == 0)
    def _():
        zero_rows(wait=True)

    for _ in range(TOPK):
        _dispatch_copy(h2w_ref, xs_hbm, sem, 0, 0, DISP_T).wait()


def _dispatch_call(dest, pad, h2w):
    gs = pltpu.PrefetchScalarGridSpec(
        num_scalar_prefetch=2, grid=(T // DISP_T,),
        in_specs=[pl.BlockSpec((DISP_T * PACK_W, LANES), lambda i, d, z: (i, 0))],
        out_specs=pl.BlockSpec(memory_space=pl.ANY),
        scratch_shapes=[pltpu.VMEM((SUB * PACK_W, LANES), I32),
                        pltpu.SemaphoreType.DMA(()),
                        pltpu.SemaphoreType.DMA(())])
    return pl.pallas_call(
        _dispatch_kernel, grid_spec=gs,
        out_shape=jax.ShapeDtypeStruct((N_SUBS * SUB * PACK_W, LANES), I32),
        compiler_params=_cparams(("arbitrary",)),
        name="moe_dispatch",
    )(dest, pad, h2w)


def _sub_in_copy(xs_hbm, stage, sems, blk, slot):
    return pltpu.make_async_copy(xs_hbm.at[pl.ds(blk * (SUB * PACK_W), SUB * PACK_W)], stage.at[slot], sems.at[slot])


def _sub_out_copy(stage, out_hbm, sems, blk, slot):
    return pltpu.make_async_copy(stage.at[slot], out_hbm.at[pl.ds(blk * (SUB * PACK_W), SUB * PACK_W)], sems.at[slot])


def _for_row_chunks(nblk, fn):
    pair = 2 * CHUNK_SUBS

    @pl.when(nblk >= pair)
    def _():
        fn(0, CHUNK_SUBS * SUB)
        fn(CHUNK_SUBS * SUB, CHUNK_SUBS * SUB)

    for c in range(-(-SB_SUBS // CHUNK_SUBS)):
        row0 = c * CHUNK_SUBS * SUB
        most = min(CHUNK_SUBS, SB_SUBS - c * CHUNK_SUBS)
        for r in range(1, most + 1):
            full = r == CHUNK_SUBS
            cond = (nblk >= c * CHUNK_SUBS + r) if full else (nblk == c * CHUNK_SUBS + r)
            if c < 2:
                cond = cond & (nblk < pair)
            pl.when(cond)(functools.partial(fn, row0, r * SUB))


def _wtile_copy(w_hbm, buf, sems, e, col_tile, slot, which):
    return pltpu.make_async_copy(w_hbm.at[e, :, pl.ds(pl.multiple_of(col_tile * FF_T, FF_T), FF_T)],
                                 buf.at[slot, which], sems.at[slot, which])


def _expert_kernel(sbe_ref, sbn_ref, sbb_ref, xs_hbm, wgu_hbm, bgu_ref, wdn_hbm, bdn_ref,
                   out_hbm, x_s, act_s, wbuf, stage_in, stage_out,
                   sem_w, sem_in, sem_out):
    s = pl.program_id(0)
    nblk = sbn_ref[s]
    blk0 = sbb_ref[s]
    e = sbe_ref[s]

    def gate_tiles(j, slot):
        return (_wtile_copy(wgu_hbm, wbuf, sem_w, e, j, slot, 0),
                _wtile_copy(wgu_hbm, wbuf, sem_w, e, N1 + j, slot, 1))

    def down_tile(j, slot):
        return _wtile_copy(wdn_hbm, wbuf, sem_w, e, j, slot, 0)

    @pl.when(nblk > 0)
    def _():
        for cp in gate_tiles(0, 0):
            cp.start()

        _sub_in_copy(xs_hbm, stage_in, sem_in, blk0, 0).start()
        for i in range(SB_SUBS):
            @pl.when(i < nblk)
            def _():
                if i + 1 < SB_SUBS:
                    @pl.when(i + 1 < nblk)
                    def _():
                        _sub_in_copy(xs_hbm, stage_in, sem_in, blk0 + i + 1, (i + 1) % 2).start()
                _sub_in_copy(xs_hbm, stage_in, sem_in, blk0 + i, i % 2).wait()
                for w in range(PACK_W):
                    lo, hi = _unpack_word(stage_in[i % 2, pl.ds(w, SUB, stride=PACK_W), :])
                    x_s[i * SUB:(i + 1) * SUB, 2 * w * LANES:(2 * w + 1) * LANES] = lo.astype(BF16)
                    x_s[i * SUB:(i + 1) * SUB, (2 * w + 1) * LANES:(2 * w + 2) * LANES] = hi.astype(BF16)

        def hidden_step(j, carry):
            slot = j % 2
            for cp in gate_tiles(j, slot):
                cp.wait()

            @pl.when(j + 1 < N1)
            def _():
                for cp in gate_tiles(j + 1, 1 - slot):
                    cp.start()

            @pl.when(j + 1 == N1)
            def _():
                down_tile(0, 1 - slot).start()

            col = pl.ds(pl.multiple_of(j * FF_T, FF_T), FF_T)
            bg = bgu_ref[0, :, col]
            bl = bgu_ref[0, :, pl.ds(pl.multiple_of(DFF + j * FF_T, FF_T), FF_T)]

            def gate_up(row0, rows):
                x = x_s[row0:row0 + rows, :]
                gu = jnp.dot(x, wbuf[slot, 0].astype(BF16), preferred_element_type=F32) + bg
                li = jnp.dot(x, wbuf[slot, 1].astype(BF16), preferred_element_type=F32) + bl
                glu = jnp.minimum(gu, LIMIT)
                lin = jnp.clip(li, -LIMIT, LIMIT)
                act = glu * _sigmoid(ALPHA * glu) * (lin + 1.0)
                act_s[row0:row0 + rows, col] = act.astype(BF16)

            _for_row_chunks(nblk, gate_up)
            return carry

        lax.fori_loop(0, N1, hidden_step, 0)

        def out_step(j2, carry):
            slot = (N1 + j2) % 2
            down_tile(j2, slot).wait()

            @pl.when(j2 + 1 < N2)
            def _():
                down_tile(j2 + 1, 1 - slot).start()

            bd = bdn_ref[0, :, pl.ds(pl.multiple_of(j2 * FF_T, FF_T), FF_T)]

            def down(row0, rows):
                acc = jnp.dot(act_s[row0:row0 + rows, :], wbuf[slot, 0].astype(BF16), preferred_element_type=F32) + bd
                x_s[row0:row0 + rows, pl.ds(pl.multiple_of(j2 * FF_T, FF_T), FF_T)] = acc.astype(BF16)

            _for_row_chunks(nblk, down)
            return carry

        lax.fori_loop(0, N2, out_step, 0)

        for i in range(SB_SUBS):
            @pl.when(i < nblk)
            def _():
                if i >= 2:
                    _sub_out_copy(stage_out, out_hbm, sem_out, blk0 + i - 2, i % 2).wait()
                _pack_rows(x_s[i * SUB:(i + 1) * SUB, :].astype(F32), stage_out.at[i % 2], 0, SUB)
                _sub_out_copy(stage_out, out_hbm, sem_out, blk0 + i, i % 2).start()
        for i in range(SB_SUBS):
            @pl.when((i < nblk) & (i + 2 >= nblk))
            def _():
                _sub_out_copy(stage_out, out_hbm, sem_out, blk0 + i, i % 2).wait()


def _expert_call(sb_e, sb_n, sb_b, xs, w_gu, b_gu, w_dn, b_dn):
    gs = pltpu.PrefetchScalarGridSpec(
        num_scalar_prefetch=3, grid=(S_MAX,),
        in_specs=[pl.BlockSpec(memory_space=pl.ANY),
                  pl.BlockSpec(memory_space=pl.ANY),
                  pl.BlockSpec((1, 1, 2 * DFF), lambda s, e, n, b: (e[s], 0, 0)),
                  pl.BlockSpec(memory_space=pl.ANY),
                  pl.BlockSpec((1, 1, D), lambda s, e, n, b: (e[s], 0, 0))],
        out_specs=pl.BlockSpec(memory_space=pl.ANY),
        scratch_shapes=[pltpu.VMEM((SB_R, D), BF16),
                        pltpu.VMEM((SB_R, DFF), BF16),
                        pltpu.VMEM((2, 2, D, FF_T), F32),
                        pltpu.VMEM((2, SUB * PACK_W, LANES), I32),
                        pltpu.VMEM((2, SUB * PACK_W, LANES), I32),
                        pltpu.SemaphoreType.DMA((2, 2)),
                        pltpu.SemaphoreType.DMA((2,)),
                        pltpu.SemaphoreType.DMA((2,))])
    return pl.pallas_call(
        _expert_kernel, grid_spec=gs,
        out_shape=jax.ShapeDtypeStruct(xs.shape, xs.dtype),
        input_output_aliases={3: 0},
        compiler_params=_cparams(("arbitrary",)),
        name="moe_experts",
    )(sb_e, sb_n, sb_b, xs, w_gu, b_gu, w_dn, b_dn)


def _combine_copy(src_hbm, buf, sem, src_row, dst_row, rows):
    return pltpu.make_async_copy(src_hbm.at[pl.ds(src_row * PACK_W, rows * PACK_W)],
                                 buf.at[pl.ds(dst_row * PACK_W, rows * PACK_W)], sem)


def _combine_kernel(dest_ref, ys_hbm, gates_ref, x1_ref, gt_ref, ln_ref, o_ref, bufs, y_s, sems):
    step = pl.program_id(0) * TILES_L + pl.program_id(1)

    def issue(tile, slot):
        def body(tg, carry):
            for tt in range(COMBINE_UNROLL):
                t = tg * COMBINE_UNROLL + tt
                for kk in range(TOPK):
                    _combine_copy(ys_hbm, bufs.at[slot], sems.at[slot], dest_ref[(tile * ROW_T + t) * TOPK + kk],
                                  kk * ROW_T + t, 1).start()
            return carry
        lax.fori_loop(0, ROW_T // COMBINE_UNROLL, body, 0)

    @pl.when(step == 0)
    def _():
        issue(0, 0)

    @pl.when(step + 1 < B * TILES_L)
    def _():
        issue(step + 1, (step + 1) % 2)

    slot = step % 2
    buf = bufs.at[slot]
    _combine_copy(ys_hbm, buf, sems.at[slot], 0, 0, ROW_T * TOPK).wait()

    gates = gates_ref[...]
    gk = [jnp.broadcast_to(gates[:, kk:kk + 1], (ROW_T, LANES)) for kk in range(TOPK)]
    for w in range(PACK_W):
        acc_lo = jnp.zeros((ROW_T, LANES), F32)
        acc_hi = jnp.zeros((ROW_T, LANES), F32)
        for kk in range(TOPK):
            lo, hi = _unpack_word(buf[pl.ds(kk * ROW_T * PACK_W + w, ROW_T, stride=PACK_W), :])
            acc_lo = acc_lo + gk[kk] * lo
            acc_hi = acc_hi + gk[kk] * hi
        y_s[:, 2 * w * LANES:(2 * w + 1) * LANES] = acc_lo
        y_s[:, (2 * w + 1) * LANES:(2 * w + 2) * LANES] = acc_hi
    o_ref[0] = x1_ref[0] + gt_ref[0] * _rms(y_s[...], ln_ref[...])


def _combine_call(dest, ys, gates, x1, mod3, ln_post):
    tok = lambda b, i: b * TILES_L + i
    gs = pltpu.PrefetchScalarGridSpec(
        num_scalar_prefetch=1, grid=(B, TILES_L),
        in_specs=[pl.BlockSpec(memory_space=pl.ANY),
                  pl.BlockSpec((ROW_T, LANES), lambda b, i, d: (tok(b, i), 0)),
                  pl.BlockSpec((1, ROW_T, D), lambda b, i, d: (b, i, 0)),
                  pl.BlockSpec((1, 1, D), lambda b, i, d: (b, 0, 5)),
                  pl.BlockSpec((1, D), lambda b, i, d: (0, 0))],
        out_specs=pl.BlockSpec((1, ROW_T, D), lambda b, i, d: (b, i, 0)),
        scratch_shapes=[pltpu.VMEM((2, TOPK * ROW_T * PACK_W, LANES), I32),
                        pltpu.VMEM((ROW_T, D), F32),
                        pltpu.SemaphoreType.DMA((2,))])
    return pl.pallas_call(
        _combine_kernel, grid_spec=gs,
        out_shape=jax.ShapeDtypeStruct((B, N, D), F32),
        compiler_params=_cparams(("arbitrary", "arbitrary")),
        name="moe_combine",
    )(dest, ys, gates, x1, mod3, ln_post)


def _rot_cols(w):
    half = ROPE // 2
    return jnp.concatenate([-w[..., half:], w[..., :half]], axis=-1)


def _rope_tables():
    t = np.arange(N)
    r = (t // GRID_W).astype(np.float64)
    cl = (t % GRID_W).astype(np.float64)
    n_freq = ROPE // 4
    inv = ROPE_THETA ** (-np.arange(n_freq, dtype=np.float64) / n_freq)
    ang = np.concatenate([r[:, None] * inv, cl[:, None] * inv], axis=-1)
    cos, sin = np.cos(ang), np.sin(ang)
    cs = np.concatenate([cos, cos, sin, sin], axis=-1).astype(np.float32)
    ctx = np.concatenate([np.ones((M, ROPE), np.float32), np.zeros((M, ROPE), np.float32)], axis=-1)
    return jnp.asarray(np.concatenate([ctx, cs], axis=0)), jnp.asarray(cs * QK_SCALE)


def _routing_tables(ids, counts_f):
    e = ids[:, :TOPK]
    rank = ids[:, TOPK:2 * TOPK]
    counts = counts_f[0, :NE].astype(I32)
    nsub = (counts + SUB - 1) // SUB
    sub_start = jnp.cumsum(nsub) - nsub
    onehot = e[..., None] == jnp.arange(NE, dtype=I32)
    dest = jnp.sum(jnp.where(onehot, sub_start * SUB, 0), axis=-1) + rank
    nsb = (nsub + SB_SUBS - 1) // SB_SUBS
    sb_end = jnp.cumsum(nsb)
    sb_start = sb_end - nsb
    n_active = sb_end[-1]
    s_idx = jnp.arange(S_MAX, dtype=I32)
    s_eff = jnp.minimum(s_idx, n_active - 1)
    sb_e = jnp.minimum(jnp.searchsorted(sb_end, s_eff, side="right"), NE - 1).astype(I32)
    local = s_eff - sb_start[sb_e]
    sb_b = (sub_start[sb_e] + local * SB_SUBS).astype(I32)
    sb_n = jnp.where(s_idx < n_active, jnp.clip(nsub[sb_e] - local * SB_SUBS, 0, SB_SUBS), 0).astype(I32)
    pad = jnp.concatenate([sub_start * SUB + counts, nsub * SUB - counts,
                           jnp.stack([jnp.sum(nsub), N_SUBS - jnp.sum(nsub)])]).astype(I32)
    return dest.reshape(-1).astype(I32), pad, sb_e, sb_n, sb_b


def kernel(x, c, ctx, c_ctx, w_mod, b_mod, ln_mix_pre, ln_mix_post, ln_ffn_pre, ln_ffn_post, w_in, q_norm, kv_norm,
           w_uq, w_ukv, hgrn_lb, g_norm, w_out, router_w, router_b, w_gate_up, b_gate_up, w_down, b_down):
    c_all = jnp.concatenate([c, c_ctx[None, :], jnp.zeros((8 - B - 1, D), F32)], axis=0)
    mod = _mod_call(c_all, w_mod[0], b_mod)
    mod3 = mod.reshape(8, 1, 6 * D)

    w = w_in[0]
    o_q, o_kv, o_kr = 0, Q_RANK, Q_RANK + KV_RANK
    o_hq = o_kr + ROPE
    wide = H * HW
    o_ff, o_fb, o_hi, o_hg = o_hq + wide, o_hq + 2 * wide, o_hq + 3 * wide, o_hq + 4 * wide
    w_kr = w[:, o_kr:o_kr + ROPE]
    w_p = jnp.concatenate([w[:, o_q:o_kr], w_kr, _rot_cols(w_kr), jnp.zeros((D, LANES), F32),
                           w[:, o_hq:o_hq + wide], w[:, o_hi:o_hi + wide], w[:, o_hg:o_hg + wide]],
                          axis=1).astype(BF16)
    w_f = w[:, o_ff:o_ff + 2 * wide].astype(BF16)
    wq = w_uq[0]
    wq_r = wq[..., HW:]
    w_q = jnp.concatenate([wq[..., :HW], wq_r, _rot_cols(wq_r)], axis=-1).reshape(Q_RANK, 2 * wide).astype(BF16)
    wkv = w_ukv[0]
    w_kv = jnp.concatenate([wkv[..., :HW].reshape(KV_RANK, wide), wkv[..., HW:].reshape(KV_RANK, wide)],
                           axis=1).astype(BF16)
    w_o = w_out[0].astype(BF16)
    cs_all, cs_q = _rope_tables()
    lb = jnp.cumsum(jax.nn.softmax(hgrn_lb.astype(F32), axis=1), axis=1)
    r_w = jnp.concatenate([router_w[0], jnp.zeros((D, LANES - NE), F32)], axis=1)
    r_w_hi = r_w.astype(BF16)
    r_w = jnp.concatenate([r_w_hi, (r_w - r_w_hi.astype(F32)).astype(BF16)], axis=1)
    r_b = jnp.concatenate([router_b[0], jnp.zeros((LANES - NE,), F32)])[None, :]

    h_all = _normmod_call(x, ctx, mod3, ln_mix_pre)
    h_flat = h_all.reshape(TA, D)
    p = _matmul_call(h_flat, w_p, BF16, "in_proj")
    fraw = _matmul_call(h_flat, w_f, F32, "in_proj_forget")
    k_all, v_all = _kvprep_call(p, kv_norm, w_kv, cs_all)
    q_all = _qprep_call(p, q_norm, w_q, cs_q)
    att = _attn_call(q_all, k_all, v_all)
    o_f, o_b = _hgrn_call(p, fraw, lb[0, 0][None, :], lb[1, 0][None, :])

    g_norm_t = g_norm
    x1, h2w, ids, gates, counts = _mixout_call(att, o_f, o_b, p, g_norm_t, w_o[:wide], w_o[wide:], x, mod3,
                                               ln_mix_post, ln_ffn_pre, r_w, r_b)

    dest, pad, sb_e, sb_n, sb_b = _routing_tables(ids, counts)
    xs = _dispatch_call(dest, pad, h2w)
    ys = _expert_call(sb_e, sb_n, sb_b, xs, w_gate_up[0], b_gate_up[0][:, None, :], w_down[0], b_down[0][:, None, :])
    return _combine_call(dest, ys, gates, x1, mod3, ln_ffn_post)
```

```python
import functools

import numpy as np
import jax
import jax.numpy as jnp
from jax import lax
from jax.experimental import pallas as pl
from jax.experimental.pallas import tpu as pltpu

F32 = jnp.float32
BF16 = jnp.bfloat16
I32 = jnp.int32

D = 2048
B = 4
N = 4096
M = 256
NA = N + M
T = B * N
TA = B * NA
EPS = 1e-6
H = 8
HW = 128
Q_RANK = 512
KV_RANK = 256
ROPE = 64
QK_SCALE = float((128 + ROPE) ** -0.5)
GRID_W = 64
ROPE_THETA = 10000.0
NE = 32
TOPK = 4
DFF = 2048
LIMIT = 7.0
ALPHA = 1.702
LOG2E = float(np.log2(np.e))

LANES = 128
SUBLANES = 8
VMEM_LIMIT = 56 * 1024 * 1024

ROW_T = 256
TILES_A = NA // ROW_T
TILES_L = N // ROW_T
MM_TM = 1024
MM_TN = 1024
HG_C = 256
HG_STEPS = NA // HG_C
HG_CTX = M // HG_C
HG_LEVELS = tuple(HG_C >> i for i in range(1, HG_C.bit_length()))
SUB = 256
N_SUBS = T * TOPK // SUB + NE
SB_SUBS = 14
SB_R = SB_SUBS * SUB
CHUNK_SUBS = 4
S_MAX = -(-N_SUBS // SB_SUBS) + NE
FF_T = 256
N1 = DFF // FF_T
N2 = D // FF_T
PACK_W = D // 2 // LANES
DISP_T = 512
COMBINE_UNROLL = 4


def _sigmoid(x):
    return 1.0 / (1.0 + jnp.exp(-x))


def _rms(x, w):
    ms = jnp.mean(x * x, axis=-1, keepdims=True)
    return x * lax.rsqrt(ms + EPS) * w


def _cparams(sem, vmem=VMEM_LIMIT):
    return pltpu.CompilerParams(dimension_semantics=sem, vmem_limit_bytes=vmem)


def _mod_kernel(c_ref, w_ref, b_ref, o_ref):
    c = c_ref[...]
    a = (c * _sigmoid(c)).astype(BF16)
    o_ref[...] = jnp.dot(a, w_ref[...].astype(BF16), preferred_element_type=F32) + b_ref[...]


def _mod_call(c_all, w_mod, b_mod):
    tn = 1024
    return pl.pallas_call(
        _mod_kernel,
        grid=(6 * D // tn,),
        in_specs=[pl.BlockSpec((8, D), lambda j: (0, 0)),
                  pl.BlockSpec((D, tn), lambda j: (0, j)),
                  pl.BlockSpec((1, tn), lambda j: (0, j))],
        out_specs=pl.BlockSpec((8, tn), lambda j: (0, j)),
        out_shape=jax.ShapeDtypeStruct((8, 6 * D), F32),
        compiler_params=_cparams(("arbitrary",)),
        name="mod",
    )(c_all, w_mod, b_mod)


def _normmod_kernel(x_ref, ctx_ref, sh_ref, sc_ref, w_ref, o_ref):
    def go(src):
        y = _rms(src, w_ref[...])
        o_ref[0] = (y * (1.0 + sc_ref[0]) + sh_ref[0]).astype(BF16)

    i = pl.program_id(1)

    @pl.when(i == 0)
    def _():
        go(ctx_ref[0])

    @pl.when(i > 0)
    def _():
        go(x_ref[0])


def _normmod_call(x, ctx, mod3, ln_w):
    row = lambda b, i: jnp.where(i == 0, B, b)
    return pl.pallas_call(
        _normmod_kernel,
        grid=(B, TILES_A),
        in_specs=[pl.BlockSpec((1, ROW_T, D), lambda b, i: (b, jnp.maximum(i - 1, 0), 0)),
                  pl.BlockSpec((1, ROW_T, D), lambda b, i: (b, 0, 0)),
                  pl.BlockSpec((1, 1, D), lambda b, i: (row(b, i), 0, 0)),
                  pl.BlockSpec((1, 1, D), lambda b, i: (row(b, i), 0, 1)),
                  pl.BlockSpec((1, D), lambda b, i: (0, 0))],
        out_specs=pl.BlockSpec((1, ROW_T, D), lambda b, i: (b, i, 0)),
        out_shape=jax.ShapeDtypeStruct((B, NA, D), BF16),
        compiler_params=_cparams(("arbitrary", "arbitrary")),
        name="normmod",
    )(x, ctx, mod3, mod3, ln_w)


def _matmul_kernel(a_ref, w_ref, o_ref):
    o_ref[...] = jnp.dot(a_ref[...], w_ref[...], preferred_element_type=F32).astype(o_ref.dtype)


def _matmul_call(a, w, out_dtype, name):
    rows, k = a.shape
    cols = w.shape[1]
    return pl.pallas_call(
        _matmul_kernel,
        grid=(rows // MM_TM, cols // MM_TN),
        in_specs=[pl.BlockSpec((MM_TM, k), lambda i, j: (i, 0)),
                  pl.BlockSpec((k, MM_TN), lambda i, j: (0, j))],
        out_specs=pl.BlockSpec((MM_TM, MM_TN), lambda i, j: (i, j)),
        out_shape=jax.ShapeDtypeStruct((rows, cols), out_dtype),
        compiler_params=_cparams(("arbitrary", "arbitrary")),
        name=name,
    )(a, w)


def _kvprep_kernel(p_ref, kvn_ref, w_ref, cs_ref, k_ref, v_ref):
    ckv = p_ref[:, Q_RANK:Q_RANK + KV_RANK].astype(F32)
    ckvn = _rms(ckv, kvn_ref[...]).astype(BF16)
    kv = jnp.dot(ckvn, w_ref[...], preferred_element_type=F32)
    a = p_ref[:, Q_RANK + KV_RANK:Q_RANK + KV_RANK + LANES].astype(F32) * cs_ref[...]
    r = a + pltpu.roll(a, ROPE, axis=1)
    lane = lax.broadcasted_iota(I32, r.shape, 1)
    r = jnp.where(lane < ROPE, r, 0.0).astype(BF16)
    for h in range(H):
        k_ref[0, h, :, 0:HW] = kv[:, h * HW:(h + 1) * HW].astype(BF16)
        k_ref[0, h, :, HW:2 * HW] = r
        v_ref[0, h] = kv[:, (H + h) * HW:(H + h + 1) * HW].astype(BF16)


def _kvprep_call(p, kv_norm, w_kv, cs_all):
    return pl.pallas_call(
        _kvprep_kernel,
        grid=(B, TILES_A),
        in_specs=[pl.BlockSpec((ROW_T, 1024), lambda b, i: (b * TILES_A + i, 0)),
                  pl.BlockSpec((1, KV_RANK), lambda b, i: (0, 0)),
                  pl.BlockSpec((KV_RANK, 2 * H * HW), lambda b, i: (0, 0)),
                  pl.BlockSpec((ROW_T, LANES), lambda b, i: (i, 0))],
        out_specs=[pl.BlockSpec((1, H, ROW_T, 2 * HW), lambda b, i: (b, 0, i, 0)),
                   pl.BlockSpec((1, H, ROW_T, HW), lambda b, i: (b, 0, i, 0))],
        out_shape=[jax.ShapeDtypeStruct((B, H, NA, 2 * HW), BF16),
                   jax.ShapeDtypeStruct((B, H, NA, HW), BF16)],
        compiler_params=_cparams(("arbitrary", "arbitrary")),
        name="kvprep",
    )(p, kv_norm, w_kv, cs_all)


def _qprep_kernel(p_ref, qn_ref, w_ref, cs_ref, q_ref):
    cq = p_ref[:, 0:Q_RANK].astype(F32)
    cqn = _rms(cq, qn_ref[...]).astype(BF16)
    q = jnp.dot(cqn, w_ref[...], preferred_element_type=F32)
    cs = cs_ref[...]
    for h in range(H):
        q_ref[0, h, :, 0:HW] = (q[:, 2 * h * HW:(2 * h + 1) * HW] * QK_SCALE).astype(BF16)
        a = q[:, (2 * h + 1) * HW:(2 * h + 2) * HW] * cs
        q_ref[0, h, :, HW:2 * HW] = (a + pltpu.roll(a, ROPE, axis=1)).astype(BF16)


def _qprep_call(p, q_norm, w_q, cs_q):
    return pl.pallas_call(
        _qprep_kernel,
        grid=(B, TILES_L),
        in_specs=[pl.BlockSpec((ROW_T, 1024), lambda b, i: (b * TILES_A + 1 + i, 0)),
                  pl.BlockSpec((1, Q_RANK), lambda b, i: (0, 0)),
                  pl.BlockSpec((Q_RANK, 2 * H * HW), lambda b, i: (0, 0)),
                  pl.BlockSpec((ROW_T, LANES), lambda b, i: (i, 0))],
        out_specs=pl.BlockSpec((1, H, ROW_T, 2 * HW), lambda b, i: (b, 0, i, 0)),
        out_shape=jax.ShapeDtypeStruct((B, H, N, 2 * HW), BF16),
        compiler_params=_cparams(("arbitrary", "arbitrary")),
        name="qprep",
    )(p, q_norm, w_q, cs_q)


ATT_TQ = 256
ATT_TILES = 16


def _attn_kernel(q_ref, k_ref, v_ref, o_ref):
    for t in range(ATT_TILES):
        rows = slice(t * ATT_TQ, (t + 1) * ATT_TQ)
        s = lax.dot_general(q_ref[0, 0, rows, :], k_ref[0, 0], (((1,), (1,)), ((), ())),
                            preferred_element_type=F32)
        m = jnp.max(s, axis=-1, keepdims=True)
        p = jnp.exp(s - m)
        l = jnp.sum(p, axis=-1, keepdims=True)
        o = jnp.dot(p.astype(BF16), v_ref[0, 0], preferred_element_type=F32)
        o_ref[rows, :] = (o * (1.0 / l)).astype(BF16)


def _attn_call(q, k, v):
    tq = ATT_TQ * ATT_TILES
    return pl.pallas_call(
        _attn_kernel,
        grid=(B, H, N // tq),
        in_specs=[pl.BlockSpec((1, 1, tq, 2 * HW), lambda b, h, i: (b, h, i, 0)),
                  pl.BlockSpec((1, 1, NA, 2 * HW), lambda b, h, i: (b, h, 0, 0)),
                  pl.BlockSpec((1, 1, NA, HW), lambda b, h, i: (b, h, 0, 0))],
        out_specs=pl.BlockSpec((tq, HW), lambda b, h, i: (b * (N // tq) + i, h)),
        out_shape=jax.ShapeDtypeStruct((T, H * HW), BF16),
        compiler_params=_cparams(("arbitrary", "arbitrary", "arbitrary")),
        name="attention",
    )(q, k, v)


def _hgrn_tables(reverse):
    c = HG_C
    t = np.arange(c)[:, None]
    u = np.arange(c)[None, :]
    cum = (u <= t).astype(np.float32)
    masks = [(u == t)]
    for m in HG_LEVELS:
        blk_t = t // m
        masks.append(((blk_t % 2) == 1) & ((u // m) == blk_t - 1))
    masks = np.stack(masks).astype(np.float32)
    if reverse:
        cum = cum[::-1, ::-1]
        masks = masks[:, ::-1, ::-1]
    return jnp.asarray(cum, dtype=BF16), jnp.asarray(masks.reshape(-1, c), dtype=F32)


def _pair_boundary(c_ref, cval, m, reverse):
    def rows_of(r_of_group):
        return jnp.concatenate([jnp.broadcast_to(c_ref[r_of_group(gi):r_of_group(gi) + 1, :], (SUBLANES, HW))
                                for gi in range(HG_C // SUBLANES)], axis=0)

    if m >= 4:
        def boundary(gi):
            pair = (SUBLANES * gi // (2 * m)) * 2 * m
            return pair + m if reverse else pair + m - 1
        return rows_of(boundary)
    row = lax.broadcasted_iota(I32, (HG_C, HW), 0)
    if m == 2:
        first = rows_of(lambda gi: SUBLANES * gi + (2 if reverse else 1))
        second = rows_of(lambda gi: SUBLANES * gi + (6 if reverse else 5))
        return jnp.where((row & 7) < 4, first, second)
    if reverse:
        return jnp.where((row & 1) == 0, pltpu.roll(cval, HG_C - 1, 0), cval)
    return jnp.where((row & 1) == 1, pltpu.roll(cval, 1, 0), cval)


def _hgrn_head(h, hq_ref, hi_ref, fr_ref, lb_ref, cum_ref, mk_ref, s_ref, o_ref, c_ref, reverse):
    c = HG_C
    col = pl.ds(pl.multiple_of(h * HW, HW), HW)
    xq = hq_ref[:, col].astype(F32)
    q = xq * _sigmoid(xq)
    lb = lb_ref[:, col]
    f = lb + (1.0 - lb) * _sigmoid(fr_ref[:, col])
    g = jnp.log(f)
    k = 1.0 - f
    v = hi_ref[:, col]
    g_hi = g.astype(BF16)
    g_lo = (g - g_hi.astype(F32)).astype(BF16)
    cum = cum_ref[...]
    cs = jnp.dot(jnp.concatenate([cum, cum], axis=1), jnp.concatenate([g_hi, g_lo], axis=0),
                 preferred_element_type=F32) * LOG2E
    c_ref[...] = cs
    total_row = 0 if reverse else c - 1
    c_tot = c_ref[total_row:total_row + 1, :]

    nt = (((1,), (1,)), ((), ()))
    st = s_ref[h]
    o = lax.dot_general((q * jnp.exp2(cs)).astype(BF16), st.astype(BF16), nt, preferred_element_type=F32)
    qb = q.astype(BF16)
    kb = k.astype(BF16)
    sc = lax.dot_general(qb, kb, nt, preferred_element_type=F32) * mk_ref[0:c, :]
    sign = jnp.uint32(0x80000000)
    for li, m in enumerate(HG_LEVELS):
        z = cs - _pair_boundary(c_ref, cs, m, reverse)
        w = jnp.exp2(pltpu.bitcast(pltpu.bitcast(z, jnp.uint32) | sign, F32)).astype(BF16)
        sc = sc + (lax.dot_general(qb * w, kb * w, nt, preferred_element_type=F32)
                   * mk_ref[(li + 1) * c:(li + 2) * c, :])
    o = o + jnp.dot(sc.astype(BF16), v, preferred_element_type=F32)
    o_ref[:, col] = o.astype(BF16)
    kend = (k * jnp.exp2(c_tot - cs)).astype(BF16)
    s_ref[h] = st * jnp.exp2(c_tot) + lax.dot_general(v, kend, (((0,), (0,)), ((), ())), preferred_element_type=F32)


HG_UNROLL = 8


def _hgrn_kernel(hq_f, hi_f, fr_f, hq_b, hi_b, fr_b, lbf_ref, lbb_ref, cumf_ref, mkf_ref, cumb_ref, mkb_ref,
                 of_ref, ob_ref, sf_ref, sb_ref, c_s):
    @pl.when(pl.program_id(1) == 0)
    def _():
        sf_ref[...] = jnp.zeros_like(sf_ref)
        sb_ref[...] = jnp.zeros_like(sb_ref)

    def body(i, carry):
        for hh in range(HG_UNROLL):
            h = i * HG_UNROLL + hh
            _hgrn_head(h, hq_f, hi_f, fr_f, lbf_ref, cumf_ref, mkf_ref, sf_ref, of_ref, c_s.at[2 * hh], False)
            _hgrn_head(h, hq_b, hi_b, fr_b, lbb_ref, cumb_ref, mkb_ref, sb_ref, ob_ref, c_s.at[2 * hh + 1], True)
        return carry

    lax.fori_loop(0, H // HG_UNROLL, body, 0)


def _hgrn_call(p, fraw, lb_f, lb_b):
    cum_f, mk_f = _hgrn_tables(False)
    cum_b, mk_b = _hgrn_tables(True)
    wide = H * HW

    def cf(b, s):
        return b * HG_STEPS + s

    def cb(b, s):
        return b * HG_STEPS + jnp.where(s < HG_CTX, HG_CTX - 1 - s, HG_STEPS - 1 + HG_CTX - s)

    const = lambda shape: pl.BlockSpec(shape, lambda b, s: (0, 0))
    return pl.pallas_call(
        _hgrn_kernel,
        grid=(B, HG_STEPS),
        in_specs=[pl.BlockSpec((HG_C, wide), lambda b, s: (cf(b, s), 1)),
                  pl.BlockSpec((HG_C, wide), lambda b, s: (cf(b, s), 2)),
                  pl.BlockSpec((HG_C, wide), lambda b, s: (cf(b, s), 0)),
                  pl.BlockSpec((HG_C, wide), lambda b, s: (cb(b, s), 1)),
                  pl.BlockSpec((HG_C, wide), lambda b, s: (cb(b, s), 2)),
                  pl.BlockSpec((HG_C, wide), lambda b, s: (cb(b, s), 1)),
                  const((1, wide)), const((1, wide)),
                  const(cum_f.shape), const(mk_f.shape), const(cum_b.shape), const(mk_b.shape)],
        out_specs=[pl.BlockSpec((HG_C, wide), lambda b, s: (cf(b, s), 0)),
                   pl.BlockSpec((HG_C, wide), lambda b, s: (cb(b, s), 0))],
        out_shape=[jax.ShapeDtypeStruct((TA, wide), BF16), jax.ShapeDtypeStruct((TA, wide), BF16)],
        scratch_shapes=[pltpu.VMEM((H, HW, HW), F32), pltpu.VMEM((H, HW, HW), F32),
                        pltpu.VMEM((2 * HG_UNROLL, HG_C, HW), F32)],
        compiler_params=_cparams(("arbitrary", "arbitrary")),
        name="hgrn",
    )(p, p, fraw, p, p, fraw, lb_f, lb_b, cum_f, mk_f, cum_b, mk_b)


def _pack_rows(val, dst_ref, row0, rows):
    for s in range(PACK_W):
        lo = val[:, 2 * s * LANES:(2 * s + 1) * LANES]
        hi = val[:, (2 * s + 1) * LANES:(2 * s + 2) * LANES]
        w = pltpu.pack_elementwise([lo, hi], packed_dtype=BF16)
        dst_ref[pl.ds(row0 * PACK_W + s, rows, stride=PACK_W), :] = w


def _unpack_word(w):
    lo = pltpu.unpack_elementwise(w, index=0, packed_dtype=BF16, unpacked_dtype=F32)
    hi = pltpu.unpack_elementwise(w, index=1, packed_dtype=BF16, unpacked_dtype=F32)
    return lo, hi


def _mixout_kernel(att_ref, of_ref, ob_ref, hg_ref, gn_ref, wa_ref, wr_ref, x_ref, gt_ref, shf_ref, scf_ref,
                   lnp_ref, lnf_ref, rw_ref, rb_ref,
                   x1_ref, h2w_ref, ids_ref, gates_ref, cnt_ref, carry_ref):
    first = (pl.program_id(0) == 0) & (pl.program_id(1) == 0)

    @pl.when(first)
    def _():
        carry_ref[...] = jnp.zeros_like(carry_ref)

    o = of_ref[...].astype(F32) + ob_ref[...].astype(F32)
    gn = gn_ref[...]
    recs = []
    for h in range(H):
        oh = o[:, h * HW:(h + 1) * HW]
        gate = hg_ref[:, h * HW:(h + 1) * HW].astype(F32)
        recs.append((_rms(oh, gn) * (gate * _sigmoid(gate))).astype(BF16))
    rec = jnp.concatenate(recs, axis=-1)
    mix = (jnp.dot(att_ref[...], wa_ref[...], preferred_element_type=F32)
           + jnp.dot(rec, wr_ref[...], preferred_element_type=F32))
    x1 = x_ref[0] + gt_ref[0] * _rms(mix, lnp_ref[...])
    x1_ref[0] = x1
    h2 = _rms(x1, lnf_ref[...]) * (1.0 + scf_ref[0]) + shf_ref[0]
    _pack_rows(h2, h2w_ref, 0, ROW_T)

    h_hi = h2.astype(BF16)
    h_lo = (h2 - h_hi.astype(F32)).astype(BF16)
    hh = jnp.dot(h_hi, rw_ref[...], preferred_element_type=F32)
    logits = (hh[:, :LANES] + hh[:, LANES:] + jnp.dot(h_lo, rw_ref[:, :LANES], preferred_element_type=F32)
              + rb_ref[...])
    lane = lax.broadcasted_iota(I32, logits.shape, 1)
    neg = jnp.float32(-jnp.inf)
    lg = jnp.where(lane < NE, logits, neg)
    tops, idxs = [], []
    for _ in range(TOPK):
        m = jnp.max(lg, axis=-1, keepdims=True)
        idx = jnp.min(jnp.where(lg == m, lane, LANES), axis=-1, keepdims=True)
        tops.append(m)
        idxs.append(idx)
        lg = jnp.where(lane == idx, neg, lg)
    exps = [jnp.exp(tk - tops[0]) for tk in tops]
    inv = 1.0 / (exps[0] + exps[1] + exps[2] + exps[3])
    onehot = jnp.zeros(logits.shape, F32)
    for idx in idxs:
        onehot = onehot + jnp.where(lane == idx, 1.0, 0.0)
    r = lax.broadcasted_iota(I32, (ROW_T, ROW_T), 0)
    cidx = lax.broadcasted_iota(I32, (ROW_T, ROW_T), 1)
    lower = jnp.where(cidx < r, 1.0, 0.0).astype(BF16)
    prefix = jnp.dot(lower, onehot.astype(BF16), preferred_element_type=F32) + carry_ref[...]
    ids = jnp.zeros(logits.shape, I32)
    gates = jnp.zeros(logits.shape, F32)
    for kk in range(TOPK):
        rank = jnp.sum(jnp.where(lane == idxs[kk], prefix, 0.0), axis=-1, keepdims=True).astype(I32)
        ids = jnp.where(lane == kk, idxs[kk], ids)
        ids = jnp.where(lane == TOPK + kk, rank, ids)
        gates = jnp.where(lane == kk, exps[kk] * inv, gates)
    ids_ref[...] = ids
    gates_ref[...] = gates
    carry = carry_ref[...] + jnp.sum(onehot, axis=0, keepdims=True)
    carry_ref[...] = carry
    cnt_ref[...] = carry


def _mixout_call(att, o_f, o_b, p, g_norm_t, w_a, w_r, x, mod3, ln_post, ln_ffn, r_w, r_b):
    wide = H * HW
    lat = lambda b, i: b * TILES_A + 1 + i
    tok = lambda b, i: b * TILES_L + i
    const = lambda shape: pl.BlockSpec(shape, lambda b, i: tuple(0 for _ in shape))
    return pl.pallas_call(
        _mixout_kernel,
        grid=(B, TILES_L),
        in_specs=[pl.BlockSpec((ROW_T, wide), lambda b, i: (tok(b, i), 0)),
                  pl.BlockSpec((ROW_T, wide), lambda b, i: (lat(b, i), 0)),
                  pl.BlockSpec((ROW_T, wide), lambda b, i: (lat(b, i), 0)),
                  pl.BlockSpec((ROW_T, wide), lambda b, i: (lat(b, i), 3)),
                  const((1, HW)), const((wide, D)), const((wide, D)),
                  pl.BlockSpec((1, ROW_T, D), lambda b, i: (b, i, 0)),
                  pl.BlockSpec((1, 1, D), lambda b, i: (b, 0, 2)),
                  pl.BlockSpec((1, 1, D), lambda b, i: (b, 0, 3)),
                  pl.BlockSpec((1, 1, D), lambda b, i: (b, 0, 4)),
                  const((1, D)), const((1, D)), const((D, 2 * LANES)), const((1, LANES))],
        out_specs=[pl.BlockSpec((1, ROW_T, D), lambda b, i: (b, i, 0)),
                   pl.BlockSpec((ROW_T * PACK_W, LANES), lambda b, i: (tok(b, i), 0)),
                   pl.BlockSpec((ROW_T, LANES), lambda b, i: (tok(b, i), 0)),
                   pl.BlockSpec((ROW_T, LANES), lambda b, i: (tok(b, i), 0)),
                   const((1, LANES))],
        out_shape=[jax.ShapeDtypeStruct((B, N, D), F32),
                   jax.ShapeDtypeStruct((T * PACK_W, LANES), I32),
                   jax.ShapeDtypeStruct((T, LANES), I32),
                   jax.ShapeDtypeStruct((T, LANES), F32),
                   jax.ShapeDtypeStruct((1, LANES), F32)],
        scratch_shapes=[pltpu.VMEM((1, LANES), F32)],
        compiler_params=_cparams(("arbitrary", "arbitrary")),
        name="mixout_router",
    )(att, o_f, o_b, p, g_norm_t, w_a, w_r, x, mod3, mod3, mod3, ln_post, ln_ffn, r_w, r_b)


def _dispatch_copy(src_vmem, dst_hbm, sem, src_row, dst_row, rows):
    return pltpu.make_async_copy(src_vmem.at[pl.ds(src_row * PACK_W, rows * PACK_W)],
                                 dst_hbm.at[pl.ds(dst_row * PACK_W, rows * PACK_W)], sem)


def _zero_copy(zeros_s, dst_hbm, sem, dst_row, rows):
    return pltpu.make_async_copy(zeros_s.at[pl.ds(0, rows * PACK_W)],
                                 dst_hbm.at[pl.ds(dst_row * PACK_W, rows * PACK_W)], sem)


def _dispatch_kernel(dest_ref, pad_ref, h2w_ref, xs_hbm, zeros_s, sem, zsem):
    step = pl.program_id(0)
    base = step * DISP_T

    def issue(t, carry):
        for kk in range(TOPK):
            _dispatch_copy(h2w_ref, xs_hbm, sem, t, dest_ref[(base + t) * TOPK + kk], 1).start()
        return carry

    tail0 = pad_ref[2 * NE]
    tail_n = pad_ref[2 * NE + 1]

    def zero_rows(wait):
        def expert_pad(e, carry):
            def one(r, c):
                cp = _zero_copy(zeros_s, xs_hbm, zsem, pad_ref[e] + r, 1)
                cp.wait() if wait else cp.start()
                return c
            lax.fori_loop(0, pad_ref[NE + e], one, 0)
            return carry

        def tail(i, c):
            cp = _zero_copy(zeros_s, xs_hbm, zsem, (tail0 + i) * SUB, SUB)
            cp.wait() if wait else cp.start()
            return c

        lax.fori_loop(0, NE, expert_pad, 0)
        lax.fori_loop(0, tail_n, tail, 0)

    @pl.when(step == 0)
    def _():
        zeros_s[...] = jnp.zeros_like(zeros_s)
        zero_rows(wait=False)

    lax.fori_loop(0, DISP_T, issue, 0)

    @pl.when(step == 0)
    def _():
        zero_rows(wait=True)

    for _ in range(TOPK):
        _dispatch_copy(h2w_ref, xs_hbm, sem, 0, 0, DISP_T).wait()


def _dispatch_call(dest, pad, h2w):
    gs = pltpu.PrefetchScalarGridSpec(
        num_scalar_prefetch=2, grid=(T // DISP_T,),
        in_specs=[pl.BlockSpec((DISP_T * PACK_W, LANES), lambda i, d, z: (i, 0))],
        out_specs=pl.BlockSpec(memory_space=pl.ANY),
        scratch_shapes=[pltpu.VMEM((SUB * PACK_W, LANES), I32),
                        pltpu.SemaphoreType.DMA(()),
                        pltpu.SemaphoreType.DMA(())])
    return pl.pallas_call(
        _dispatch_kernel, grid_spec=gs,
        out_shape=jax.ShapeDtypeStruct((N_SUBS * SUB * PACK_W, LANES), I32),
        compiler_params=_cparams(("arbitrary",)),
        name="moe_dispatch",
    )(dest, pad, h2w)


def _sub_in_copy(xs_hbm, stage, sems, blk, slot):
    return pltpu.make_async_copy(xs_hbm.at[pl.ds(blk * (SUB * PACK_W), SUB * PACK_W)], stage.at[slot], sems.at[slot])


def _sub_out_copy(stage, out_hbm, sems, blk, slot):
    return pltpu.make_async_copy(stage.at[slot], out_hbm.at[pl.ds(blk * (SUB * PACK_W), SUB * PACK_W)], sems.at[slot])


def _for_row_chunks(nblk, fn):
    pair = 2 * CHUNK_SUBS

    @pl.when(nblk >= pair)
    def _():
        fn(0, CHUNK_SUBS * SUB)
        fn(CHUNK_SUBS * SUB, CHUNK_SUBS * SUB)

    for c in range(-(-SB_SUBS // CHUNK_SUBS)):
        row0 = c * CHUNK_SUBS * SUB
        most = min(CHUNK_SUBS, SB_SUBS - c * CHUNK_SUBS)
        for r in range(1, most + 1):
            full = r == CHUNK_SUBS
            cond = (nblk >= c * CHUNK_SUBS + r) if full else (nblk == c * CHUNK_SUBS + r)
            if c < 2:
                cond = cond & (nblk < pair)
            pl.when(cond)(functools.partial(fn, row0, r * SUB))


def _wtile_copy(w_hbm, buf, sems, e, col_tile, slot, which):
    return pltpu.make_async_copy(w_hbm.at[e, :, pl.ds(pl.multiple_of(col_tile * FF_T, FF_T), FF_T)],
                                 buf.at[slot, which], sems.at[slot, which])


def _expert_kernel(sbe_ref, sbn_ref, sbb_ref, xs_hbm, wgu_hbm, bgu_ref, wdn_hbm, bdn_ref,
                   out_hbm, x_s, act_s, wbuf, stage_in, stage_out,
                   sem_w, sem_in, sem_out):
    s = pl.program_id(0)
    nblk = sbn_ref[s]
    blk0 = sbb_ref[s]
    e = sbe_ref[s]

    def gate_tiles(j, slot):
        return (_wtile_copy(wgu_hbm, wbuf, sem_w, e, j, slot, 0),
                _wtile_copy(wgu_hbm, wbuf, sem_w, e, N1 + j, slot, 1))

    def down_tile(j, slot):
        return _wtile_copy(wdn_hbm, wbuf, sem_w, e, j, slot, 0)

    @pl.when(nblk > 0)
    def _():
        for cp in gate_tiles(0, 0):
            cp.start()

        _sub_in_copy(xs_hbm, stage_in, sem_in, blk0, 0).start()
        for i in range(SB_SUBS):
            @pl.when(i < nblk)
            def _():
                if i + 1 < SB_SUBS:
                    @pl.when(i + 1 < nblk)
                    def _():
                        _sub_in_copy(xs_hbm, stage_in, sem_in, blk0 + i + 1, (i + 1) % 2).start()
                _sub_in_copy(xs_hbm, stage_in, sem_in, blk0 + i, i % 2).wait()
                for w in range(PACK_W):
                    lo, hi = _unpack_word(stage_in[i % 2, pl.ds(w, SUB, stride=PACK_W), :])
                    x_s[i * SUB:(i + 1) * SUB, 2 * w * LANES:(2 * w + 1) * LANES] = lo.astype(BF16)
                    x_s[i * SUB:(i + 1) * SUB, (2 * w + 1) * LANES:(2 * w + 2) * LANES] = hi.astype(BF16)

        def hidden_step(j, carry):
            slot = j % 2
            for cp in gate_tiles(j, slot):
                cp.wait()

            @pl.when(j + 1 < N1)
            def _():
                for cp in gate_tiles(j + 1, 1 - slot):
                    cp.start()

            @pl.when(j + 1 == N1)
            def _():
                down_tile(0, 1 - slot).start()

            col = pl.ds(pl.multiple_of(j * FF_T, FF_T), FF_T)
            bg = bgu_ref[0, :, col]
            bl = bgu_ref[0, :, pl.ds(pl.multiple_of(DFF + j * FF_T, FF_T), FF_T)]

            def gate_up(row0, rows):
                x = x_s[row0:row0 + rows, :]
                gu = jnp.dot(x, wbuf[slot, 0].astype(BF16), preferred_element_type=F32) + bg
                li = jnp.dot(x, wbuf[slot, 1].astype(BF16), preferred_element_type=F32) + bl
                glu = jnp.minimum(gu, LIMIT)
                lin = jnp.clip(li, -LIMIT, LIMIT)
                act = glu * _sigmoid(ALPHA * glu) * (lin + 1.0)
                act_s[row0:row0 + rows, col] = act.astype(BF16)

            _for_row_chunks(nblk, gate_up)
            return carry

        lax.fori_loop(0, N1, hidden_step, 0)

        def out_step(j2, carry):
            slot = (N1 + j2) % 2
            down_tile(j2, slot).wait()

            @pl.when(j2 + 1 < N2)
            def _():
                down_tile(j2 + 1, 1 - slot).start()

            bd = bdn_ref[0, :, pl.ds(pl.multiple_of(j2 * FF_T, FF_T), FF_T)]

            def down(row0, rows):
                acc = jnp.dot(act_s[row0:row0 + rows, :], wbuf[slot, 0].astype(BF16), preferred_element_type=F32) + bd
                x_s[row0:row0 + rows, pl.ds(pl.multiple_of(j2 * FF_T, FF_T), FF_T)] = acc.astype(BF16)

            _for_row_chunks(nblk, down)
            return carry

        lax.fori_loop(0, N2, out_step, 0)

        for i in range(SB_SUBS):
            @pl.when(i < nblk)
            def _():
                if i >= 2:
                    _sub_out_copy(stage_out, out_hbm, sem_out, blk0 + i - 2, i % 2).wait()
                _pack_rows(x_s[i * SUB:(i + 1) * SUB, :].astype(F32), stage_out.at[i % 2], 0, SUB)
                _sub_out_copy(stage_out, out_hbm, sem_out, blk0 + i, i % 2).start()
        for i in range(SB_SUBS):
            @pl.when((i < nblk) & (i + 2 >= nblk))
            def _():
                _sub_out_copy(stage_out, out_hbm, sem_out, blk0 + i, i % 2).wait()


def _expert_call(sb_e, sb_n, sb_b, xs, w_gu, b_gu, w_dn, b_dn):
    gs = pltpu.PrefetchScalarGridSpec(
        num_scalar_prefetch=3, grid=(S_MAX,),
        in_specs=[pl.BlockSpec(memory_space=pl.ANY),
                  pl.BlockSpec(memory_space=pl.ANY),
                  pl.BlockSpec((1, 1, 2 * DFF), lambda s, e, n, b: (e[s], 0, 0)),
                  pl.BlockSpec(memory_space=pl.ANY),
                  pl.BlockSpec((1, 1, D), lambda s, e, n, b: (e[s], 0, 0))],
        out_specs=pl.BlockSpec(memory_space=pl.ANY),
        scratch_shapes=[pltpu.VMEM((SB_R, D), BF16),
                        pltpu.VMEM((SB_R, DFF), BF16),
                        pltpu.VMEM((2, 2, D, FF_T), F32),
                        pltpu.VMEM((2, SUB * PACK_W, LANES), I32),
                        pltpu.VMEM((2, SUB * PACK_W, LANES), I32),
                        pltpu.SemaphoreType.DMA((2, 2)),
                        pltpu.SemaphoreType.DMA((2,)),
                        pltpu.SemaphoreType.DMA((2,))])
    return pl.pallas_call(
        _expert_kernel, grid_spec=gs,
        out_shape=jax.ShapeDtypeStruct(xs.shape, xs.dtype),
        input_output_aliases={3: 0},
        compiler_params=_cparams(("arbitrary",)),
        name="moe_experts",
    )(sb_e, sb_n, sb_b, xs, w_gu, b_gu, w_dn, b_dn)


def _combine_copy(src_hbm, buf, sem, src_row, dst_row, rows):
    return pltpu.make_async_copy(src_hbm.at[pl.ds(src_row * PACK_W, rows * PACK_W)],
                                 buf.at[pl.ds(dst_row * PACK_W, rows * PACK_W)], sem)


def _combine_kernel(dest_ref, ys_hbm, gates_ref, x1_ref, gt_ref, ln_ref, o_ref, bufs, y_s, sems):
    step = pl.program_id(0) * TILES_L + pl.program_id(1)

    def issue(tile, slot):
        def body(tg, carry):
            for tt in range(COMBINE_UNROLL):
                t = tg * COMBINE_UNROLL + tt
                for kk in range(TOPK):
                    _combine_copy(ys_hbm, bufs.at[slot], sems.at[slot], dest_ref[(tile * ROW_T + t) * TOPK + kk],
                                  kk * ROW_T + t, 1).start()
            return carry
        lax.fori_loop(0, ROW_T // COMBINE_UNROLL, body, 0)

    @pl.when(step == 0)
    def _():
        issue(0, 0)

    @pl.when(step + 1 < B * TILES_L)
    def _():
        issue(step + 1, (step + 1) % 2)

    slot = step % 2
    buf = bufs.at[slot]
    _combine_copy(ys_hbm, buf, sems.at[slot], 0, 0, ROW_T * TOPK).wait()

    gates = gates_ref[...]
    gk = [jnp.broadcast_to(gates[:, kk:kk + 1], (ROW_T, LANES)) for kk in range(TOPK)]
    for w in range(PACK_W):
        acc_lo = jnp.zeros((ROW_T, LANES), F32)
        acc_hi = jnp.zeros((ROW_T, LANES), F32)
        for kk in range(TOPK):
            lo, hi = _unpack_word(buf[pl.ds(kk * ROW_T * PACK_W + w, ROW_T, stride=PACK_W), :])
            acc_lo = acc_lo + gk[kk] * lo
            acc_hi = acc_hi + gk[kk] * hi
        y_s[:, 2 * w * LANES:(2 * w + 1) * LANES] = acc_lo
        y_s[:, (2 * w + 1) * LANES:(2 * w + 2) * LANES] = acc_hi
    o_ref[0] = x1_ref[0] + gt_ref[0] * _rms(y_s[...], ln_ref[...])


def _combine_call(dest, ys, gates, x1, mod3, ln_post):
    tok = lambda b, i: b * TILES_L + i
    gs = pltpu.PrefetchScalarGridSpec(
        num_scalar_prefetch=1, grid=(B, TILES_L),
        in_specs=[pl.BlockSpec(memory_space=pl.ANY),
                  pl.BlockSpec((ROW_T, LANES), lambda b, i, d: (tok(b, i), 0)),
                  pl.BlockSpec((1, ROW_T, D), lambda b, i, d: (b, i, 0)),
                  pl.BlockSpec((1, 1, D), lambda b, i, d: (b, 0, 5)),
                  pl.BlockSpec((1, D), lambda b, i, d: (0, 0))],
        out_specs=pl.BlockSpec((1, ROW_T, D), lambda b, i, d: (b, i, 0)),
        scratch_shapes=[pltpu.VMEM((2, TOPK * ROW_T * PACK_W, LANES), I32),
                        pltpu.VMEM((ROW_T, D), F32),
                        pltpu.SemaphoreType.DMA((2,))])
    return pl.pallas_call(
        _combine_kernel, grid_spec=gs,
        out_shape=jax.ShapeDtypeStruct((B, N, D), F32),
        compiler_params=_cparams(("arbitrary", "arbitrary")),
        name="moe_combine",
    )(dest, ys, gates, x1, mod3, ln_post)


def _rot_cols(w):
    half = ROPE // 2
    return jnp.concatenate([-w[..., half:], w[..., :half]], axis=-1)


def _rope_tables():
    t = np.arange(N)
    r = (t // GRID_W).astype(np.float64)
    cl = (t % GRID_W).astype(np.float64)
    n_freq = ROPE // 4
    inv = ROPE_THETA ** (-np.arange(n_freq, dtype=np.float64) / n_freq)
    ang = np.concatenate([r[:, None] * inv, cl[:, None] * inv], axis=-1)
    cos, sin = np.cos(ang), np.sin(ang)
    cs = np.concatenate([cos, cos, sin, sin], axis=-1).astype(np.float32)
    ctx = np.concatenate([np.ones((M, ROPE), np.float32), np.zeros((M, ROPE), np.float32)], axis=-1)
    return jnp.asarray(np.concatenate([ctx, cs], axis=0)), jnp.asarray(cs * QK_SCALE)


def _routing_tables(ids, counts_f):
    e = ids[:, :TOPK]
    rank = ids[:, TOPK:2 * TOPK]
    counts = counts_f[0, :NE].astype(I32)
    nsub = (counts + SUB - 1) // SUB
    sub_start = jnp.cumsum(nsub) - nsub
    onehot = e[..., None] == jnp.arange(NE, dtype=I32)
    dest = jnp.sum(jnp.where(onehot, sub_start * SUB, 0), axis=-1) + rank
    nsb = (nsub + SB_SUBS - 1) // SB_SUBS
    sb_end = jnp.cumsum(nsb)
    sb_start = sb_end - nsb
    n_active = sb_end[-1]
    s_idx = jnp.arange(S_MAX, dtype=I32)
    s_eff = jnp.minimum(s_idx, n_active - 1)
    sb_e = jnp.minimum(jnp.searchsorted(sb_end, s_eff, side="right"), NE - 1).astype(I32)
    local = s_eff - sb_start[sb_e]
    sb_b = (sub_start[sb_e] + local * SB_SUBS).astype(I32)
    sb_n = jnp.where(s_idx < n_active, jnp.clip(nsub[sb_e] - local * SB_SUBS, 0, SB_SUBS), 0).astype(I32)
    pad = jnp.concatenate([sub_start * SUB + counts, nsub * SUB - counts,
                           jnp.stack([jnp.sum(nsub), N_SUBS - jnp.sum(nsub)])]).astype(I32)
    return dest.reshape(-1).astype(I32), pad, sb_e, sb_n, sb_b


def kernel(x, c, ctx, c_ctx, w_mod, b_mod, ln_mix_pre, ln_mix_post, ln_ffn_pre, ln_ffn_post, w_in, q_norm, kv_norm,
           w_uq, w_ukv, hgrn_lb, g_norm, w_out, router_w, router_b, w_gate_up, b_gate_up, w_down, b_down):
    c_all = jnp.concatenate([c, c_ctx[None, :], jnp.zeros((8 - B - 1, D), F32)], axis=0)
    mod = _mod_call(c_all, w_mod[0], b_mod)
    mod3 = mod.reshape(8, 1, 6 * D)

    w = w_in[0]
    o_q, o_kv, o_kr = 0, Q_RANK, Q_RANK + KV_RANK
    o_hq = o_kr + ROPE
    wide = H * HW
    o_ff, o_fb, o_hi, o_hg = o_hq + wide, o_hq + 2 * wide, o_hq + 3 * wide, o_hq + 4 * wide
    w_kr = w[:, o_kr:o_kr + ROPE]
    w_p = jnp.concatenate([w[:, o_q:o_kr], w_kr, _rot_cols(w_kr), jnp.zeros((D, LANES), F32),
                           w[:, o_hq:o_hq + wide], w[:, o_hi:o_hi + wide], w[:, o_hg:o_hg + wide]],
                          axis=1).astype(BF16)
    w_f = w[:, o_ff:o_ff + 2 * wide].astype(BF16)
    wq = w_uq[0]
    wq_r = wq[..., HW:]
    w_q = jnp.concatenate([wq[..., :HW], wq_r, _rot_cols(wq_r)], axis=-1).reshape(Q_RANK, 2 * wide).astype(BF16)
    wkv = w_ukv[0]
    w_kv = jnp.concatenate([wkv[..., :HW].reshape(KV_RANK, wide), wkv[..., HW:].reshape(KV_RANK, wide)],
                           axis=1).astype(BF16)
    w_o = w_out[0].astype(BF16)
    cs_all, cs_q = _rope_tables()
    lb = jnp.cumsum(jax.nn.softmax(hgrn_lb.astype(F32), axis=1), axis=1)
    r_w = jnp.concatenate([router_w[0], jnp.zeros((D, LANES - NE), F32)], axis=1)
    r_w_hi = r_w.astype(BF16)
    r_w = jnp.concatenate([r_w_hi, (r_w - r_w_hi.astype(F32)).astype(BF16)], axis=1)
    r_b = jnp.concatenate([router_b[0], jnp.zeros((LANES - NE,), F32)])[None, :]

    h_all = _normmod_call(x, ctx, mod3, ln_mix_pre)
    h_flat = h_all.reshape(TA, D)
    p = _matmul_call(h_flat, w_p, BF16, "in_proj")
    fraw = _matmul_call(h_flat, w_f, F32, "in_proj_forget")
    k_all, v_all = _kvprep_call(p, kv_norm, w_kv, cs_all)
    q_all = _qprep_call(p, q_norm, w_q, cs_q)
    att = _attn_call(q_all, k_all, v_all)
    o_f, o_b = _hgrn_call(p, fraw, lb[0, 0][None, :], lb[1, 0][None, :])

    g_norm_t = g_norm
    x1, h2w, ids, gates, counts = _mixout_call(att, o_f, o_b, p, g_norm_t, w_o[:wide], w_o[wide:], x, mod3,
                                               ln_mix_post, ln_ffn_pre, r_w, r_b)

    dest, pad, sb_e, sb_n, sb_b = _routing_tables(ids, counts)
    xs = _dispatch_call(dest, pad, h2w)
    ys = _expert_call(sb_e, sb_n, sb_b, xs, w_gate_up[0], b_gate_up[0][:, None, :], w_down[0], b_down[0][:, None, :])
    return _combine_call(dest, ys, gates, x1, mod3, ln_ffn_post)
```

```python
import functools

import numpy as np
import jax
import jax.numpy as jnp
from jax import lax
from jax.experimental import pallas as pl
from jax.experimental.pallas import tpu as pltpu

F32 = jnp.float32
BF16 = jnp.bfloat16
I32 = jnp.int32

D = 2048
B = 4
N = 4096
M = 256
NA = N + M
T = B * N
TA = B * NA
EPS = 1e-6
H = 8
HW = 128
Q_RANK = 512
KV_RANK = 256
ROPE = 64
QK_SCALE = float((128 + ROPE) ** -0.5)
GRID_W = 64
ROPE_THETA = 10000.0
NE = 32
TOPK = 4
DFF = 2048
LIMIT = 7.0
ALPHA = 1.702
LOG2E = float(np.log2(np.e))

LANES = 128
SUBLANES = 8
VMEM_LIMIT = 56 * 1024 * 1024

ROW_T = 256
TILES_A = NA // ROW_T
TILES_L = N // ROW_T
MM_TM = 1024
MM_TN = 1024
MOD_ROWS = SUBLANES
HG_C = 256
HG_STEPS = NA // HG_C
HG_CTX = M // HG_C
HG_LEVELS = tuple(HG_C >> i for i in range(1, HG_C.bit_length()))
SUB = 256
N_SUBS = T * TOPK // SUB + NE
SB_SUBS = 14
SB_R = SB_SUBS * SUB
CHUNK_SUBS = 4
S_MAX = -(-N_SUBS // SB_SUBS) + NE
FF_T = 256
N1 = DFF // FF_T
N2 = D // FF_T
PACK_W = D // 2 // LANES
DISP_T = 512
COMBINE_UNROLL = 4


def _sigmoid(x):
    return 1.0 / (1.0 + jnp.exp(-x))


def _rms(x, w):
    ms = jnp.mean(x * x, axis=-1, keepdims=True)
    return x * lax.rsqrt(ms + EPS) * w


def _cparams(sem, vmem=VMEM_LIMIT):
    return pltpu.CompilerParams(dimension_semantics=sem, vmem_limit_bytes=vmem)


def _mod_kernel(c_ref, w_ref, b_ref, o_ref):
    c = c_ref[...]
    a = (c * _sigmoid(c)).astype(BF16)
    o_ref[...] = jnp.dot(a, w_ref[...].astype(BF16), preferred_element_type=F32) + b_ref[...]


def _mod_call(c_all, w_mod, b_mod):
    tn = MM_TN
    return pl.pallas_call(
        _mod_kernel,
        grid=(6 * D // tn,),
        in_specs=[pl.BlockSpec((MOD_ROWS, D), lambda j: (0, 0)),
                  pl.BlockSpec((D, tn), lambda j: (0, j)),
                  pl.BlockSpec((1, tn), lambda j: (0, j))],
        out_specs=pl.BlockSpec((MOD_ROWS, tn), lambda j: (0, j)),
        out_shape=jax.ShapeDtypeStruct((MOD_ROWS, 6 * D), F32),
        compiler_params=_cparams(("arbitrary",)),
        name="mod",
    )(c_all, w_mod, b_mod)


def _normmod_kernel(x_ref, ctx_ref, sh_ref, sc_ref, w_ref, o_ref):
    def go(src):
        y = _rms(src, w_ref[...])
        o_ref[0] = (y * (1.0 + sc_ref[0]) + sh_ref[0]).astype(BF16)

    i = pl.program_id(1)

    @pl.when(i == 0)
    def _():
        go(ctx_ref[0])

    @pl.when(i > 0)
    def _():
        go(x_ref[0])


def _normmod_call(x, ctx, mod3, ln_w):
    row = lambda b, i: jnp.where(i == 0, B, b)
    return pl.pallas_call(
        _normmod_kernel,
        grid=(B, TILES_A),
        in_specs=[pl.BlockSpec((1, ROW_T, D), lambda b, i: (b, jnp.maximum(i - 1, 0), 0)),
                  pl.BlockSpec((1, ROW_T, D), lambda b, i: (b, 0, 0)),
                  pl.BlockSpec((1, 1, D), lambda b, i: (row(b, i), 0, 0)),
                  pl.BlockSpec((1, 1, D), lambda b, i: (row(b, i), 0, 1)),
                  pl.BlockSpec((1, D), lambda b, i: (0, 0))],
        out_specs=pl.BlockSpec((1, ROW_T, D), lambda b, i: (b, i, 0)),
        out_shape=jax.ShapeDtypeStruct((B, NA, D), BF16),
        compiler_params=_cparams(("arbitrary", "arbitrary")),
        name="normmod",
    )(x, ctx, mod3, mod3, ln_w)


def _matmul_kernel(a_ref, w_ref, o_ref):
    o_ref[...] = jnp.dot(a_ref[...], w_ref[...], preferred_element_type=F32).astype(o_ref.dtype)


def _matmul_call(a, w, out_dtype, name):
    rows, k = a.shape
    cols = w.shape[1]
    return pl.pallas_call(
        _matmul_kernel,
        grid=(rows // MM_TM, cols // MM_TN),
        in_specs=[pl.BlockSpec((MM_TM, k), lambda i, j: (i, 0)),
                  pl.BlockSpec((k, MM_TN), lambda i, j: (0, j))],
        out_specs=pl.BlockSpec((MM_TM, MM_TN), lambda i, j: (i, j)),
        out_shape=jax.ShapeDtypeStruct((rows, cols), out_dtype),
        compiler_params=_cparams(("arbitrary", "arbitrary")),
        name=name,
    )(a, w)


def _kvprep_kernel(p_ref, kvn_ref, w_ref, cs_ref, k_ref, v_ref):
    ckv = p_ref[:, Q_RANK:Q_RANK + KV_RANK].astype(F32)
    ckvn = _rms(ckv, kvn_ref[...]).astype(BF16)
    kv = jnp.dot(ckvn, w_ref[...], preferred_element_type=F32)
    a = p_ref[:, Q_RANK + KV_RANK:Q_RANK + KV_RANK + LANES].astype(F32) * cs_ref[...]
    r = a + pltpu.roll(a, ROPE, axis=1)
    lane = lax.broadcasted_iota(I32, r.shape, 1)
    r = jnp.where(lane < ROPE, r, 0.0).astype(BF16)
    for h in range(H):
        k_ref[0, h, :, 0:HW] = kv[:, h * HW:(h + 1) * HW].astype(BF16)
        k_ref[0, h, :, HW:2 * HW] = r
        v_ref[0, h] = kv[:, (H + h) * HW:(H + h + 1) * HW].astype(BF16)


def _kvprep_call(p, kv_norm, w_kv, cs_all):
    return pl.pallas_call(
        _kvprep_kernel,
        grid=(B, TILES_A),
        in_specs=[pl.BlockSpec((ROW_T, MM_TN), lambda b, i: (b * TILES_A + i, 0)),
                  pl.BlockSpec((1, KV_RANK), lambda b, i: (0, 0)),
                  pl.BlockSpec((KV_RANK, 2 * H * HW), lambda b, i: (0, 0)),
                  pl.BlockSpec((ROW_T, LANES), lambda b, i: (i, 0))],
        out_specs=[pl.BlockSpec((1, H, ROW_T, 2 * HW), lambda b, i: (b, 0, i, 0)),
                   pl.BlockSpec((1, H, ROW_T, HW), lambda b, i: (b, 0, i, 0))],
        out_shape=[jax.ShapeDtypeStruct((B, H, NA, 2 * HW), BF16),
                   jax.ShapeDtypeStruct((B, H, NA, HW), BF16)],
        compiler_params=_cparams(("arbitrary", "arbitrary")),
        name="kvprep",
    )(p, kv_norm, w_kv, cs_all)


def _qprep_kernel(p_ref, qn_ref, w_ref, cs_ref, q_ref):
    cq = p_ref[:, 0:Q_RANK].astype(F32)
    cqn = _rms(cq, qn_ref[...]).astype(BF16)
    q = jnp.dot(cqn, w_ref[...], preferred_element_type=F32)
    cs = cs_ref[...]
    for h in range(H):
        q_ref[0, h, :, 0:HW] = (q[:, 2 * h * HW:(2 * h + 1) * HW] * QK_SCALE).astype(BF16)
        a = q[:, (2 * h + 1) * HW:(2 * h + 2) * HW] * cs
        q_ref[0, h, :, HW:2 * HW] = (a + pltpu.roll(a, ROPE, axis=1)).astype(BF16)


def _qprep_call(p, q_norm, w_q, cs_q):
    return pl.pallas_call(
        _qprep_kernel,
        grid=(B, TILES_L),
        in_specs=[pl.BlockSpec((ROW_T, MM_TN), lambda b, i: (b * TILES_A + 1 + i, 0)),
                  pl.BlockSpec((1, Q_RANK), lambda b, i: (0, 0)),
                  pl.BlockSpec((Q_RANK, 2 * H * HW), lambda b, i: (0, 0)),
                  pl.BlockSpec((ROW_T, LANES), lambda b, i: (i, 0))],
        out_specs=pl.BlockSpec((1, H, ROW_T, 2 * HW), lambda b, i: (b, 0, i, 0)),
        out_shape=jax.ShapeDtypeStruct((B, H, N, 2 * HW), BF16),
        compiler_params=_cparams(("arbitrary", "arbitrary")),
        name="qprep",
    )(p, q_norm, w_q, cs_q)


ATT_TQ = 256
ATT_TILES = 16


def _attn_kernel(q_ref, k_ref, v_ref, o_ref):
    for t in range(ATT_TILES):
        rows = slice(t * ATT_TQ, (t + 1) * ATT_TQ)
        s = lax.dot_general(q_ref[0, 0, rows, :], k_ref[0, 0], (((1,), (1,)), ((), ())),
                            preferred_element_type=F32)
        m = jnp.max(s, axis=-1, keepdims=True)
        p = jnp.exp(s - m)
        l = jnp.sum(p, axis=-1, keepdims=True)
        o = jnp.dot(p.astype(BF16), v_ref[0, 0], preferred_element_type=F32)
        o_ref[rows, :] = (o * (1.0 / l)).astype(BF16)


def _attn_call(q, k, v):
    tq = ATT_TQ * ATT_TILES
    return pl.pallas_call(
        _attn_kernel,
        grid=(B, H, N // tq),
        in_specs=[pl.BlockSpec((1, 1, tq, 2 * HW), lambda b, h, i: (b, h, i, 0)),
                  pl.BlockSpec((1, 1, NA, 2 * HW), lambda b, h, i: (b, h, 0, 0)),
                  pl.BlockSpec((1, 1, NA, HW), lambda b, h, i: (b, h, 0, 0))],
        out_specs=pl.BlockSpec((tq, HW), lambda b, h, i: (b * (N // tq) + i, h)),
        out_shape=jax.ShapeDtypeStruct((T, H * HW), BF16),
        compiler_params=_cparams(("arbitrary", "arbitrary", "arbitrary")),
        name="attention",
    )(q, k, v)


def _hgrn_tables(reverse):
    c = HG_C
    t = np.arange(c)[:, None]
    u = np.arange(c)[None, :]
    cum = (u <= t).astype(np.float32)
    masks = [(u == t)]
    for m in HG_LEVELS:
        blk_t = t // m
        masks.append(((blk_t % 2) == 1) & ((u // m) == blk_t - 1))
    masks = np.stack(masks).astype(np.float32)
    if reverse:
        cum = cum[::-1, ::-1]
        masks = masks[:, ::-1, ::-1]
    return jnp.asarray(cum, dtype=BF16), jnp.asarray(masks.reshape(-1, c), dtype=F32)


def _pair_boundary(c_ref, cval, m, reverse):
    def rows_of(r_of_group):
        return jnp.concatenate([jnp.broadcast_to(c_ref[r_of_group(gi):r_of_group(gi) + 1, :], (SUBLANES, HW))
                                for gi in range(HG_C // SUBLANES)], axis=0)

    if m >= 4:
        def boundary(gi):
            pair = (SUBLANES * gi // (2 * m)) * 2 * m
            return pair + m if reverse else pair + m - 1
        return rows_of(boundary)
    row = lax.broadcasted_iota(I32, (HG_C, HW), 0)
    if m == 2:
        half = SUBLANES // 2
        edge = m if reverse else m - 1
        first = rows_of(lambda gi: SUBLANES * gi + edge)
        second = rows_of(lambda gi: SUBLANES * gi + half + edge)
        return jnp.where((row & (SUBLANES - 1)) < half, first, second)
    if reverse:
        return jnp.where((row & 1) == 0, pltpu.roll(cval, HG_C - 1, 0), cval)
    return jnp.where((row & 1) == 1, pltpu.roll(cval, 1, 0), cval)


def _hgrn_head(h, hq_ref, hi_ref, fr_ref, lb_ref, cum_ref, mk_ref, s_ref, o_ref, c_ref, reverse):
    c = HG_C
    col = pl.ds(pl.multiple_of(h * HW, HW), HW)
    xq = hq_ref[:, col].astype(F32)
    q = xq * _sigmoid(xq)
    lb = lb_ref[:, col]
    f = lb + (1.0 - lb) * _sigmoid(fr_ref[:, col])
    g = jnp.log(f)
    k = 1.0 - f
    v = hi_ref[:, col]
    g_hi = g.astype(BF16)
    g_lo = (g - g_hi.astype(F32)).astype(BF16)
    cum = cum_ref[...]
    cs = jnp.dot(jnp.concatenate([cum, cum], axis=1), jnp.concatenate([g_hi, g_lo], axis=0),
                 preferred_element_type=F32) * LOG2E
    c_ref[...] = cs
    total_row = 0 if reverse else c - 1
    c_tot = c_ref[total_row:total_row + 1, :]

    nt = (((1,), (1,)), ((), ()))
    st = s_ref[h]
    o = lax.dot_general((q * jnp.exp2(cs)).astype(BF16), st.astype(BF16), nt, preferred_element_type=F32)
    qb = q.astype(BF16)
    kb = k.astype(BF16)
    sc = lax.dot_general(qb, kb, nt, preferred_element_type=F32) * mk_ref[0:c, :]
    sign = jnp.uint32(0x80000000)
    for li, m in enumerate(HG_LEVELS):
        z = cs - _pair_boundary(c_ref, cs, m, reverse)
        w = jnp.exp2(pltpu.bitcast(pltpu.bitcast(z, jnp.uint32) | sign, F32)).astype(BF16)
        sc = sc + (lax.dot_general(qb * w, kb * w, nt, preferred_element_type=F32)
                   * mk_ref[(li + 1) * c:(li + 2) * c, :])
    o = o + jnp.dot(sc.astype(BF16), v, preferred_element_type=F32)
    o_ref[:, col] = o.astype(BF16)
    kend = (k * jnp.exp2(c_tot - cs)).astype(BF16)
    s_ref[h] = st * jnp.exp2(c_tot) + lax.dot_general(v, kend, (((0,), (0,)), ((), ())), preferred_element_type=F32)


HG_UNROLL = 8


def _hgrn_kernel(hq_f, hi_f, fr_f, hq_b, hi_b, fr_b, lbf_ref, lbb_ref, cumf_ref, mkf_ref, cumb_ref, mkb_ref,
                 of_ref, ob_ref, sf_ref, sb_ref, c_s):
    @pl.when(pl.program_id(1) == 0)
    def _():
        sf_ref[...] = jnp.zeros_like(sf_ref)
        sb_ref[...] = jnp.zeros_like(sb_ref)

    def body(i, carry):
        for hh in range(HG_UNROLL):
            h = i * HG_UNROLL + hh
            _hgrn_head(h, hq_f, hi_f, fr_f, lbf_ref, cumf_ref, mkf_ref, sf_ref, of_ref, c_s.at[2 * hh], False)
            _hgrn_head(h, hq_b, hi_b, fr_b, lbb_ref, cumb_ref, mkb_ref, sb_ref, ob_ref, c_s.at[2 * hh + 1], True)
        return carry

    lax.fori_loop(0, H // HG_UNROLL, body, 0)


def _hgrn_call(p, fraw, lb_f, lb_b):
    cum_f, mk_f = _hgrn_tables(False)
    cum_b, mk_b = _hgrn_tables(True)
    wide = H * HW

    def cf(b, s):
        return b * HG_STEPS + s

    def cb(b, s):
        return b * HG_STEPS + jnp.where(s < HG_CTX, HG_CTX - 1 - s, HG_STEPS - 1 + HG_CTX - s)

    const = lambda shape: pl.BlockSpec(shape, lambda b, s: (0, 0))
    return pl.pallas_call(
        _hgrn_kernel,
        grid=(B, HG_STEPS),
        in_specs=[pl.BlockSpec((HG_C, wide), lambda b, s: (cf(b, s), 1)),
                  pl.BlockSpec((HG_C, wide), lambda b, s: (cf(b, s), 2)),
                  pl.BlockSpec((HG_C, wide), lambda b, s: (cf(b, s), 0)),
                  pl.BlockSpec((HG_C, wide), lambda b, s: (cb(b, s), 1)),
                  pl.BlockSpec((HG_C, wide), lambda b, s: (cb(b, s), 2)),
                  pl.BlockSpec((HG_C, wide), lambda b, s: (cb(b, s), 1)),
                  const((1, wide)), const((1, wide)),
                  const(cum_f.shape), const(mk_f.shape), const(cum_b.shape), const(mk_b.shape)],
        out_specs=[pl.BlockSpec((HG_C, wide), lambda b, s: (cf(b, s), 0)),
                   pl.BlockSpec((HG_C, wide), lambda b, s: (cb(b, s), 0))],
        out_shape=[jax.ShapeDtypeStruct((TA, wide), BF16), jax.ShapeDtypeStruct((TA, wide), BF16)],
        scratch_shapes=[pltpu.VMEM((H, HW, HW), F32), pltpu.VMEM((H, HW, HW), F32),
                        pltpu.VMEM((2 * HG_UNROLL, HG_C, HW), F32)],
        compiler_params=_cparams(("arbitrary", "arbitrary")),
        name="hgrn",
    )(p, p, fraw, p, p, fraw, lb_f, lb_b, cum_f, mk_f, cum_b, mk_b)


def _pack_rows(val, dst_ref, row0, rows):
    for s in range(PACK_W):
        lo = val[:, 2 * s * LANES:(2 * s + 1) * LANES]
        hi = val[:, (2 * s + 1) * LANES:(2 * s + 2) * LANES]
        w = pltpu.pack_elementwise([lo, hi], packed_dtype=BF16)
        dst_ref[pl.ds(row0 * PACK_W + s, rows, stride=PACK_W), :] = w


def _unpack_word(w):
    lo = pltpu.unpack_elementwise(w, index=0, packed_dtype=BF16, unpacked_dtype=F32)
    hi = pltpu.unpack_elementwise(w, index=1, packed_dtype=BF16, unpacked_dtype=F32)
    return lo, hi


def _mixout_kernel(att_ref, of_ref, ob_ref, hg_ref, gn_ref, wa_ref, wr_ref, x_ref, gt_ref, shf_ref, scf_ref,
                   lnp_ref, lnf_ref, rw_ref, rb_ref,
                   x1_ref, h2w_ref, ids_ref, gates_ref, cnt_ref, carry_ref):
    first = (pl.program_id(0) == 0) & (pl.program_id(1) == 0)

    @pl.when(first)
    def _():
        carry_ref[...] = jnp.zeros_like(carry_ref)

    o = of_ref[...].astype(F32) + ob_ref[...].astype(F32)
    gn = gn_ref[...]
    recs = []
    for h in range(H):
        oh = o[:, h * HW:(h + 1) * HW]
        gate = hg_ref[:, h * HW:(h + 1) * HW].astype(F32)
        recs.append((_rms(oh, gn) * (gate * _sigmoid(gate))).astype(BF16))
    rec = jnp.concatenate(recs, axis=-1)
    mix = (jnp.dot(att_ref[...], wa_ref[...], preferred_element_type=F32)
           + jnp.dot(rec, wr_ref[...], preferred_element_type=F32))
    x1 = x_ref[0] + gt_ref[0] * _rms(mix, lnp_ref[...])
    x1_ref[0] = x1
    h2 = _rms(x1, lnf_ref[...]) * (1.0 + scf_ref[0]) + shf_ref[0]
    _pack_rows(h2, h2w_ref, 0, ROW_T)

    h_hi = h2.astype(BF16)
    h_lo = (h2 - h_hi.astype(F32)).astype(BF16)
    hh = jnp.dot(h_hi, rw_ref[...], preferred_element_type=F32)
    logits = (hh[:, :LANES] + hh[:, LANES:] + jnp.dot(h_lo, rw_ref[:, :LANES], preferred_element_type=F32)
              + rb_ref[...])
    lane = lax.broadcasted_iota(I32, logits.shape, 1)
    neg = jnp.float32(-jnp.inf)
    lg = jnp.where(lane < NE, logits, neg)
    tops, idxs = [], []
    for _ in range(TOPK):
        m = jnp.max(lg, axis=-1, keepdims=True)
        idx = jnp.min(jnp.where(lg == m, lane, LANES), axis=-1, keepdims=True)
        tops.append(m)
        idxs.append(idx)
        lg = jnp.where(lane == idx, neg, lg)
    exps = [jnp.exp(tk - tops[0]) for tk in tops]
    inv = 1.0 / (exps[0] + exps[1] + exps[2] + exps[3])
    onehot = jnp.zeros(logits.shape, F32)
    for idx in idxs:
        onehot = onehot + jnp.where(lane == idx, 1.0, 0.0)
    r = lax.broadcasted_iota(I32, (ROW_T, ROW_T), 0)
    cidx = lax.broadcasted_iota(I32, (ROW_T, ROW_T), 1)
    lower = jnp.where(cidx < r, 1.0, 0.0).astype(BF16)
    prefix = jnp.dot(lower, onehot.astype(BF16), preferred_element_type=F32) + carry_ref[...]
    ids = jnp.zeros(logits.shape, I32)
    gates = jnp.zeros(logits.shape, F32)
    for kk in range(TOPK):
        rank = jnp.sum(jnp.where(lane == idxs[kk], prefix, 0.0), axis=-1, keepdims=True).astype(I32)
        ids = jnp.where(lane == kk, idxs[kk], ids)
        ids = jnp.where(lane == TOPK + kk, rank, ids)
        gates = jnp.where(lane == kk, exps[kk] * inv, gates)
    ids_ref[...] = ids
    gates_ref[...] = gates
    carry = carry_ref[...] + jnp.sum(onehot, axis=0, keepdims=True)
    carry_ref[...] = carry
    cnt_ref[...] = carry


def _mixout_call(att, o_f, o_b, p, g_norm_t, w_a, w_r, x, mod3, ln_post, ln_ffn, r_w, r_b):
    wide = H * HW
    lat = lambda b, i: b * TILES_A + 1 + i
    tok = lambda b, i: b * TILES_L + i
    const = lambda shape: pl.BlockSpec(shape, lambda b, i: tuple(0 for _ in shape))
    return pl.pallas_call(
        _mixout_kernel,
        grid=(B, TILES_L),
        in_specs=[pl.BlockSpec((ROW_T, wide), lambda b, i: (tok(b, i), 0)),
                  pl.BlockSpec((ROW_T, wide), lambda b, i: (lat(b, i), 0)),
                  pl.BlockSpec((ROW_T, wide), lambda b, i: (lat(b, i), 0)),
                  pl.BlockSpec((ROW_T, wide), lambda b, i: (lat(b, i), 3)),
                  const((1, HW)), const((wide, D)), const((wide, D)),
                  pl.BlockSpec((1, ROW_T, D), lambda b, i: (b, i, 0)),
                  pl.BlockSpec((1, 1, D), lambda b, i: (b, 0, 2)),
                  pl.BlockSpec((1, 1, D), lambda b, i: (b, 0, 3)),
                  pl.BlockSpec((1, 1, D), lambda b, i: (b, 0, 4)),
                  const((1, D)), const((1, D)), const((D, 2 * LANES)), const((1, LANES))],
        out_specs=[pl.BlockSpec((1, ROW_T, D), lambda b, i: (b, i, 0)),
                   pl.BlockSpec((ROW_T * PACK_W, LANES), lambda b, i: (tok(b, i), 0)),
                   pl.BlockSpec((ROW_T, LANES), lambda b, i: (tok(b, i), 0)),
                   pl.BlockSpec((ROW_T, LANES), lambda b, i: (tok(b, i), 0)),
                   const((1, LANES))],
        out_shape=[jax.ShapeDtypeStruct((B, N, D), F32),
                   jax.ShapeDtypeStruct((T * PACK_W, LANES), I32),
                   jax.ShapeDtypeStruct((T, LANES), I32),
                   jax.ShapeDtypeStruct((T, LANES), F32),
                   jax.ShapeDtypeStruct((1, LANES), F32)],
        scratch_shapes=[pltpu.VMEM((1, LANES), F32)],
        compiler_params=_cparams(("arbitrary", "arbitrary")),
        name="mixout_router",
    )(att, o_f, o_b, p, g_norm_t, w_a, w_r, x, mod3, mod3, mod3, ln_post, ln_ffn, r_w, r_b)


def _dispatch_copy(src_vmem, dst_hbm, sem, src_row, dst_row, rows):
    return pltpu.make_async_copy(src_vmem.at[pl.ds(src_row * PACK_W, rows * PACK_W)],
                                 dst_hbm.at[pl.ds(dst_row * PACK_W, rows * PACK_W)], sem)


def _zero_copy(zeros_s, dst_hbm, sem, dst_row, rows):
    return pltpu.make_async_copy(zeros_s.at[pl.ds(0, rows * PACK_W)],
                                 dst_hbm.at[pl.ds(dst_row * PACK_W, rows * PACK_W)], sem)


def _dispatch_kernel(dest_ref, pad_ref, h2w_ref, xs_hbm, zeros_s, sem, zsem):
    step = pl.program_id(0)
    base = step * DISP_T

    def issue(t, carry):
        for kk in range(TOPK):
            _dispatch_copy(h2w_ref, xs_hbm, sem, t, dest_ref[(base + t) * TOPK + kk], 1).start()
        return carry

    tail0 = pad_ref[2 * NE]
    tail_n = pad_ref[2 * NE + 1]

    def zero_rows(wait):
        def expert_pad(e, carry):
            def one(r, c):
                cp = _zero_copy(zeros_s, xs_hbm, zsem, pad_ref[e] + r, 1)
                cp.wait() if wait else cp.start()
                return c
            lax.fori_loop(0, pad_ref[NE + e], one, 0)
            return carry

        def tail(i, c):
            cp = _zero_copy(zeros_s, xs_hbm, zsem, (tail0 + i) * SUB, SUB)
            cp.wait() if wait else cp.start()
            return c

        lax.fori_loop(0, NE, expert_pad, 0)
        lax.fori_loop(0, tail_n, tail, 0)

    @pl.when(step == 0)
    def _():
        zeros_s[...] = jnp.zeros_like(zeros_s)
        zero_rows(wait=False)

    lax.fori_loop(0, DISP_T, issue, 0)

    @pl.when(step == 0)
    def _():
        zero_rows(wait=True)

    for _ in range(TOPK):
        _dispatch_copy(h2w_ref, xs_hbm, sem, 0, 0, DISP_T).wait()


def _dispatch_call(dest, pad, h2w):
    gs = pltpu.PrefetchScalarGridSpec(
        num_scalar_prefetch=2, grid=(T // DISP_T,),
        in_specs=[pl.BlockSpec((DISP_T * PACK_W, LANES), lambda i, d, z: (i, 0))],
        out_specs=pl.BlockSpec(memory_space=pl.ANY),
        scratch_shapes=[pltpu.VMEM((SUB * PACK_W, LANES), I32),
                        pltpu.SemaphoreType.DMA(()),
                        pltpu.SemaphoreType.DMA(())])
    return pl.pallas_call(
        _dispatch_kernel, grid_spec=gs,
        out_shape=jax.ShapeDtypeStruct((N_SUBS * SUB * PACK_W, LANES), I32),
        compiler_params=_cparams(("arbitrary",)),
        name="moe_dispatch",
    )(dest, pad, h2w)


def _sub_in_copy(xs_hbm, stage, sems, blk, slot):
    return pltpu.make_async_copy(xs_hbm.at[pl.ds(blk * (SUB * PACK_W), SUB * PACK_W)], stage.at[slot], sems.at[slot])


def _sub_out_copy(stage, out_hbm, sems, blk, slot):
    return pltpu.make_async_copy(stage.at[slot], out_hbm.at[pl.ds(blk * (SUB * PACK_W), SUB * PACK_W)], sems.at[slot])


def _for_row_chunks(nblk, fn):
    pair = 2 * CHUNK_SUBS

    @pl.when(nblk >= pair)
    def _():
        fn(0, CHUNK_SUBS * SUB)
        fn(CHUNK_SUBS * SUB, CHUNK_SUBS * SUB)

    for c in range(-(-SB_SUBS // CHUNK_SUBS)):
        row0 = c * CHUNK_SUBS * SUB
        most = min(CHUNK_SUBS, SB_SUBS - c * CHUNK_SUBS)
        for r in range(1, most + 1):
            full = r == CHUNK_SUBS
            cond = (nblk >= c * CHUNK_SUBS + r) if full else (nblk == c * CHUNK_SUBS + r)
            if c < 2:
                cond = cond & (nblk < pair)
            pl.when(cond)(functools.partial(fn, row0, r * SUB))


def _wtile_copy(w_hbm, buf, sems, e, col_tile, slot, which):
    return pltpu.make_async_copy(w_hbm.at[e, :, pl.ds(pl.multiple_of(col_tile * FF_T, FF_T), FF_T)],
                                 buf.at[slot, which], sems.at[slot, which])


def _expert_kernel(sbe_ref, sbn_ref, sbb_ref, xs_hbm, wgu_hbm, bgu_ref, wdn_hbm, bdn_ref,
                   out_hbm, x_s, act_s, wbuf, stage_in, stage_out,
                   sem_w, sem_in, sem_out):
    s = pl.program_id(0)
    nblk = sbn_ref[s]
    blk0 = sbb_ref[s]
    e = sbe_ref[s]

    def gate_tiles(j, slot):
        return (_wtile_copy(wgu_hbm, wbuf, sem_w, e, j, slot, 0),
                _wtile_copy(wgu_hbm, wbuf, sem_w, e, N1 + j, slot, 1))

    def down_tile(j, slot):
        return _wtile_copy(wdn_hbm, wbuf, sem_w, e, j, slot, 0)

    @pl.when(nblk > 0)
    def _():
        for cp in gate_tiles(0, 0):
            cp.start()

        _sub_in_copy(xs_hbm, stage_in, sem_in, blk0, 0).start()
        for i in range(SB_SUBS):
            @pl.when(i < nblk)
            def _():
                if i + 1 < SB_SUBS:
                    @pl.when(i + 1 < nblk)
                    def _():
                        _sub_in_copy(xs_hbm, stage_in, sem_in, blk0 + i + 1, (i + 1) % 2).start()
                _sub_in_copy(xs_hbm, stage_in, sem_in, blk0 + i, i % 2).wait()
                for w in range(PACK_W):
                    lo, hi = _unpack_word(stage_in[i % 2, pl.ds(w, SUB, stride=PACK_W), :])
                    x_s[i * SUB:(i + 1) * SUB, 2 * w * LANES:(2 * w + 1) * LANES] = lo.astype(BF16)
                    x_s[i * SUB:(i + 1) * SUB, (2 * w + 1) * LANES:(2 * w + 2) * LANES] = hi.astype(BF16)

        def hidden_step(j, carry):
            slot = j % 2
            for cp in gate_tiles(j, slot):
                cp.wait()

            @pl.when(j + 1 < N1)
            def _():
                for cp in gate_tiles(j + 1, 1 - slot):
                    cp.start()

            @pl.when(j + 1 == N1)
            def _():
                down_tile(0, 1 - slot).start()

            col = pl.ds(pl.multiple_of(j * FF_T, FF_T), FF_T)
            bg = bgu_ref[0, :, col]
            bl = bgu_ref[0, :, pl.ds(pl.multiple_of(DFF + j * FF_T, FF_T), FF_T)]

            def gate_up(row0, rows):
                x = x_s[row0:row0 + rows, :]
                gu = jnp.dot(x, wbuf[slot, 0].astype(BF16), preferred_element_type=F32) + bg
                li = jnp.dot(x, wbuf[slot, 1].astype(BF16), preferred_element_type=F32) + bl
                glu = jnp.minimum(gu, LIMIT)
                lin = jnp.clip(li, -LIMIT, LIMIT)
                act = glu * _sigmoid(ALPHA * glu) * (lin + 1.0)
                act_s[row0:row0 + rows, col] = act.astype(BF16)

            _for_row_chunks(nblk, gate_up)
            return carry

        lax.fori_loop(0, N1, hidden_step, 0)

        def out_step(j2, carry):
            slot = (N1 + j2) % 2
            down_tile(j2, slot).wait()

            @pl.when(j2 + 1 < N2)
            def _():
                down_tile(j2 + 1, 1 - slot).start()

            bd = bdn_ref[0, :, pl.ds(pl.multiple_of(j2 * FF_T, FF_T), FF_T)]

            def down(row0, rows):
                acc = jnp.dot(act_s[row0:row0 + rows, :], wbuf[slot, 0].astype(BF16), preferred_element_type=F32) + bd
                x_s[row0:row0 + rows, pl.ds(pl.multiple_of(j2 * FF_T, FF_T), FF_T)] = acc.astype(BF16)

            _for_row_chunks(nblk, down)
            return carry

        lax.fori_loop(0, N2, out_step, 0)

        for i in range(SB_SUBS):
            @pl.when(i < nblk)
            def _():
                if i >= 2:
                    _sub_out_copy(stage_out, out_hbm, sem_out, blk0 + i - 2, i % 2).wait()
                _pack_rows(x_s[i * SUB:(i + 1) * SUB, :].astype(F32), stage_out.at[i % 2], 0, SUB)
                _sub_out_copy(stage_out, out_hbm, sem_out, blk0 + i, i % 2).start()
        for i in range(SB_SUBS):
            @pl.when((i < nblk) & (i + 2 >= nblk))
            def _():
                _sub_out_copy(stage_out, out_hbm, sem_out, blk0 + i, i % 2).wait()


def _expert_call(sb_e, sb_n, sb_b, xs, w_gu, b_gu, w_dn, b_dn):
    gs = pltpu.PrefetchScalarGridSpec(
        num_scalar_prefetch=3, grid=(S_MAX,),
        in_specs=[pl.BlockSpec(memory_space=pl.ANY),
                  pl.BlockSpec(memory_space=pl.ANY),
                  pl.BlockSpec((1, 1, 2 * DFF), lambda s, e, n, b: (e[s], 0, 0)),
                  pl.BlockSpec(memory_space=pl.ANY),
                  pl.BlockSpec((1, 1, D), lambda s, e, n, b: (e[s], 0, 0))],
        out_specs=pl.BlockSpec(memory_space=pl.ANY),
        scratch_shapes=[pltpu.VMEM((SB_R, D), BF16),
                        pltpu.VMEM((SB_R, DFF), BF16),
                        pltpu.VMEM((2, 2, D, FF_T), F32),
                        pltpu.VMEM((2, SUB * PACK_W, LANES), I32),
                        pltpu.VMEM((2, SUB * PACK_W, LANES), I32),
                        pltpu.SemaphoreType.DMA((2, 2)),
                        pltpu.SemaphoreType.DMA((2,)),
                        pltpu.SemaphoreType.DMA((2,))])
    return pl.pallas_call(
        _expert_kernel, grid_spec=gs,
        out_shape=jax.ShapeDtypeStruct(xs.shape, xs.dtype),
        input_output_aliases={3: 0},
        compiler_params=_cparams(("arbitrary",)),
        name="moe_experts",
    )(sb_e, sb_n, sb_b, xs, w_gu, b_gu, w_dn, b_dn)


def _combine_copy(src_hbm, buf, sem, src_row, dst_row, rows):
    return pltpu.make_async_copy(src_hbm.at[pl.ds(src_row * PACK_W, rows * PACK_W)],
                                 buf.at[pl.ds(dst_row * PACK_W, rows * PACK_W)], sem)


def _combine_kernel(dest_ref, ys_hbm, gates_ref, x1_ref, gt_ref, ln_ref, o_ref, bufs, y_s, sems):
    step = pl.program_id(0) * TILES_L + pl.program_id(1)

    def issue(tile, slot):
        def body(tg, carry):
            for tt in range(COMBINE_UNROLL):
                t = tg * COMBINE_UNROLL + tt
                for kk in range(TOPK):
                    _combine_copy(ys_hbm, bufs.at[slot], sems.at[slot], dest_ref[(tile * ROW_T + t) * TOPK + kk],
                                  kk * ROW_T + t, 1).start()
            return carry
        lax.fori_loop(0, ROW_T // COMBINE_UNROLL, body, 0)

    @pl.when(step == 0)
    def _():
        issue(0, 0)

    @pl.when(step + 1 < B * TILES_L)
    def _():
        issue(step + 1, (step + 1) % 2)

    slot = step % 2
    buf = bufs.at[slot]
    _combine_copy(ys_hbm, buf, sems.at[slot], 0, 0, ROW_T * TOPK).wait()

    gates = gates_ref[...]
    gk = [jnp.broadcast_to(gates[:, kk:kk + 1], (ROW_T, LANES)) for kk in range(TOPK)]
    for w in range(PACK_W):
        acc_lo = jnp.zeros((ROW_T, LANES), F32)
        acc_hi = jnp.zeros((ROW_T, LANES), F32)
        for kk in range(TOPK):
            lo, hi = _unpack_word(buf[pl.ds(kk * ROW_T * PACK_W + w, ROW_T, stride=PACK_W), :])
            acc_lo = acc_lo + gk[kk] * lo
            acc_hi = acc_hi + gk[kk] * hi
        y_s[:, 2 * w * LANES:(2 * w + 1) * LANES] = acc_lo
        y_s[:, (2 * w + 1) * LANES:(2 * w + 2) * LANES] = acc_hi
    o_ref[0] = x1_ref[0] + gt_ref[0] * _rms(y_s[...], ln_ref[...])


def _combine_call(dest, ys, gates, x1, mod3, ln_post):
    tok = lambda b, i: b * TILES_L + i
    gs = pltpu.PrefetchScalarGridSpec(
        num_scalar_prefetch=1, grid=(B, TILES_L),
        in_specs=[pl.BlockSpec(memory_space=pl.ANY),
                  pl.BlockSpec((ROW_T, LANES), lambda b, i, d: (tok(b, i), 0)),
                  pl.BlockSpec((1, ROW_T, D), lambda b, i, d: (b, i, 0)),
                  pl.BlockSpec((1, 1, D), lambda b, i, d: (b, 0, 5)),
                  pl.BlockSpec((1, D), lambda b, i, d: (0, 0))],
        out_specs=pl.BlockSpec((1, ROW_T, D), lambda b, i, d: (b, i, 0)),
        scratch_shapes=[pltpu.VMEM((2, TOPK * ROW_T * PACK_W, LANES), I32),
                        pltpu.VMEM((ROW_T, D), F32),
                        pltpu.SemaphoreType.DMA((2,))])
    return pl.pallas_call(
        _combine_kernel, grid_spec=gs,
        out_shape=jax.ShapeDtypeStruct((B, N, D), F32),
        compiler_params=_cparams(("arbitrary", "arbitrary")),
        name="moe_combine",
    )(dest, ys, gates, x1, mod3, ln_post)


def _rot_cols(w):
    half = ROPE // 2
    return jnp.concatenate([-w[..., half:], w[..., :half]], axis=-1)


def _rope_tables():
    t = np.arange(N)
    r = (t // GRID_W).astype(np.float64)
    cl = (t % GRID_W).astype(np.float64)
    n_freq = ROPE // 4
    inv = ROPE_THETA ** (-np.arange(n_freq, dtype=np.float64) / n_freq)
    ang = np.concatenate([r[:, None] * inv, cl[:, None] * inv], axis=-1)
    cos, sin = np.cos(ang), np.sin(ang)
    cs = np.concatenate([cos, cos, sin, sin], axis=-1).astype(np.float32)
    ctx = np.concatenate([np.ones((M, ROPE), np.float32), np.zeros((M, ROPE), np.float32)], axis=-1)
    return jnp.asarray(np.concatenate([ctx, cs], axis=0)), jnp.asarray(cs * QK_SCALE)


def _routing_tables(ids, counts_f):
    e = ids[:, :TOPK]
    rank = ids[:, TOPK:2 * TOPK]
    counts = counts_f[0, :NE].astype(I32)
    nsub = (counts + SUB - 1) // SUB
    sub_start = jnp.cumsum(nsub) - nsub
    onehot = e[..., None] == jnp.arange(NE, dtype=I32)
    dest = jnp.sum(jnp.where(onehot, sub_start * SUB, 0), axis=-1) + rank
    nsb = (nsub + SB_SUBS - 1) // SB_SUBS
    sb_end = jnp.cumsum(nsb)
    sb_start = sb_end - nsb
    n_active = sb_end[-1]
    s_idx = jnp.arange(S_MAX, dtype=I32)
    s_eff = jnp.minimum(s_idx, n_active - 1)
    sb_e = jnp.minimum(jnp.searchsorted(sb_end, s_eff, side="right"), NE - 1).astype(I32)
    local = s_eff - sb_start[sb_e]
    sb_b = (sub_start[sb_e] + local * SB_SUBS).astype(I32)
    sb_n = jnp.where(s_idx < n_active, jnp.clip(nsub[sb_e] - local * SB_SUBS, 0, SB_SUBS), 0).astype(I32)
    pad = jnp.concatenate([sub_start * SUB + counts, nsub * SUB - counts,
                           jnp.stack([jnp.sum(nsub), N_SUBS - jnp.sum(nsub)])]).astype(I32)
    return dest.reshape(-1).astype(I32), pad, sb_e, sb_n, sb_b


def kernel(x, c, ctx, c_ctx, w_mod, b_mod, ln_mix_pre, ln_mix_post, ln_ffn_pre, ln_ffn_post, w_in, q_norm, kv_norm,
           w_uq, w_ukv, hgrn_lb, g_norm, w_out, router_w, router_b, w_gate_up, b_gate_up, w_down, b_down):
    c_all = jnp.concatenate([c, c_ctx[None, :], jnp.zeros((MOD_ROWS - B - 1, D), F32)], axis=0)
    mod = _mod_call(c_all, w_mod[0], b_mod)
    mod3 = mod.reshape(MOD_ROWS, 1, 6 * D)

    w = w_in[0]
    o_q, o_kv, o_kr = 0, Q_RANK, Q_RANK + KV_RANK
    o_hq = o_kr + ROPE
    wide = H * HW
    o_ff, o_fb, o_hi, o_hg = o_hq + wide, o_hq + 2 * wide, o_hq + 3 * wide, o_hq + 4 * wide
    w_kr = w[:, o_kr:o_kr + ROPE]
    w_p = jnp.concatenate([w[:, o_q:o_kr], w_kr, _rot_cols(w_kr), jnp.zeros((D, LANES), F32),
                           w[:, o_hq:o_hq + wide], w[:, o_hi:o_hi + wide], w[:, o_hg:o_hg + wide]],
                          axis=1).astype(BF16)
    w_f = w[:, o_ff:o_ff + 2 * wide].astype(BF16)
    wq = w_uq[0]
    wq_r = wq[..., HW:]
    w_q = jnp.concatenate([wq[..., :HW], wq_r, _rot_cols(wq_r)], axis=-1).reshape(Q_RANK, 2 * wide).astype(BF16)
    wkv = w_ukv[0]
    w_kv = jnp.concatenate([wkv[..., :HW].reshape(KV_RANK, wide), wkv[..., HW:].reshape(KV_RANK, wide)],
                           axis=1).astype(BF16)
    w_o = w_out[0].astype(BF16)
    cs_all, cs_q = _rope_tables()
    lb = jnp.cumsum(jax.nn.softmax(hgrn_lb.astype(F32), axis=1), axis=1)
    r_w = jnp.concatenate([router_w[0], jnp.zeros((D, LANES - NE), F32)], axis=1)
    r_w_hi = r_w.astype(BF16)
    r_w = jnp.concatenate([r_w_hi, (r_w - r_w_hi.astype(F32)).astype(BF16)], axis=1)
    r_b = jnp.concatenate([router_b[0], jnp.zeros((LANES - NE,), F32)])[None, :]

    h_all = _normmod_call(x, ctx, mod3, ln_mix_pre)
    h_flat = h_all.reshape(TA, D)
    p = _matmul_call(h_flat, w_p, BF16, "in_proj")
    fraw = _matmul_call(h_flat, w_f, F32, "in_proj_forget")
    k_all, v_all = _kvprep_call(p, kv_norm, w_kv, cs_all)
    q_all = _qprep_call(p, q_norm, w_q, cs_q)
    att = _attn_call(q_all, k_all, v_all)
    o_f, o_b = _hgrn_call(p, fraw, lb[0, 0][None, :], lb[1, 0][None, :])

    g_norm_t = g_norm
    x1, h2w, ids, gates, counts = _mixout_call(att, o_f, o_b, p, g_norm_t, w_o[:wide], w_o[wide:], x, mod3,
                                               ln_mix_post, ln_ffn_pre, r_w, r_b)

    dest, pad, sb_e, sb_n, sb_b = _routing_tables(ids, counts)
    xs = _dispatch_call(dest, pad, h2w)
    ys = _expert_call(sb_e, sb_n, sb_b, xs, w_gate_up[0], b_gate_up[0][:, None, :], w_down[0], b_down[0][:, None, :])
    return _combine_call(dest, ys, gates, x1, mod3, ln_ffn_post)
```

```python
import functools

import numpy as np
import jax
import jax.numpy as jnp
from jax import lax
from jax.experimental import pallas as pl
from jax.experimental.pallas import tpu as pltpu

F32 = jnp.float32
BF16 = jnp.bfloat16
I32 = jnp.int32

D = 2048
B = 4
N = 4096
M = 256
NA = N + M
T = B * N
TA = B * NA
EPS = 1e-6
H = 8
HW = 128
Q_RANK = 512
KV_RANK = 256
ROPE = 64
QK_SCALE = float((128 + ROPE) ** -0.5)
GRID_W = 64
ROPE_THETA = 10000.0
NE = 32
TOPK = 4
DFF = 2048
LIMIT = 7.0
ALPHA = 1.702
LOG2E = float(np.log2(np.e))

LANES = 128
SUBLANES = 8
VMEM_LIMIT = 56 * 1024 * 1024

ROW_T = 256
TILES_A = NA // ROW_T
TILES_L = N // ROW_T
MM_TM = 1024
MM_TN = 1024
MOD_ROWS = SUBLANES
HG_C = 256
HG_STEPS = NA // HG_C
HG_CTX = M // HG_C
HG_LEVELS = tuple(HG_C >> i for i in range(1, HG_C.bit_length()))
SUB = 256
N_SUBS = T * TOPK // SUB + NE
SB_SUBS = 14
SB_R = SB_SUBS * SUB
CHUNK_SUBS = 4
S_MAX = -(-N_SUBS // SB_SUBS) + NE
FF_T = 256
N1 = DFF // FF_T
N2 = D // FF_T
PACK_W = D // 2 // LANES
DISP_T = 512
COMBINE_UNROLL = 4


def _sigmoid(x):
    return 1.0 / (1.0 + jnp.exp(-x))


def _rms(x, w):
    ms = jnp.mean(x * x, axis=-1, keepdims=True)
    return x * lax.rsqrt(ms + EPS) * w


def _cparams(sem, vmem=VMEM_LIMIT):
    return pltpu.CompilerParams(dimension_semantics=sem, vmem_limit_bytes=vmem)


def _mod_kernel(c_ref, w_ref, b_ref, o_ref):
    c = c_ref[...]
    a = (c * _sigmoid(c)).astype(BF16)
    o_ref[...] = jnp.dot(a, w_ref[...].astype(BF16), preferred_element_type=F32) + b_ref[...]


def _mod_call(c_all, w_mod, b_mod):
    tn = MM_TN
    return pl.pallas_call(
        _mod_kernel,
        grid=(6 * D // tn,),
        in_specs=[pl.BlockSpec((MOD_ROWS, D), lambda j: (0, 0)),
                  pl.BlockSpec((D, tn), lambda j: (0, j)),
                  pl.BlockSpec((1, tn), lambda j: (0, j))],
        out_specs=pl.BlockSpec((MOD_ROWS, tn), lambda j: (0, j)),
        out_shape=jax.ShapeDtypeStruct((MOD_ROWS, 6 * D), F32),
        compiler_params=_cparams(("arbitrary",)),
        name="mod",
    )(c_all, w_mod, b_mod)


def _normmod_kernel(x_ref, ctx_ref, sh_ref, sc_ref, w_ref, o_ref):
    def go(src):
        y = _rms(src, w_ref[...])
        o_ref[0] = (y * (1.0 + sc_ref[0]) + sh_ref[0]).astype(BF16)

    i = pl.program_id(1)

    @pl.when(i == 0)
    def _():
        go(ctx_ref[0])

    @pl.when(i > 0)
    def _():
        go(x_ref[0])


def _normmod_call(x, ctx, mod3, ln_w):
    row = lambda b, i: jnp.where(i == 0, B, b)
    return pl.pallas_call(
        _normmod_kernel,
        grid=(B, TILES_A),
        in_specs=[pl.BlockSpec((1, ROW_T, D), lambda b, i: (b, jnp.maximum(i - 1, 0), 0)),
                  pl.BlockSpec((1, ROW_T, D), lambda b, i: (b, 0, 0)),
                  pl.BlockSpec((1, 1, D), lambda b, i: (row(b, i), 0, 0)),
                  pl.BlockSpec((1, 1, D), lambda b, i: (row(b, i), 0, 1)),
                  pl.BlockSpec((1, D), lambda b, i: (0, 0))],
        out_specs=pl.BlockSpec((1, ROW_T, D), lambda b, i: (b, i, 0)),
        out_shape=jax.ShapeDtypeStruct((B, NA, D), BF16),
        compiler_params=_cparams(("arbitrary", "arbitrary")),
        name="normmod",
    )(x, ctx, mod3, mod3, ln_w)


def _matmul_kernel(a_ref, w_ref, o_ref):
    o_ref[...] = jnp.dot(a_ref[...], w_ref[...], preferred_element_type=F32).astype(o_ref.dtype)


def _matmul_call(a, w, out_dtype, name):
    rows, k = a.shape
    cols = w.shape[1]
    return pl.pallas_call(
        _matmul_kernel,
        grid=(rows // MM_TM, cols // MM_TN),
        in_specs=[pl.BlockSpec((MM_TM, k), lambda i, j: (i, 0)),
                  pl.BlockSpec((k, MM_TN), lambda i, j: (0, j))],
        out_specs=pl.BlockSpec((MM_TM, MM_TN), lambda i, j: (i, j)),
        out_shape=jax.ShapeDtypeStruct((rows, cols), out_dtype),
        compiler_params=_cparams(("arbitrary", "arbitrary")),
        name=name,
    )(a, w)


def _kvprep_kernel(p_ref, kvn_ref, w_ref, cs_ref, k_ref, v_ref):
    ckv = p_ref[:, Q_RANK:Q_RANK + KV_RANK].astype(F32)
    ckvn = _rms(ckv, kvn_ref[...]).astype(BF16)
    kv = jnp.dot(ckvn, w_ref[...], preferred_element_type=F32)
    a = p_ref[:, Q_RANK + KV_RANK:Q_RANK + KV_RANK + LANES].astype(F32) * cs_ref[...]
    r = a + pltpu.roll(a, ROPE, axis=1)
    lane = lax.broadcasted_iota(I32, r.shape, 1)
    r = jnp.where(lane < ROPE, r, 0.0).astype(BF16)
    for h in range(H):
        k_ref[0, h, :, 0:HW] = kv[:, h * HW:(h + 1) * HW].astype(BF16)
        k_ref[0, h, :, HW:2 * HW] = r
        v_ref[0, h] = kv[:, (H + h) * HW:(H + h + 1) * HW].astype(BF16)


def _kvprep_call(p, kv_norm, w_kv, cs_all):
    return pl.pallas_call(
        _kvprep_kernel,
        grid=(B, TILES_A),
        in_specs=[pl.BlockSpec((ROW_T, MM_TN), lambda b, i: (b * TILES_A + i, 0)),
                  pl.BlockSpec((1, KV_RANK), lambda b, i: (0, 0)),
                  pl.BlockSpec((KV_RANK, 2 * H * HW), lambda b, i: (0, 0)),
                  pl.BlockSpec((ROW_T, LANES), lambda b, i: (i, 0))],
        out_specs=[pl.BlockSpec((1, H, ROW_T, 2 * HW), lambda b, i: (b, 0, i, 0)),
                   pl.BlockSpec((1, H, ROW_T, HW), lambda b, i: (b, 0, i, 0))],
        out_shape=[jax.ShapeDtypeStruct((B, H, NA, 2 * HW), BF16),
                   jax.ShapeDtypeStruct((B, H, NA, HW), BF16)],
        compiler_params=_cparams(("arbitrary", "arbitrary")),
        name="kvprep",
    )(p, kv_norm, w_kv, cs_all)


def _qprep_kernel(p_ref, qn_ref, w_ref, cs_ref, q_ref):
    cq = p_ref[:, 0:Q_RANK].astype(F32)
    cqn = _rms(cq, qn_ref[...]).astype(BF16)
    q = jnp.dot(cqn, w_ref[...], preferred_element_type=F32)
    cs = cs_ref[...]
    for h in range(H):
        q_ref[0, h, :, 0:HW] = (q[:, 2 * h * HW:(2 * h + 1) * HW] * QK_SCALE).astype(BF16)
        a = q[:, (2 * h + 1) * HW:(2 * h + 2) * HW] * cs
        q_ref[0, h, :, HW:2 * HW] = (a + pltpu.roll(a, ROPE, axis=1)).astype(BF16)


def _qprep_call(p, q_norm, w_q, cs_q):
    return pl.pallas_call(
        _qprep_kernel,
        grid=(B, TILES_L),
        in_specs=[pl.BlockSpec((ROW_T, MM_TN), lambda b, i: (b * TILES_A + 1 + i, 0)),
                  pl.BlockSpec((1, Q_RANK), lambda b, i: (0, 0)),
                  pl.BlockSpec((Q_RANK, 2 * H * HW), lambda b, i: (0, 0)),
                  pl.BlockSpec((ROW_T, LANES), lambda b, i: (i, 0))],
        out_specs=pl.BlockSpec((1, H, ROW_T, 2 * HW), lambda b, i: (b, 0, i, 0)),
        out_shape=jax.ShapeDtypeStruct((B, H, N, 2 * HW), BF16),
        compiler_params=_cparams(("arbitrary", "arbitrary")),
        name="qprep",
    )(p, q_norm, w_q, cs_q)


ATT_TQ = 256
ATT_TILES = 16


def _attn_kernel(q_ref, k_ref, v_ref, o_ref):
    for t in range(ATT_TILES):
        rows = slice(t * ATT_TQ, (t + 1) * ATT_TQ)
        s = lax.dot_general(q_ref[0, 0, rows, :], k_ref[0, 0], (((1,), (1,)), ((), ())),
                            preferred_element_type=F32)
        m = jnp.max(s, axis=-1, keepdims=True)
        p = jnp.exp(s - m)
        l = jnp.sum(p, axis=-1, keepdims=True)
        o = jnp.dot(p.astype(BF16), v_ref[0, 0], preferred_element_type=F32)
        o_ref[rows, :] = (o * (1.0 / l)).astype(BF16)


def _attn_call(q, k, v):
    tq = ATT_TQ * ATT_TILES
    return pl.pallas_call(
        _attn_kernel,
        grid=(B, H, N // tq),
        in_specs=[pl.BlockSpec((1, 1, tq, 2 * HW), lambda b, h, i: (b, h, i, 0)),
                  pl.BlockSpec((1, 1, NA, 2 * HW), lambda b, h, i: (b, h, 0, 0)),
                  pl.BlockSpec((1, 1, NA, HW), lambda b, h, i: (b, h, 0, 0))],
        out_specs=pl.BlockSpec((tq, HW), lambda b, h, i: (b * (N // tq) + i, h)),
        out_shape=jax.ShapeDtypeStruct((T, H * HW), BF16),
        compiler_params=_cparams(("arbitrary", "arbitrary", "arbitrary")),
        name="attention",
    )(q, k, v)


def _hgrn_tables(reverse):
    c = HG_C
    t = np.arange(c)[:, None]
    u = np.arange(c)[None, :]
    cum = (u <= t).astype(np.float32)
    masks = [(u == t)]
    for m in HG_LEVELS:
        blk_t = t // m
        masks.append(((blk_t % 2) == 1) & ((u // m) == blk_t - 1))
    masks = np.stack(masks).astype(np.float32)
    if reverse:
        cum = cum[::-1, ::-1]
        masks = masks[:, ::-1, ::-1]
    return jnp.asarray(cum, dtype=BF16), jnp.asarray(masks.reshape(-1, c), dtype=F32)


def _pair_boundary(c_ref, cval, m, reverse):
    def rows_of(r_of_group):
        return jnp.concatenate([jnp.broadcast_to(c_ref[r_of_group(gi):r_of_group(gi) + 1, :], (SUBLANES, HW))
                                for gi in range(HG_C // SUBLANES)], axis=0)

    if m >= 4:
        def boundary(gi):
            pair = (SUBLANES * gi // (2 * m)) * 2 * m
            return pair + m if reverse else pair + m - 1
        return rows_of(boundary)
    row = lax.broadcasted_iota(I32, (HG_C, HW), 0)
    if m == 2:
        half = SUBLANES // 2
        edge = m if reverse else m - 1
        first = rows_of(lambda gi: SUBLANES * gi + edge)
        second = rows_of(lambda gi: SUBLANES * gi + half + edge)
        return jnp.where((row & (SUBLANES - 1)) < half, first, second)
    if reverse:
        return jnp.where((row & 1) == 0, pltpu.roll(cval, HG_C - 1, 0), cval)
    return jnp.where((row & 1) == 1, pltpu.roll(cval, 1, 0), cval)


def _hgrn_head(h, hq_ref, hi_ref, fr_ref, lb_ref, cum_ref, mk_ref, s_ref, o_ref, c_ref, reverse):
    c = HG_C
    col = pl.ds(pl.multiple_of(h * HW, HW), HW)
    xq = hq_ref[:, col].astype(F32)
    q = xq * _sigmoid(xq)
    lb = lb_ref[:, col]
    f = lb + (1.0 - lb) * _sigmoid(fr_ref[:, col])
    g = jnp.log(f)
    k = 1.0 - f
    v = hi_ref[:, col]
    g_hi = g.astype(BF16)
    g_lo = (g - g_hi.astype(F32)).astype(BF16)
    cum = cum_ref[...]
    cs = jnp.dot(jnp.concatenate([cum, cum], axis=1), jnp.concatenate([g_hi, g_lo], axis=0),
                 preferred_element_type=F32) * LOG2E
    c_ref[...] = cs
    total_row = 0 if reverse else c - 1
    c_tot = c_ref[total_row:total_row + 1, :]

    nt = (((1,), (1,)), ((), ()))
    st = s_ref[h]
    o = lax.dot_general((q * jnp.exp2(cs)).astype(BF16), st.astype(BF16), nt, preferred_element_type=F32)
    qb = q.astype(BF16)
    kb = k.astype(BF16)
    sc = lax.dot_general(qb, kb, nt, preferred_element_type=F32) * mk_ref[0:c, :]
    sign = jnp.uint32(0x80000000)
    for li, m in enumerate(HG_LEVELS):
        z = cs - _pair_boundary(c_ref, cs, m, reverse)
        w = jnp.exp2(pltpu.bitcast(pltpu.bitcast(z, jnp.uint32) | sign, F32)).astype(BF16)
        sc = sc + (lax.dot_general(qb * w, kb * w, nt, preferred_element_type=F32)
                   * mk_ref[(li + 1) * c:(li + 2) * c, :])
    o = o + jnp.dot(sc.astype(BF16), v, preferred_element_type=F32)
    o_ref[:, col] = o.astype(BF16)
    kend = (k * jnp.exp2(c_tot - cs)).astype(BF16)
    s_ref[h] = st * jnp.exp2(c_tot) + lax.dot_general(v, kend, (((0,), (0,)), ((), ())), preferred_element_type=F32)


HG_UNROLL = 8


def _hgrn_kernel(hq_f, hi_f, fr_f, hq_b, hi_b, fr_b, lbf_ref, lbb_ref, cumf_ref, mkf_ref, cumb_ref, mkb_ref,
                 of_ref, ob_ref, sf_ref, sb_ref, c_s):
    @pl.when(pl.program_id(1) == 0)
    def _():
        sf_ref[...] = jnp.zeros_like(sf_ref)
        sb_ref[...] = jnp.zeros_like(sb_ref)

    def body(i, carry):
        for hh in range(HG_UNROLL):
            h = i * HG_UNROLL + hh
            _hgrn_head(h, hq_f, hi_f, fr_f, lbf_ref, cumf_ref, mkf_ref, sf_ref, of_ref, c_s.at[2 * hh], False)
            _hgrn_head(h, hq_b, hi_b, fr_b, lbb_ref, cumb_ref, mkb_ref, sb_ref, ob_ref, c_s.at[2 * hh + 1], True)
        return carry

    lax.fori_loop(0, H // HG_UNROLL, body, 0)


def _hgrn_call(p, fraw, lb_f, lb_b):
    cum_f, mk_f = _hgrn_tables(False)
    cum_b, mk_b = _hgrn_tables(True)
    wide = H * HW

    def cf(b, s):
        return b * HG_STEPS + s

    def cb(b, s):
        return b * HG_STEPS + jnp.where(s < HG_CTX, HG_CTX - 1 - s, HG_STEPS - 1 + HG_CTX - s)

    const = lambda shape: pl.BlockSpec(shape, lambda b, s: (0, 0))
    return pl.pallas_call(
        _hgrn_kernel,
        grid=(B, HG_STEPS),
        in_specs=[pl.BlockSpec((HG_C, wide), lambda b, s: (cf(b, s), 1)),
                  pl.BlockSpec((HG_C, wide), lambda b, s: (cf(b, s), 2)),
                  pl.BlockSpec((HG_C, wide), lambda b, s: (cf(b, s), 0)),
                  pl.BlockSpec((HG_C, wide), lambda b, s: (cb(b, s), 1)),
                  pl.BlockSpec((HG_C, wide), lambda b, s: (cb(b, s), 2)),
                  pl.BlockSpec((HG_C, wide), lambda b, s: (cb(b, s), 1)),
                  const((1, wide)), const((1, wide)),
                  const(cum_f.shape), const(mk_f.shape), const(cum_b.shape), const(mk_b.shape)],
        out_specs=[pl.BlockSpec((HG_C, wide), lambda b, s: (cf(b, s), 0)),
                   pl.BlockSpec((HG_C, wide), lambda b, s: (cb(b, s), 0))],
        out_shape=[jax.ShapeDtypeStruct((TA, wide), BF16), jax.ShapeDtypeStruct((TA, wide), BF16)],
        scratch_shapes=[pltpu.VMEM((H, HW, HW), F32), pltpu.VMEM((H, HW, HW), F32),
                        pltpu.VMEM((2 * HG_UNROLL, HG_C, HW), F32)],
        compiler_params=_cparams(("arbitrary", "arbitrary")),
        name="hgrn",
    )(p, p, fraw, p, p, fraw, lb_f, lb_b, cum_f, mk_f, cum_b, mk_b)


def _pack_rows(val, dst_ref, row0, rows):
    for s in range(PACK_W):
        lo = val[:, 2 * s * LANES:(2 * s + 1) * LANES]
        hi = val[:, (2 * s + 1) * LANES:(2 * s + 2) * LANES]
        w = pltpu.pack_elementwise([lo, hi], packed_dtype=BF16)
        dst_ref[pl.ds(row0 * PACK_W + s, rows, stride=PACK_W), :] = w


def _unpack_word(w):
    lo = pltpu.unpack_elementwise(w, index=0, packed_dtype=BF16, unpacked_dtype=F32)
    hi = pltpu.unpack_elementwise(w, index=1, packed_dtype=BF16, unpacked_dtype=F32)
    return lo, hi


def _mixout_kernel(att_ref, of_ref, ob_ref, hg_ref, gn_ref, wa_ref, wr_ref, x_ref, gt_ref, shf_ref, scf_ref,
                   lnp_ref, lnf_ref, rw_ref, rb_ref,
                   x1_ref, h2w_ref, ids_ref, gates_ref, cnt_ref, carry_ref):
    first = (pl.program_id(0) == 0) & (pl.program_id(1) == 0)

    @pl.when(first)
    def _():
        carry_ref[...] = jnp.zeros_like(carry_ref)

    o = of_ref[...].astype(F32) + ob_ref[...].astype(F32)
    gn = gn_ref[...]
    recs = []
    for h in range(H):
        oh = o[:, h * HW:(h + 1) * HW]
        gate = hg_ref[:, h * HW:(h + 1) * HW].astype(F32)
        recs.append((_rms(oh, gn) * (gate * _sigmoid(gate))).astype(BF16))
    rec = jnp.concatenate(recs, axis=-1)
    mix = (jnp.dot(att_ref[...], wa_ref[...], preferred_element_type=F32)
           + jnp.dot(rec, wr_ref[...], preferred_element_type=F32))
    x1 = x_ref[0] + gt_ref[0] * _rms(mix, lnp_ref[...])
    x1_ref[0] = x1
    h2 = _rms(x1, lnf_ref[...]) * (1.0 + scf_ref[0]) + shf_ref[0]
    _pack_rows(h2, h2w_ref, 0, ROW_T)

    h_hi = h2.astype(BF16)
    h_lo = (h2 - h_hi.astype(F32)).astype(BF16)
    hh = jnp.dot(h_hi, rw_ref[...], preferred_element_type=F32)
    logits = (hh[:, :LANES] + hh[:, LANES:] + jnp.dot(h_lo, rw_ref[:, :LANES], preferred_element_type=F32)
              + rb_ref[...])
    lane = lax.broadcasted_iota(I32, logits.shape, 1)
    neg = jnp.float32(-jnp.inf)
    lg = jnp.where(lane < NE, logits, neg)
    tops, idxs = [], []
    for _ in range(TOPK):
        m = jnp.max(lg, axis=-1, keepdims=True)
        idx = jnp.min(jnp.where(lg == m, lane, LANES), axis=-1, keepdims=True)
        tops.append(m)
        idxs.append(idx)
        lg = jnp.where(lane == idx, neg, lg)
    exps = [jnp.exp(tk - tops[0]) for tk in tops]
    inv = 1.0 / (exps[0] + exps[1] + exps[2] + exps[3])
    onehot = jnp.zeros(logits.shape, F32)
    for idx in idxs:
        onehot = onehot + jnp.where(lane == idx, 1.0, 0.0)
    r = lax.broadcasted_iota(I32, (ROW_T, ROW_T), 0)
    cidx = lax.broadcasted_iota(I32, (ROW_T, ROW_T), 1)
    lower = jnp.where(cidx < r, 1.0, 0.0).astype(BF16)
    prefix = jnp.dot(lower, onehot.astype(BF16), preferred_element_type=F32) + carry_ref[...]
    ids = jnp.zeros(logits.shape, I32)
    gates = jnp.zeros(logits.shape, F32)
    for kk in range(TOPK):
        rank = jnp.sum(jnp.where(lane == idxs[kk], prefix, 0.0), axis=-1, keepdims=True).astype(I32)
        ids = jnp.where(lane == kk, idxs[kk], ids)
        ids = jnp.where(lane == TOPK + kk, rank, ids)
        gates = jnp.where(lane == kk, exps[kk] * inv, gates)
    ids_ref[...] = ids
    gates_ref[...] = gates
    carry = carry_ref[...] + jnp.sum(onehot, axis=0, keepdims=True)
    carry_ref[...] = carry
    cnt_ref[...] = carry


def _mixout_call(att, o_f, o_b, p, g_norm_t, w_a, w_r, x, mod3, ln_post, ln_ffn, r_w, r_b):
    wide = H * HW
    lat = lambda b, i: b * TILES_A + 1 + i
    tok = lambda b, i: b * TILES_L + i
    const = lambda shape: pl.BlockSpec(shape, lambda b, i: tuple(0 for _ in shape))
    return pl.pallas_call(
        _mixout_kernel,
        grid=(B, TILES_L),
        in_specs=[pl.BlockSpec((ROW_T, wide), lambda b, i: (tok(b, i), 0)),
                  pl.BlockSpec((ROW_T, wide), lambda b, i: (lat(b, i), 0)),
                  pl.BlockSpec((ROW_T, wide), lambda b, i: (lat(b, i), 0)),
                  pl.BlockSpec((ROW_T, wide), lambda b, i: (lat(b, i), 3)),
                  const((1, HW)), const((wide, D)), const((wide, D)),
                  pl.BlockSpec((1, ROW_T, D), lambda b, i: (b, i, 0)),
                  pl.BlockSpec((1, 1, D), lambda b, i: (b, 0, 2)),
                  pl.BlockSpec((1, 1, D), lambda b, i: (b, 0, 3)),
                  pl.BlockSpec((1, 1, D), lambda b, i: (b, 0, 4)),
                  const((1, D)), const((1, D)), const((D, 2 * LANES)), const((1, LANES))],
        out_specs=[pl.BlockSpec((1, ROW_T, D), lambda b, i: (b, i, 0)),
                   pl.BlockSpec((ROW_T * PACK_W, LANES), lambda b, i: (tok(b, i), 0)),
                   pl.BlockSpec((ROW_T, LANES), lambda b, i: (tok(b, i), 0)),
                   pl.BlockSpec((ROW_T, LANES), lambda b, i: (tok(b, i), 0)),
                   const((1, LANES))],
        out_shape=[jax.ShapeDtypeStruct((B, N, D), F32),
                   jax.ShapeDtypeStruct((T * PACK_W, LANES), I32),
                   jax.ShapeDtypeStruct((T, LANES), I32),
                   jax.ShapeDtypeStruct((T, LANES), F32),
                   jax.ShapeDtypeStruct((1, LANES), F32)],
        scratch_shapes=[pltpu.VMEM((1, LANES), F32)],
        compiler_params=_cparams(("arbitrary", "arbitrary")),
        name="mixout_router",
    )(att, o_f, o_b, p, g_norm_t, w_a, w_r, x, mod3, mod3, mod3, ln_post, ln_ffn, r_w, r_b)


def _dispatch_copy(src_vmem, dst_hbm, sem, src_row, dst_row, rows):
    return pltpu.make_async_copy(src_vmem.at[pl.ds(src_row * PACK_W, rows * PACK_W)],
                                 dst_hbm.at[pl.ds(dst_row * PACK_W, rows * PACK_W)], sem)


def _zero_copy(zeros_s, dst_hbm, sem, dst_row, rows):
    return pltpu.make_async_copy(zeros_s.at[pl.ds(0, rows * PACK_W)],
                                 dst_hbm.at[pl.ds(dst_row * PACK_W, rows * PACK_W)], sem)


def _dispatch_kernel(dest_ref, pad_ref, h2w_ref, xs_hbm, zeros_s, sem, zsem):
    step = pl.program_id(0)
    base = step * DISP_T

    def issue(t, carry):
        for kk in range(TOPK):
            _dispatch_copy(h2w_ref, xs_hbm, sem, t, dest_ref[(base + t) * TOPK + kk], 1).start()
        return carry

    tail0 = pad_ref[2 * NE]
    tail_n = pad_ref[2 * NE + 1]

    def zero_rows(wait):
        def expert_pad(e, carry):
            @pl.when(pad_ref[NE + e] > 0)
            def _():
                cp = _zero_copy(zeros_s, xs_hbm, zsem, (pad_ref[e] // SUB) * SUB, SUB)
                cp.wait() if wait else cp.start()
            return carry

        def tail(i, c):
            cp = _zero_copy(zeros_s, xs_hbm, zsem, (tail0 + i) * SUB, SUB)
            cp.wait() if wait else cp.start()
            return c

        lax.fori_loop(0, NE, expert_pad, 0)
        lax.fori_loop(0, tail_n, tail, 0)

    @pl.when(step == 0)
    def _():
        zeros_s[...] = jnp.zeros_like(zeros_s)
        zero_rows(wait=False)
        zero_rows(wait=True)

    lax.fori_loop(0, DISP_T, issue, 0)

    for _ in range(TOPK):
        _dispatch_copy(h2w_ref, xs_hbm, sem, 0, 0, DISP_T).wait()


def _dispatch_call(dest, pad, h2w):
    gs = pltpu.PrefetchScalarGridSpec(
        num_scalar_prefetch=2, grid=(T // DISP_T,),
        in_specs=[pl.BlockSpec((DISP_T * PACK_W, LANES), lambda i, d, z: (i, 0))],
        out_specs=pl.BlockSpec(memory_space=pl.ANY),
        scratch_shapes=[pltpu.VMEM((SUB * PACK_W, LANES), I32),
                        pltpu.SemaphoreType.DMA(()),
                        pltpu.SemaphoreType.DMA(())])
    return pl.pallas_call(
        _dispatch_kernel, grid_spec=gs,
        out_shape=jax.ShapeDtypeStruct((N_SUBS * SUB * PACK_W, LANES), I32),
        compiler_params=_cparams(("arbitrary",)),
        name="moe_dispatch",
    )(dest, pad, h2w)


def _sub_in_copy(xs_hbm, stage, sems, blk, slot):
    return pltpu.make_async_copy(xs_hbm.at[pl.ds(blk * (SUB * PACK_W), SUB * PACK_W)], stage.at[slot], sems.at[slot])


def _sub_out_copy(stage, out_hbm, sems, blk, slot):
    return pltpu.make_async_copy(stage.at[slot], out_hbm.at[pl.ds(blk * (SUB * PACK_W), SUB * PACK_W)], sems.at[slot])


def _for_row_chunks(nblk, fn):
    pair = 2 * CHUNK_SUBS

    @pl.when(nblk >= pair)
    def _():
        fn(0, CHUNK_SUBS * SUB)
        fn(CHUNK_SUBS * SUB, CHUNK_SUBS * SUB)

    for c in range(-(-SB_SUBS // CHUNK_SUBS)):
        row0 = c * CHUNK_SUBS * SUB
        most = min(CHUNK_SUBS, SB_SUBS - c * CHUNK_SUBS)
        for r in range(1, most + 1):
            full = r == CHUNK_SUBS
            cond = (nblk >= c * CHUNK_SUBS + r) if full else (nblk == c * CHUNK_SUBS + r)
            if c < 2:
                cond = cond & (nblk < pair)
            pl.when(cond)(functools.partial(fn, row0, r * SUB))


def _wtile_copy(w_hbm, buf, sems, e, col_tile, slot, which):
    return pltpu.make_async_copy(w_hbm.at[e, :, pl.ds(pl.multiple_of(col_tile * FF_T, FF_T), FF_T)],
                                 buf.at[slot, which], sems.at[slot, which])


def _expert_kernel(sbe_ref, sbn_ref, sbb_ref, xs_hbm, wgu_hbm, bgu_ref, wdn_hbm, bdn_ref,
                   out_hbm, x_s, act_s, wbuf, stage_in, stage_out,
                   sem_w, sem_in, sem_out):
    s = pl.program_id(0)
    nblk = sbn_ref[s]
    blk0 = sbb_ref[s]
    e = sbe_ref[s]

    def gate_tiles(j, slot):
        return (_wtile_copy(wgu_hbm, wbuf, sem_w, e, j, slot, 0),
                _wtile_copy(wgu_hbm, wbuf, sem_w, e, N1 + j, slot, 1))

    def down_tile(j, slot):
        return _wtile_copy(wdn_hbm, wbuf, sem_w, e, j, slot, 0)

    @pl.when(nblk > 0)
    def _():
        for cp in gate_tiles(0, 0):
            cp.start()

        _sub_in_copy(xs_hbm, stage_in, sem_in, blk0, 0).start()
        for i in range(SB_SUBS):
            @pl.when(i < nblk)
            def _():
                if i + 1 < SB_SUBS:
                    @pl.when(i + 1 < nblk)
                    def _():
                        _sub_in_copy(xs_hbm, stage_in, sem_in, blk0 + i + 1, (i + 1) % 2).start()
                _sub_in_copy(xs_hbm, stage_in, sem_in, blk0 + i, i % 2).wait()
                for w in range(PACK_W):
                    lo, hi = _unpack_word(stage_in[i % 2, pl.ds(w, SUB, stride=PACK_W), :])
                    x_s[i * SUB:(i + 1) * SUB, 2 * w * LANES:(2 * w + 1) * LANES] = lo.astype(BF16)
                    x_s[i * SUB:(i + 1) * SUB, (2 * w + 1) * LANES:(2 * w + 2) * LANES] = hi.astype(BF16)

        def hidden_step(j, carry):
            slot = j % 2
            for cp in gate_tiles(j, slot):
                cp.wait()

            @pl.when(j + 1 < N1)
            def _():
                for cp in gate_tiles(j + 1, 1 - slot):
                    cp.start()

            @pl.when(j + 1 == N1)
            def _():
                down_tile(0, 1 - slot).start()

            col = pl.ds(pl.multiple_of(j * FF_T, FF_T), FF_T)
            bg = bgu_ref[0, :, col]
            bl = bgu_ref[0, :, pl.ds(pl.multiple_of(DFF + j * FF_T, FF_T), FF_T)]

            def gate_up(row0, rows):
                x = x_s[row0:row0 + rows, :]
                gu = jnp.dot(x, wbuf[slot, 0].astype(BF16), preferred_element_type=F32) + bg
                li = jnp.dot(x, wbuf[slot, 1].astype(BF16), preferred_element_type=F32) + bl
                glu = jnp.minimum(gu, LIMIT)
                lin = jnp.clip(li, -LIMIT, LIMIT)
                act = glu * _sigmoid(ALPHA * glu) * (lin + 1.0)
                act_s[row0:row0 + rows, col] = act.astype(BF16)

            _for_row_chunks(nblk, gate_up)
            return carry

        lax.fori_loop(0, N1, hidden_step, 0)

        def out_step(j2, carry):
            slot = (N1 + j2) % 2
            down_tile(j2, slot).wait()

            @pl.when(j2 + 1 < N2)
            def _():
                down_tile(j2 + 1, 1 - slot).start()

            bd = bdn_ref[0, :, pl.ds(pl.multiple_of(j2 * FF_T, FF_T), FF_T)]

            def down(row0, rows):
                acc = jnp.dot(act_s[row0:row0 + rows, :], wbuf[slot, 0].astype(BF16), preferred_element_type=F32) + bd
                x_s[row0:row0 + rows, pl.ds(pl.multiple_of(j2 * FF_T, FF_T), FF_T)] = acc.astype(BF16)

            _for_row_chunks(nblk, down)
            return carry

        lax.fori_loop(0, N2, out_step, 0)

        for i in range(SB_SUBS):
            @pl.when(i < nblk)
            def _():
                if i >= 2:
                    _sub_out_copy(stage_out, out_hbm, sem_out, blk0 + i - 2, i % 2).wait()
                _pack_rows(x_s[i * SUB:(i + 1) * SUB, :].astype(F32), stage_out.at[i % 2], 0, SUB)
                _sub_out_copy(stage_out, out_hbm, sem_out, blk0 + i, i % 2).start()
        for i in range(SB_SUBS):
            @pl.when((i < nblk) & (i + 2 >= nblk))
            def _():
                _sub_out_copy(stage_out, out_hbm, sem_out, blk0 + i, i % 2).wait()


def _expert_call(sb_e, sb_n, sb_b, xs, w_gu, b_gu, w_dn, b_dn):
    gs = pltpu.PrefetchScalarGridSpec(
        num_scalar_prefetch=3, grid=(S_MAX,),
        in_specs=[pl.BlockSpec(memory_space=pl.ANY),
                  pl.BlockSpec(memory_space=pl.ANY),
                  pl.BlockSpec((1, 1, 2 * DFF), lambda s, e, n, b: (e[s], 0, 0)),
                  pl.BlockSpec(memory_space=pl.ANY),
                  pl.BlockSpec((1, 1, D), lambda s, e, n, b: (e[s], 0, 0))],
        out_specs=pl.BlockSpec(memory_space=pl.ANY),
        scratch_shapes=[pltpu.VMEM((SB_R, D), BF16),
                        pltpu.VMEM((SB_R, DFF), BF16),
                        pltpu.VMEM((2, 2, D, FF_T), F32),
                        pltpu.VMEM((2, SUB * PACK_W, LANES), I32),
                        pltpu.VMEM((2, SUB * PACK_W, LANES), I32),
                        pltpu.SemaphoreType.DMA((2, 2)),
                        pltpu.SemaphoreType.DMA((2,)),
                        pltpu.SemaphoreType.DMA((2,))])
    return pl.pallas_call(
        _expert_kernel, grid_spec=gs,
        out_shape=jax.ShapeDtypeStruct(xs.shape, xs.dtype),
        input_output_aliases={3: 0},
        compiler_params=_cparams(("arbitrary",)),
        name="moe_experts",
    )(sb_e, sb_n, sb_b, xs, w_gu, b_gu, w_dn, b_dn)


def _combine_copy(src_hbm, buf, sem, src_row, dst_row, rows):
    return pltpu.make_async_copy(src_hbm.at[pl.ds(src_row * PACK_W, rows * PACK_W)],
                                 buf.at[pl.ds(dst_row * PACK_W, rows * PACK_W)], sem)


def _combine_kernel(dest_ref, ys_hbm, gates_ref, x1_ref, gt_ref, ln_ref, o_ref, bufs, y_s, sems):
    step = pl.program_id(0) * TILES_L + pl.program_id(1)

    def issue(tile, slot):
        def body(tg, carry):
            for tt in range(COMBINE_UNROLL):
                t = tg * COMBINE_UNROLL + tt
                for kk in range(TOPK):
                    _combine_copy(ys_hbm, bufs.at[slot], sems.at[slot], dest_ref[(tile * ROW_T + t) * TOPK + kk],
                                  kk * ROW_T + t, 1).start()
            return carry
        lax.fori_loop(0, ROW_T // COMBINE_UNROLL, body, 0)

    @pl.when(step == 0)
    def _():
        issue(0, 0)

    @pl.when(step + 1 < B * TILES_L)
    def _():
        issue(step + 1, (step + 1) % 2)

    slot = step % 2
    buf = bufs.at[slot]
    _combine_copy(ys_hbm, buf, sems.at[slot], 0, 0, ROW_T * TOPK).wait()

    gates = gates_ref[...]
    gk = [jnp.broadcast_to(gates[:, kk:kk + 1], (ROW_T, LANES)) for kk in range(TOPK)]
    for w in range(PACK_W):
        acc_lo = jnp.zeros((ROW_T, LANES), F32)
        acc_hi = jnp.zeros((ROW_T, LANES), F32)
        for kk in range(TOPK):
            lo, hi = _unpack_word(buf[pl.ds(kk * ROW_T * PACK_W + w, ROW_T, stride=PACK_W), :])
            acc_lo = acc_lo + gk[kk] * lo
            acc_hi = acc_hi + gk[kk] * hi
        y_s[:, 2 * w * LANES:(2 * w + 1) * LANES] = acc_lo
        y_s[:, (2 * w + 1) * LANES:(2 * w + 2) * LANES] = acc_hi
    o_ref[0] = x1_ref[0] + gt_ref[0] * _rms(y_s[...], ln_ref[...])


def _combine_call(dest, ys, gates, x1, mod3, ln_post):
    tok = lambda b, i: b * TILES_L + i
    gs = pltpu.PrefetchScalarGridSpec(
        num_scalar_prefetch=1, grid=(B, TILES_L),
        in_specs=[pl.BlockSpec(memory_space=pl.ANY),
                  pl.BlockSpec((ROW_T, LANES), lambda b, i, d: (tok(b, i), 0)),
                  pl.BlockSpec((1, ROW_T, D), lambda b, i, d: (b, i, 0)),
                  pl.BlockSpec((1, 1, D), lambda b, i, d: (b, 0, 5)),
                  pl.BlockSpec((1, D), lambda b, i, d: (0, 0))],
        out_specs=pl.BlockSpec((1, ROW_T, D), lambda b, i, d: (b, i, 0)),
        scratch_shapes=[pltpu.VMEM((2, TOPK * ROW_T * PACK_W, LANES), I32),
                        pltpu.VMEM((ROW_T, D), F32),
                        pltpu.SemaphoreType.DMA((2,))])
    return pl.pallas_call(
        _combine_kernel, grid_spec=gs,
        out_shape=jax.ShapeDtypeStruct((B, N, D), F32),
        compiler_params=_cparams(("arbitrary", "arbitrary")),
        name="moe_combine",
    )(dest, ys, gates, x1, mod3, ln_post)


def _rot_cols(w):
    half = ROPE // 2
    return jnp.concatenate([-w[..., half:], w[..., :half]], axis=-1)


def _rope_tables():
    t = np.arange(N)
    r = (t // GRID_W).astype(np.float64)
    cl = (t % GRID_W).astype(np.float64)
    n_freq = ROPE // 4
    inv = ROPE_THETA ** (-np.arange(n_freq, dtype=np.float64) / n_freq)
    ang = np.concatenate([r[:, None] * inv, cl[:, None] * inv], axis=-1)
    cos, sin = np.cos(ang), np.sin(ang)
    cs = np.concatenate([cos, cos, sin, sin], axis=-1).astype(np.float32)
    ctx = np.concatenate([np.ones((M, ROPE), np.float32), np.zeros((M, ROPE), np.float32)], axis=-1)
    return jnp.asarray(np.concatenate([ctx, cs], axis=0)), jnp.asarray(cs * QK_SCALE)


def _routing_tables(ids, counts_f):
    e = ids[:, :TOPK]
    rank = ids[:, TOPK:2 * TOPK]
    counts = counts_f[0, :NE].astype(I32)
    nsub = (counts + SUB - 1) // SUB
    sub_start = jnp.cumsum(nsub) - nsub
    onehot = e[..., None] == jnp.arange(NE, dtype=I32)
    dest = jnp.sum(jnp.where(onehot, sub_start * SUB, 0), axis=-1) + rank
    nsb = (nsub + SB_SUBS - 1) // SB_SUBS
    sb_end = jnp.cumsum(nsb)
    sb_start = sb_end - nsb
    n_active = sb_end[-1]
    s_idx = jnp.arange(S_MAX, dtype=I32)
    s_eff = jnp.minimum(s_idx, n_active - 1)
    sb_e = jnp.minimum(jnp.searchsorted(sb_end, s_eff, side="right"), NE - 1).astype(I32)
    local = s_eff - sb_start[sb_e]
    sb_b = (sub_start[sb_e] + local * SB_SUBS).astype(I32)
    sb_n = jnp.where(s_idx < n_active, jnp.clip(nsub[sb_e] - local * SB_SUBS, 0, SB_SUBS), 0).astype(I32)
    pad = jnp.concatenate([sub_start * SUB + counts, nsub * SUB - counts,
                           jnp.stack([jnp.sum(nsub), N_SUBS - jnp.sum(nsub)])]).astype(I32)
    return dest.reshape(-1).astype(I32), pad, sb_e, sb_n, sb_b


def kernel(x, c, ctx, c_ctx, w_mod, b_mod, ln_mix_pre, ln_mix_post, ln_ffn_pre, ln_ffn_post, w_in, q_norm, kv_norm,
           w_uq, w_ukv, hgrn_lb, g_norm, w_out, router_w, router_b, w_gate_up, b_gate_up, w_down, b_down):
    c_all = jnp.concatenate([c, c_ctx[None, :], jnp.zeros((MOD_ROWS - B - 1, D), F32)], axis=0)
    mod = _mod_call(c_all, w_mod[0], b_mod)
    mod3 = mod.reshape(MOD_ROWS, 1, 6 * D)

    w = w_in[0]
    o_q, o_kv, o_kr = 0, Q_RANK, Q_RANK + KV_RANK
    o_hq = o_kr + ROPE
    wide = H * HW
    o_ff, o_fb, o_hi, o_hg = o_hq + wide, o_hq + 2 * wide, o_hq + 3 * wide, o_hq + 4 * wide
    w_kr = w[:, o_kr:o_kr + ROPE]
    w_p = jnp.concatenate([w[:, o_q:o_kr], w_kr, _rot_cols(w_kr), jnp.zeros((D, LANES), F32),
                           w[:, o_hq:o_hq + wide], w[:, o_hi:o_hi + wide], w[:, o_hg:o_hg + wide]],
                          axis=1).astype(BF16)
    w_f = w[:, o_ff:o_ff + 2 * wide].astype(BF16)
    wq = w_uq[0]
    wq_r = wq[..., HW:]
    w_q = jnp.concatenate([wq[..., :HW], wq_r, _rot_cols(wq_r)], axis=-1).reshape(Q_RANK, 2 * wide).astype(BF16)
    wkv = w_ukv[0]
    w_kv = jnp.concatenate([wkv[..., :HW].reshape(KV_RANK, wide), wkv[..., HW:].reshape(KV_RANK, wide)],
                           axis=1).astype(BF16)
    w_o = w_out[0].astype(BF16)
    cs_all, cs_q = _rope_tables()
    lb = jnp.cumsum(jax.nn.softmax(hgrn_lb.astype(F32), axis=1), axis=1)
    r_w = jnp.concatenate([router_w[0], jnp.zeros((D, LANES - NE), F32)], axis=1)
    r_w_hi = r_w.astype(BF16)
    r_w = jnp.concatenate([r_w_hi, (r_w - r_w_hi.astype(F32)).astype(BF16)], axis=1)
    r_b = jnp.concatenate([router_b[0], jnp.zeros((LANES - NE,), F32)])[None, :]

    h_all = _normmod_call(x, ctx, mod3, ln_mix_pre)
    h_flat = h_all.reshape(TA, D)
    p = _matmul_call(h_flat, w_p, BF16, "in_proj")
    fraw = _matmul_call(h_flat, w_f, F32, "in_proj_forget")
    k_all, v_all = _kvprep_call(p, kv_norm, w_kv, cs_all)
    q_all = _qprep_call(p, q_norm, w_q, cs_q)
    att = _attn_call(q_all, k_all, v_all)
    o_f, o_b = _hgrn_call(p, fraw, lb[0, 0][None, :], lb[1, 0][None, :])

    g_norm_t = g_norm
    x1, h2w, ids, gates, counts = _mixout_call(att, o_f, o_b, p, g_norm_t, w_o[:wide], w_o[wide:], x, mod3,
                                               ln_mix_post, ln_ffn_pre, r_w, r_b)

    dest, pad, sb_e, sb_n, sb_b = _routing_tables(ids, counts)
    xs = _dispatch_call(dest, pad, h2w)
    ys = _expert_call(sb_e, sb_n, sb_b, xs, w_gate_up[0], b_gate_up[0][:, None, :], w_down[0], b_down[0][:, None, :])
    return _combine_call(dest, ys, gates, x1, mod3, ln_ffn_post)
```

```python
import functools

import numpy as np
import jax
import jax.numpy as jnp
from jax import lax
from jax.experimental import pallas as pl
from jax.experimental.pallas import tpu as pltpu

F32 = jnp.float32
BF16 = jnp.bfloat16
I32 = jnp.int32

D = 2048
B = 4
N = 4096
M = 256
NA = N + M
T = B * N
TA = B * NA
EPS = 1e-6
H = 8
HW = 128
Q_RANK = 512
KV_RANK = 256
ROPE = 64
QK_SCALE = float((128 + ROPE) ** -0.5)
GRID_W = 64
ROPE_THETA = 10000.0
NE = 32
TOPK = 4
DFF = 2048
LIMIT = 7.0
ALPHA = 1.702
LOG2E = float(np.log2(np.e))

LANES = 128
SUBLANES = 8
VMEM_LIMIT = 56 * 1024 * 1024

ROW_T = 256
TILES_A = NA // ROW_T
TILES_L = N // ROW_T
MM_TM = 1024
MM_TN = 1024
MOD_ROWS = SUBLANES
HG_C = 256
HG_STEPS = NA // HG_C
HG_CTX = M // HG_C
HG_LEVELS = tuple(HG_C >> i for i in range(1, HG_C.bit_length()))
SUB = 256
N_SUBS = T * TOPK // SUB + NE
SB_SUBS = 14
SB_R = SB_SUBS * SUB
CHUNK_SUBS = 4
S_MAX = -(-N_SUBS // SB_SUBS) + NE
FF_T = 256
N1 = DFF // FF_T
N2 = D // FF_T
PACK_W = D // 2 // LANES
DISP_T = 512
COMBINE_UNROLL = 4


def _sigmoid(x):
    return 1.0 / (1.0 + jnp.exp(-x))


def _rms(x, w):
    ms = jnp.mean(x * x, axis=-1, keepdims=True)
    return x * lax.rsqrt(ms + EPS) * w


def _cparams(sem, vmem=VMEM_LIMIT):
    return pltpu.CompilerParams(dimension_semantics=sem, vmem_limit_bytes=vmem)


def _mod_kernel(c_ref, w_ref, b_ref, o_ref):
    c = c_ref[...]
    a = (c * _sigmoid(c)).astype(BF16)
    o_ref[...] = jnp.dot(a, w_ref[...].astype(BF16), preferred_element_type=F32) + b_ref[...]


def _mod_call(c_all, w_mod, b_mod):
    tn = MM_TN
    return pl.pallas_call(
        _mod_kernel,
        grid=(6 * D // tn,),
        in_specs=[pl.BlockSpec((MOD_ROWS, D), lambda j: (0, 0)),
                  pl.BlockSpec((D, tn), lambda j: (0, j)),
                  pl.BlockSpec((1, tn), lambda j: (0, j))],
        out_specs=pl.BlockSpec((MOD_ROWS, tn), lambda j: (0, j)),
        out_shape=jax.ShapeDtypeStruct((MOD_ROWS, 6 * D), F32),
        compiler_params=_cparams(("arbitrary",)),
        name="mod",
    )(c_all, w_mod, b_mod)


def _normmod_kernel(x_ref, ctx_ref, sh_ref, sc_ref, w_ref, o_ref):
    def go(src):
        y = _rms(src, w_ref[...])
        o_ref[0] = (y * (1.0 + sc_ref[0]) + sh_ref[0]).astype(BF16)

    i = pl.program_id(1)

    @pl.when(i == 0)
    def _():
        go(ctx_ref[0])

    @pl.when(i > 0)
    def _():
        go(x_ref[0])


def _normmod_call(x, ctx, mod3, ln_w):
    row = lambda b, i: jnp.where(i == 0, B, b)
    return pl.pallas_call(
        _normmod_kernel,
        grid=(B, TILES_A),
        in_specs=[pl.BlockSpec((1, ROW_T, D), lambda b, i: (b, jnp.maximum(i - 1, 0), 0)),
                  pl.BlockSpec((1, ROW_T, D), lambda b, i: (b, 0, 0)),
                  pl.BlockSpec((1, 1, D), lambda b, i: (row(b, i), 0, 0)),
                  pl.BlockSpec((1, 1, D), lambda b, i: (row(b, i), 0, 1)),
                  pl.BlockSpec((1, D), lambda b, i: (0, 0))],
        out_specs=pl.BlockSpec((1, ROW_T, D), lambda b, i: (b, i, 0)),
        out_shape=jax.ShapeDtypeStruct((B, NA, D), BF16),
        compiler_params=_cparams(("arbitrary", "arbitrary")),
        name="normmod",
    )(x, ctx, mod3, mod3, ln_w)


def _matmul_kernel(a_ref, w_ref, o_ref):
    o_ref[...] = jnp.dot(a_ref[...], w_ref[...], preferred_element_type=F32).astype(o_ref.dtype)


def _matmul_call(a, w, out_dtype, name):
    rows, k = a.shape
    cols = w.shape[1]
    return pl.pallas_call(
        _matmul_kernel,
        grid=(rows // MM_TM, cols // MM_TN),
        in_specs=[pl.BlockSpec((MM_TM, k), lambda i, j: (i, 0)),
                  pl.BlockSpec((k, MM_TN), lambda i, j: (0, j))],
        out_specs=pl.BlockSpec((MM_TM, MM_TN), lambda i, j: (i, j)),
        out_shape=jax.ShapeDtypeStruct((rows, cols), out_dtype),
        compiler_params=_cparams(("arbitrary", "arbitrary")),
        name=name,
    )(a, w)


def _kvprep_kernel(p_ref, kvn_ref, w_ref, cs_ref, k_ref, v_ref):
    ckv = p_ref[:, Q_RANK:Q_RANK + KV_RANK].astype(F32)
    ckvn = _rms(ckv, kvn_ref[...]).astype(BF16)
    kv = jnp.dot(ckvn, w_ref[...], preferred_element_type=F32)
    a = p_ref[:, Q_RANK + KV_RANK:Q_RANK + KV_RANK + LANES].astype(F32) * cs_ref[...]
    r = a + pltpu.roll(a, ROPE, axis=1)
    lane = lax.broadcasted_iota(I32, r.shape, 1)
    r = jnp.where(lane < ROPE, r, 0.0).astype(BF16)
    for h in range(H):
        k_ref[0, h, :, 0:HW] = kv[:, h * HW:(h + 1) * HW].astype(BF16)
        k_ref[0, h, :, HW:2 * HW] = r
        v_ref[0, h] = kv[:, (H + h) * HW:(H + h + 1) * HW].astype(BF16)


def _kvprep_call(p, kv_norm, w_kv, cs_all):
    return pl.pallas_call(
        _kvprep_kernel,
        grid=(B, TILES_A),
        in_specs=[pl.BlockSpec((ROW_T, MM_TN), lambda b, i: (b * TILES_A + i, 0)),
                  pl.BlockSpec((1, KV_RANK), lambda b, i: (0, 0)),
                  pl.BlockSpec((KV_RANK, 2 * H * HW), lambda b, i: (0, 0)),
                  pl.BlockSpec((ROW_T, LANES), lambda b, i: (i, 0))],
        out_specs=[pl.BlockSpec((1, H, ROW_T, 2 * HW), lambda b, i: (b, 0, i, 0)),
                   pl.BlockSpec((1, H, ROW_T, HW), lambda b, i: (b, 0, i, 0))],
        out_shape=[jax.ShapeDtypeStruct((B, H, NA, 2 * HW), BF16),
                   jax.ShapeDtypeStruct((B, H, NA, HW), BF16)],
        compiler_params=_cparams(("arbitrary", "arbitrary")),
        name="kvprep",
    )(p, kv_norm, w_kv, cs_all)


def _qprep_kernel(p_ref, qn_ref, w_ref, cs_ref, q_ref):
    cq = p_ref[:, 0:Q_RANK].astype(F32)
    cqn = _rms(cq, qn_ref[...]).astype(BF16)
    q = jnp.dot(cqn, w_ref[...], preferred_element_type=F32)
    cs = cs_ref[...]
    for h in range(H):
        q_ref[0, h, :, 0:HW] = (q[:, 2 * h * HW:(2 * h + 1) * HW] * QK_SCALE).astype(BF16)
        a = q[:, (2 * h + 1) * HW:(2 * h + 2) * HW] * cs
        q_ref[0, h, :, HW:2 * HW] = (a + pltpu.roll(a, ROPE, axis=1)).astype(BF16)


def _qprep_call(p, q_norm, w_q, cs_q):
    return pl.pallas_call(
        _qprep_kernel,
        grid=(B, TILES_L),
        in_specs=[pl.BlockSpec((ROW_T, MM_TN), lambda b, i: (b * TILES_A + 1 + i, 0)),
                  pl.BlockSpec((1, Q_RANK), lambda b, i: (0, 0)),
                  pl.BlockSpec((Q_RANK, 2 * H * HW), lambda b, i: (0, 0)),
                  pl.BlockSpec((ROW_T, LANES), lambda b, i: (i, 0))],
        out_specs=pl.BlockSpec((1, H, ROW_T, 2 * HW), lambda b, i: (b, 0, i, 0)),
        out_shape=jax.ShapeDtypeStruct((B, H, N, 2 * HW), BF16),
        compiler_params=_cparams(("arbitrary", "arbitrary")),
        name="qprep",
    )(p, q_norm, w_q, cs_q)


ATT_TQ = 256
ATT_TILES = 16


def _attn_kernel(q_ref, k_ref, v_ref, o_ref):
    for t in range(ATT_TILES):
        rows = slice(t * ATT_TQ, (t + 1) * ATT_TQ)
        s = lax.dot_general(q_ref[0, 0, rows, :], k_ref[0, 0], (((1,), (1,)), ((), ())),
                            preferred_element_type=F32)
        m = jnp.max(s, axis=-1, keepdims=True)
        p = jnp.exp(s - m)
        l = jnp.sum(p, axis=-1, keepdims=True)
        o = jnp.dot(p.astype(BF16), v_ref[0, 0], preferred_element_type=F32)
        o_ref[rows, :] = (o * (1.0 / l)).astype(BF16)


def _attn_call(q, k, v):
    tq = ATT_TQ * ATT_TILES
    return pl.pallas_call(
        _attn_kernel,
        grid=(B, H, N // tq),
        in_specs=[pl.BlockSpec((1, 1, tq, 2 * HW), lambda b, h, i: (b, h, i, 0)),
                  pl.BlockSpec((1, 1, NA, 2 * HW), lambda b, h, i: (b, h, 0, 0)),
                  pl.BlockSpec((1, 1, NA, HW), lambda b, h, i: (b, h, 0, 0))],
        out_specs=pl.BlockSpec((tq, HW), lambda b, h, i: (b * (N // tq) + i, h)),
        out_shape=jax.ShapeDtypeStruct((T, H * HW), BF16),
        compiler_params=_cparams(("arbitrary", "arbitrary", "arbitrary")),
        name="attention",
    )(q, k, v)


def _hgrn_tables(reverse):
    c = HG_C
    t = np.arange(c)[:, None]
    u = np.arange(c)[None, :]
    cum = (u <= t).astype(np.float32)
    masks = [(u == t)]
    for m in HG_LEVELS:
        blk_t = t // m
        masks.append(((blk_t % 2) == 1) & ((u // m) == blk_t - 1))
    masks = np.stack(masks).astype(np.float32)
    if reverse:
        cum = cum[::-1, ::-1]
        masks = masks[:, ::-1, ::-1]
    return jnp.asarray(cum, dtype=BF16), jnp.asarray(masks.reshape(-1, c), dtype=F32)


def _pair_boundary(c_ref, cval, m, reverse):
    def rows_of(r_of_group):
        return jnp.concatenate([jnp.broadcast_to(c_ref[r_of_group(gi):r_of_group(gi) + 1, :], (SUBLANES, HW))
                                for gi in range(HG_C // SUBLANES)], axis=0)

    if m >= 4:
        def boundary(gi):
            pair = (SUBLANES * gi // (2 * m)) * 2 * m
            return pair + m if reverse else pair + m - 1
        return rows_of(boundary)
    row = lax.broadcasted_iota(I32, (HG_C, HW), 0)
    if m == 2:
        half = SUBLANES // 2
        edge = m if reverse else m - 1
        first = rows_of(lambda gi: SUBLANES * gi + edge)
        second = rows_of(lambda gi: SUBLANES * gi + half + edge)
        return jnp.where((row & (SUBLANES - 1)) < half, first, second)
    if reverse:
        return jnp.where((row & 1) == 0, pltpu.roll(cval, HG_C - 1, 0), cval)
    return jnp.where((row & 1) == 1, pltpu.roll(cval, 1, 0), cval)


def _hgrn_head(h, hq_ref, hi_ref, fr_ref, lb_ref, cum_ref, mk_ref, s_ref, o_ref, c_ref, reverse):
    c = HG_C
    col = pl.ds(pl.multiple_of(h * HW, HW), HW)
    xq = hq_ref[:, col].astype(F32)
    q = xq * _sigmoid(xq)
    lb = lb_ref[:, col]
    f = lb + (1.0 - lb) * _sigmoid(fr_ref[:, col])
    g = jnp.log(f)
    k = 1.0 - f
    v = hi_ref[:, col]
    g_hi = g.astype(BF16)
    g_lo = (g - g_hi.astype(F32)).astype(BF16)
    cum = cum_ref[...]
    cs = jnp.dot(jnp.concatenate([cum, cum], axis=1), jnp.concatenate([g_hi, g_lo], axis=0),
                 preferred_element_type=F32) * LOG2E
    c_ref[...] = cs
    total_row = 0 if reverse else c - 1
    c_tot = c_ref[total_row:total_row + 1, :]

    nt = (((1,), (1,)), ((), ()))
    st = s_ref[h]
    o = lax.dot_general((q * jnp.exp2(cs)).astype(BF16), st.astype(BF16), nt, preferred_element_type=F32)
    qb = q.astype(BF16)
    kb = k.astype(BF16)
    sc = lax.dot_general(qb, kb, nt, preferred_element_type=F32) * mk_ref[0:c, :]
    sign = jnp.uint32(0x80000000)
    for li, m in enumerate(HG_LEVELS):
        z = cs - _pair_boundary(c_ref, cs, m, reverse)
        w = jnp.exp2(pltpu.bitcast(pltpu.bitcast(z, jnp.uint32) | sign, F32)).astype(BF16)
        sc = sc + (lax.dot_general(qb * w, kb * w, nt, preferred_element_type=F32)
                   * mk_ref[(li + 1) * c:(li + 2) * c, :])
    o = o + jnp.dot(sc.astype(BF16), v, preferred_element_type=F32)
    o_ref[:, col] = o.astype(BF16)
    kend = (k * jnp.exp2(c_tot - cs)).astype(BF16)
    s_ref[h] = st * jnp.exp2(c_tot) + lax.dot_general(v, kend, (((0,), (0,)), ((), ())), preferred_element_type=F32)


HG_UNROLL = 8


def _hgrn_kernel(hq_f, hi_f, fr_f, hq_b, hi_b, fr_b, lbf_ref, lbb_ref, cumf_ref, mkf_ref, cumb_ref, mkb_ref,
                 of_ref, ob_ref, sf_ref, sb_ref, c_s):
    @pl.when(pl.program_id(1) == 0)
    def _():
        sf_ref[...] = jnp.zeros_like(sf_ref)
        sb_ref[...] = jnp.zeros_like(sb_ref)

    def body(i, carry):
        for hh in range(HG_UNROLL):
            h = i * HG_UNROLL + hh
            _hgrn_head(h, hq_f, hi_f, fr_f, lbf_ref, cumf_ref, mkf_ref, sf_ref, of_ref, c_s.at[2 * hh], False)
            _hgrn_head(h, hq_b, hi_b, fr_b, lbb_ref, cumb_ref, mkb_ref, sb_ref, ob_ref, c_s.at[2 * hh + 1], True)
        return carry

    lax.fori_loop(0, H // HG_UNROLL, body, 0)


def _hgrn_call(p, fraw, lb_f, lb_b):
    cum_f, mk_f = _hgrn_tables(False)
    cum_b, mk_b = _hgrn_tables(True)
    wide = H * HW

    def cf(b, s):
        return b * HG_STEPS + s

    def cb(b, s):
        return b * HG_STEPS + jnp.where(s < HG_CTX, HG_CTX - 1 - s, HG_STEPS - 1 + HG_CTX - s)

    const = lambda shape: pl.BlockSpec(shape, lambda b, s: (0, 0))
    return pl.pallas_call(
        _hgrn_kernel,
        grid=(B, HG_STEPS),
        in_specs=[pl.BlockSpec((HG_C, wide), lambda b, s: (cf(b, s), 1)),
                  pl.BlockSpec((HG_C, wide), lambda b, s: (cf(b, s), 2)),
                  pl.BlockSpec((HG_C, wide), lambda b, s: (cf(b, s), 0)),
                  pl.BlockSpec((HG_C, wide), lambda b, s: (cb(b, s), 1)),
                  pl.BlockSpec((HG_C, wide), lambda b, s: (cb(b, s), 2)),
                  pl.BlockSpec((HG_C, wide), lambda b, s: (cb(b, s), 1)),
                  const((1, wide)), const((1, wide)),
                  const(cum_f.shape), const(mk_f.shape), const(cum_b.shape), const(mk_b.shape)],
        out_specs=[pl.BlockSpec((HG_C, wide), lambda b, s: (cf(b, s), 0)),
                   pl.BlockSpec((HG_C, wide), lambda b, s: (cb(b, s), 0))],
        out_shape=[jax.ShapeDtypeStruct((TA, wide), BF16), jax.ShapeDtypeStruct((TA, wide), BF16)],
        scratch_shapes=[pltpu.VMEM((H, HW, HW), F32), pltpu.VMEM((H, HW, HW), F32),
                        pltpu.VMEM((2 * HG_UNROLL, HG_C, HW), F32)],
        compiler_params=_cparams(("arbitrary", "arbitrary")),
        name="hgrn",
    )(p, p, fraw, p, p, fraw, lb_f, lb_b, cum_f, mk_f, cum_b, mk_b)


def _pack_rows(val, dst_ref, row0, rows):
    for s in range(PACK_W):
        lo = val[:, 2 * s * LANES:(2 * s + 1) * LANES]
        hi = val[:, (2 * s + 1) * LANES:(2 * s + 2) * LANES]
        w = pltpu.pack_elementwise([lo, hi], packed_dtype=BF16)
        dst_ref[pl.ds(row0 * PACK_W + s, rows, stride=PACK_W), :] = w


def _unpack_word(w):
    lo = pltpu.unpack_elementwise(w, index=0, packed_dtype=BF16, unpacked_dtype=F32)
    hi = pltpu.unpack_elementwise(w, index=1, packed_dtype=BF16, unpacked_dtype=F32)
    return lo, hi


def _mixout_kernel(att_ref, of_ref, ob_ref, hg_ref, gn_ref, wa_ref, wr_ref, x_ref, gt_ref, shf_ref, scf_ref,
                   lnp_ref, lnf_ref, rw_ref, rb_ref,
                   x1_ref, h2w_ref, ids_ref, gates_ref, cnt_ref, carry_ref):
    first = (pl.program_id(0) == 0) & (pl.program_id(1) == 0)

    @pl.when(first)
    def _():
        carry_ref[...] = jnp.zeros_like(carry_ref)

    o = of_ref[...].astype(F32) + ob_ref[...].astype(F32)
    gn = gn_ref[...]
    recs = []
    for h in range(H):
        oh = o[:, h * HW:(h + 1) * HW]
        gate = hg_ref[:, h * HW:(h + 1) * HW].astype(F32)
        recs.append((_rms(oh, gn) * (gate * _sigmoid(gate))).astype(BF16))
    rec = jnp.concatenate(recs, axis=-1)
    mix = (jnp.dot(att_ref[...], wa_ref[...], preferred_element_type=F32)
           + jnp.dot(rec, wr_ref[...], preferred_element_type=F32))
    x1 = x_ref[0] + gt_ref[0] * _rms(mix, lnp_ref[...])
    x1_ref[0] = x1
    h2 = _rms(x1, lnf_ref[...]) * (1.0 + scf_ref[0]) + shf_ref[0]
    _pack_rows(h2, h2w_ref, 0, ROW_T)

    h_hi = h2.astype(BF16)
    h_lo = (h2 - h_hi.astype(F32)).astype(BF16)
    hh = jnp.dot(h_hi, rw_ref[...], preferred_element_type=F32)
    logits = (hh[:, :LANES] + hh[:, LANES:] + jnp.dot(h_lo, rw_ref[:, :LANES], preferred_element_type=F32)
              + rb_ref[...])
    lane = lax.broadcasted_iota(I32, logits.shape, 1)
    neg = jnp.float32(-jnp.inf)
    lg = jnp.where(lane < NE, logits, neg)
    tops, idxs = [], []
    for _ in range(TOPK):
        m = jnp.max(lg, axis=-1, keepdims=True)
        idx = jnp.min(jnp.where(lg == m, lane, LANES), axis=-1, keepdims=True)
        tops.append(m)
        idxs.append(idx)
        lg = jnp.where(lane == idx, neg, lg)
    exps = [jnp.exp(tk - tops[0]) for tk in tops]
    inv = 1.0 / (exps[0] + exps[1] + exps[2] + exps[3])
    onehot = jnp.zeros(logits.shape, F32)
    for idx in idxs:
        onehot = onehot + jnp.where(lane == idx, 1.0, 0.0)
    r = lax.broadcasted_iota(I32, (ROW_T, ROW_T), 0)
    cidx = lax.broadcasted_iota(I32, (ROW_T, ROW_T), 1)
    lower = jnp.where(cidx < r, 1.0, 0.0).astype(BF16)
    prefix = jnp.dot(lower, onehot.astype(BF16), preferred_element_type=F32) + carry_ref[...]
    ids = jnp.zeros(logits.shape, I32)
    gates = jnp.zeros(logits.shape, F32)
    for kk in range(TOPK):
        rank = jnp.sum(jnp.where(lane == idxs[kk], prefix, 0.0), axis=-1, keepdims=True).astype(I32)
        ids = jnp.where(lane == kk, idxs[kk], ids)
        ids = jnp.where(lane == TOPK + kk, rank, ids)
        gates = jnp.where(lane == kk, exps[kk] * inv, gates)
    ids_ref[...] = ids
    gates_ref[...] = gates
    carry = carry_ref[...] + jnp.sum(onehot, axis=0, keepdims=True)
    carry_ref[...] = carry
    cnt_ref[...] = carry


def _mixout_call(att, o_f, o_b, p, g_norm_t, w_a, w_r, x, mod3, ln_post, ln_ffn, r_w, r_b):
    wide = H * HW
    lat = lambda b, i: b * TILES_A + 1 + i
    tok = lambda b, i: b * TILES_L + i
    const = lambda shape: pl.BlockSpec(shape, lambda b, i: tuple(0 for _ in shape))
    return pl.pallas_call(
        _mixout_kernel,
        grid=(B, TILES_L),
        in_specs=[pl.BlockSpec((ROW_T, wide), lambda b, i: (tok(b, i), 0)),
                  pl.BlockSpec((ROW_T, wide), lambda b, i: (lat(b, i), 0)),
                  pl.BlockSpec((ROW_T, wide), lambda b, i: (lat(b, i), 0)),
                  pl.BlockSpec((ROW_T, wide), lambda b, i: (lat(b, i), 3)),
                  const((1, HW)), const((wide, D)), const((wide, D)),
                  pl.BlockSpec((1, ROW_T, D), lambda b, i: (b, i, 0)),
                  pl.BlockSpec((1, 1, D), lambda b, i: (b, 0, 2)),
                  pl.BlockSpec((1, 1, D), lambda b, i: (b, 0, 3)),
                  pl.BlockSpec((1, 1, D), lambda b, i: (b, 0, 4)),
                  const((1, D)), const((1, D)), const((D, 2 * LANES)), const((1, LANES))],
        out_specs=[pl.BlockSpec((1, ROW_T, D), lambda b, i: (b, i, 0)),
                   pl.BlockSpec((ROW_T * PACK_W, LANES), lambda b, i: (tok(b, i), 0)),
                   pl.BlockSpec((ROW_T, LANES), lambda b, i: (tok(b, i), 0)),
                   pl.BlockSpec((ROW_T, LANES), lambda b, i: (tok(b, i), 0)),
                   const((1, LANES))],
        out_shape=[jax.ShapeDtypeStruct((B, N, D), F32),
                   jax.ShapeDtypeStruct((T * PACK_W, LANES), I32),
                   jax.ShapeDtypeStruct((T, LANES), I32),
                   jax.ShapeDtypeStruct((T, LANES), F32),
                   jax.ShapeDtypeStruct((1, LANES), F32)],
        scratch_shapes=[pltpu.VMEM((1, LANES), F32)],
        compiler_params=_cparams(("arbitrary", "arbitrary")),
        name="mixout_router",
    )(att, o_f, o_b, p, g_norm_t, w_a, w_r, x, mod3, mod3, mod3, ln_post, ln_ffn, r_w, r_b)


def _dispatch_copy(src_vmem, dst_hbm, sem, src_row, dst_row, rows):
    return pltpu.make_async_copy(src_vmem.at[pl.ds(src_row * PACK_W, rows * PACK_W)],
                                 dst_hbm.at[pl.ds(dst_row * PACK_W, rows * PACK_W)], sem)


def _zero_copy(zeros_s, dst_hbm, sem, dst_row, rows):
    return pltpu.make_async_copy(zeros_s.at[pl.ds(0, rows * PACK_W)],
                                 dst_hbm.at[pl.ds(dst_row * PACK_W, rows * PACK_W)], sem)


def _dispatch_kernel(dest_ref, pad_ref, h2w_ref, xs_hbm, zeros_s, sem, zsem):
    step = pl.program_id(0)
    base = step * DISP_T

    def issue(t, carry):
        for kk in range(TOPK):
            _dispatch_copy(h2w_ref, xs_hbm, sem, t, dest_ref[(base + t) * TOPK + kk], 1).start(priority=kk % 2)
        return carry

    tail0 = pad_ref[2 * NE]
    tail_n = pad_ref[2 * NE + 1]

    def zero_rows(wait):
        def expert_pad(e, carry):
            @pl.when(pad_ref[NE + e] > 0)
            def _():
                cp = _zero_copy(zeros_s, xs_hbm, zsem, (pad_ref[e] // SUB) * SUB, SUB)
                cp.wait() if wait else cp.start()
            return carry

        def tail(i, c):
            cp = _zero_copy(zeros_s, xs_hbm, zsem, (tail0 + i) * SUB, SUB)
            cp.wait() if wait else cp.start()
            return c

        lax.fori_loop(0, NE, expert_pad, 0)
        lax.fori_loop(0, tail_n, tail, 0)

    @pl.when(step == 0)
    def _():
        zeros_s[...] = jnp.zeros_like(zeros_s)
        zero_rows(wait=False)
        zero_rows(wait=True)

    lax.fori_loop(0, DISP_T, issue, 0)

    for _ in range(TOPK):
        _dispatch_copy(h2w_ref, xs_hbm, sem, 0, 0, DISP_T).wait()


def _dispatch_call(dest, pad, h2w):
    gs = pltpu.PrefetchScalarGridSpec(
        num_scalar_prefetch=2, grid=(T // DISP_T,),
        in_specs=[pl.BlockSpec((DISP_T * PACK_W, LANES), lambda i, d, z: (i, 0))],
        out_specs=pl.BlockSpec(memory_space=pl.ANY),
        scratch_shapes=[pltpu.VMEM((SUB * PACK_W, LANES), I32),
                        pltpu.SemaphoreType.DMA(()),
                        pltpu.SemaphoreType.DMA(())])
    return pl.pallas_call(
        _dispatch_kernel, grid_spec=gs,
        out_shape=jax.ShapeDtypeStruct((N_SUBS * SUB * PACK_W, LANES), I32),
        compiler_params=_cparams(("arbitrary",)),
        name="moe_dispatch",
    )(dest, pad, h2w)


def _sub_in_copy(xs_hbm, stage, sems, blk, slot):
    return pltpu.make_async_copy(xs_hbm.at[pl.ds(blk * (SUB * PACK_W), SUB * PACK_W)], stage.at[slot], sems.at[slot])


def _sub_out_copy(stage, out_hbm, sems, blk, slot):
    return pltpu.make_async_copy(stage.at[slot], out_hbm.at[pl.ds(blk * (SUB * PACK_W), SUB * PACK_W)], sems.at[slot])


def _for_row_chunks(nblk, fn):
    pair = 2 * CHUNK_SUBS

    @pl.when(nblk >= pair)
    def _():
        fn(0, CHUNK_SUBS * SUB)
        fn(CHUNK_SUBS * SUB, CHUNK_SUBS * SUB)

    for c in range(-(-SB_SUBS // CHUNK_SUBS)):
        row0 = c * CHUNK_SUBS * SUB
        most = min(CHUNK_SUBS, SB_SUBS - c * CHUNK_SUBS)
        for r in range(1, most + 1):
            full = r == CHUNK_SUBS
            cond = (nblk >= c * CHUNK_SUBS + r) if full else (nblk == c * CHUNK_SUBS + r)
            if c < 2:
                cond = cond & (nblk < pair)
            pl.when(cond)(functools.partial(fn, row0, r * SUB))


def _wtile_copy(w_hbm, buf, sems, e, col_tile, slot, which):
    return pltpu.make_async_copy(w_hbm.at[e, :, pl.ds(pl.multiple_of(col_tile * FF_T, FF_T), FF_T)],
                                 buf.at[slot, which], sems.at[slot, which])


def _expert_kernel(sbe_ref, sbn_ref, sbb_ref, xs_hbm, wgu_hbm, bgu_ref, wdn_hbm, bdn_ref,
                   out_hbm, x_s, act_s, wbuf, stage_in, stage_out,
                   sem_w, sem_in, sem_out):
    s = pl.program_id(0)
    nblk = sbn_ref[s]
    blk0 = sbb_ref[s]
    e = sbe_ref[s]

    def gate_tiles(j, slot):
        return (_wtile_copy(wgu_hbm, wbuf, sem_w, e, j, slot, 0),
                _wtile_copy(wgu_hbm, wbuf, sem_w, e, N1 + j, slot, 1))

    def down_tile(j, slot):
        return _wtile_copy(wdn_hbm, wbuf, sem_w, e, j, slot, 0)

    @pl.when(nblk > 0)
    def _():
        for cp in gate_tiles(0, 0):
            cp.start()

        _sub_in_copy(xs_hbm, stage_in, sem_in, blk0, 0).start()
        for i in range(SB_SUBS):
            @pl.when(i < nblk)
            def _():
                if i + 1 < SB_SUBS:
                    @pl.when(i + 1 < nblk)
                    def _():
                        _sub_in_copy(xs_hbm, stage_in, sem_in, blk0 + i + 1, (i + 1) % 2).start()
                _sub_in_copy(xs_hbm, stage_in, sem_in, blk0 + i, i % 2).wait()
                for w in range(PACK_W):
                    lo, hi = _unpack_word(stage_in[i % 2, pl.ds(w, SUB, stride=PACK_W), :])
                    x_s[i * SUB:(i + 1) * SUB, 2 * w * LANES:(2 * w + 1) * LANES] = lo.astype(BF16)
                    x_s[i * SUB:(i + 1) * SUB, (2 * w + 1) * LANES:(2 * w + 2) * LANES] = hi.astype(BF16)

        def hidden_step(j, carry):
            slot = j % 2
            for cp in gate_tiles(j, slot):
                cp.wait()

            @pl.when(j + 1 < N1)
            def _():
                for cp in gate_tiles(j + 1, 1 - slot):
                    cp.start()

            @pl.when(j + 1 == N1)
            def _():
                down_tile(0, 1 - slot).start()

            col = pl.ds(pl.multiple_of(j * FF_T, FF_T), FF_T)
            bg = bgu_ref[0, :, col]
            bl = bgu_ref[0, :, pl.ds(pl.multiple_of(DFF + j * FF_T, FF_T), FF_T)]

            def gate_up(row0, rows):
                x = x_s[row0:row0 + rows, :]
                gu = jnp.dot(x, wbuf[slot, 0].astype(BF16), preferred_element_type=F32) + bg
                li = jnp.dot(x, wbuf[slot, 1].astype(BF16), preferred_element_type=F32) + bl
                glu = jnp.minimum(gu, LIMIT)
                lin = jnp.clip(li, -LIMIT, LIMIT)
                act = glu * _sigmoid(ALPHA * glu) * (lin + 1.0)
                act_s[row0:row0 + rows, col] = act.astype(BF16)

            _for_row_chunks(nblk, gate_up)
            return carry

        lax.fori_loop(0, N1, hidden_step, 0)

        def out_step(j2, carry):
            slot = (N1 + j2) % 2
            down_tile(j2, slot).wait()

            @pl.when(j2 + 1 < N2)
            def _():
                down_tile(j2 + 1, 1 - slot).start()

            bd = bdn_ref[0, :, pl.ds(pl.multiple_of(j2 * FF_T, FF_T), FF_T)]

            def down(row0, rows):
                acc = jnp.dot(act_s[row0:row0 + rows, :], wbuf[slot, 0].astype(BF16), preferred_element_type=F32) + bd
                x_s[row0:row0 + rows, pl.ds(pl.multiple_of(j2 * FF_T, FF_T), FF_T)] = acc.astype(BF16)

            _for_row_chunks(nblk, down)
            return carry

        lax.fori_loop(0, N2, out_step, 0)

        for i in range(SB_SUBS):
            @pl.when(i < nblk)
            def _():
                if i >= 2:
                    _sub_out_copy(stage_out, out_hbm, sem_out, blk0 + i - 2, i % 2).wait()
                _pack_rows(x_s[i * SUB:(i + 1) * SUB, :].astype(F32), stage_out.at[i % 2], 0, SUB)
                _sub_out_copy(stage_out, out_hbm, sem_out, blk0 + i, i % 2).start()
        for i in range(SB_SUBS):
            @pl.when((i < nblk) & (i + 2 >= nblk))
            def _():
                _sub_out_copy(stage_out, out_hbm, sem_out, blk0 + i, i % 2).wait()


def _expert_call(sb_e, sb_n, sb_b, xs, w_gu, b_gu, w_dn, b_dn):
    gs = pltpu.PrefetchScalarGridSpec(
        num_scalar_prefetch=3, grid=(S_MAX,),
        in_specs=[pl.BlockSpec(memory_space=pl.ANY),
                  pl.BlockSpec(memory_space=pl.ANY),
                  pl.BlockSpec((1, 1, 2 * DFF), lambda s, e, n, b: (e[s], 0, 0)),
                  pl.BlockSpec(memory_space=pl.ANY),
                  pl.BlockSpec((1, 1, D), lambda s, e, n, b: (e[s], 0, 0))],
        out_specs=pl.BlockSpec(memory_space=pl.ANY),
        scratch_shapes=[pltpu.VMEM((SB_R, D), BF16),
                        pltpu.VMEM((SB_R, DFF), BF16),
                        pltpu.VMEM((2, 2, D, FF_T), F32),
                        pltpu.VMEM((2, SUB * PACK_W, LANES), I32),
                        pltpu.VMEM((2, SUB * PACK_W, LANES), I32),
                        pltpu.SemaphoreType.DMA((2, 2)),
                        pltpu.SemaphoreType.DMA((2,)),
                        pltpu.SemaphoreType.DMA((2,))])
    return pl.pallas_call(
        _expert_kernel, grid_spec=gs,
        out_shape=jax.ShapeDtypeStruct(xs.shape, xs.dtype),
        input_output_aliases={3: 0},
        compiler_params=_cparams(("arbitrary",)),
        name="moe_experts",
    )(sb_e, sb_n, sb_b, xs, w_gu, b_gu, w_dn, b_dn)


def _combine_copy(src_hbm, buf, sem, src_row, dst_row, rows):
    return pltpu.make_async_copy(src_hbm.at[pl.ds(src_row * PACK_W, rows * PACK_W)],
                                 buf.at[pl.ds(dst_row * PACK_W, rows * PACK_W)], sem)


def _combine_kernel(dest_ref, ys_hbm, gates_ref, x1_ref, gt_ref, ln_ref, o_ref, bufs, y_s, sems):
    step = pl.program_id(0) * TILES_L + pl.program_id(1)

    def issue(tile, slot):
        def body(tg, carry):
            for tt in range(COMBINE_UNROLL):
                t = tg * COMBINE_UNROLL + tt
                for kk in range(TOPK):
                    _combine_copy(ys_hbm, bufs.at[slot], sems.at[slot], dest_ref[(tile * ROW_T + t) * TOPK + kk],
                                  kk * ROW_T + t, 1).start(priority=kk % 2)
            return carry
        lax.fori_loop(0, ROW_T // COMBINE_UNROLL, body, 0)

    @pl.when(step == 0)
    def _():
        issue(0, 0)

    @pl.when(step + 1 < B * TILES_L)
    def _():
        issue(step + 1, (step + 1) % 2)

    slot = step % 2
    buf = bufs.at[slot]
    _combine_copy(ys_hbm, buf, sems.at[slot], 0, 0, ROW_T * TOPK).wait()

    gates = gates_ref[...]
    gk = [jnp.broadcast_to(gates[:, kk:kk + 1], (ROW_T, LANES)) for kk in range(TOPK)]
    for w in range(PACK_W):
        acc_lo = jnp.zeros((ROW_T, LANES), F32)
        acc_hi = jnp.zeros((ROW_T, LANES), F32)
        for kk in range(TOPK):
            lo, hi = _unpack_word(buf[pl.ds(kk * ROW_T * PACK_W + w, ROW_T, stride=PACK_W), :])
            acc_lo = acc_lo + gk[kk] * lo
            acc_hi = acc_hi + gk[kk] * hi
        y_s[:, 2 * w * LANES:(2 * w + 1) * LANES] = acc_lo
        y_s[:, (2 * w + 1) * LANES:(2 * w + 2) * LANES] = acc_hi
    o_ref[0] = x1_ref[0] + gt_ref[0] * _rms(y_s[...], ln_ref[...])


def _combine_call(dest, ys, gates, x1, mod3, ln_post):
    tok = lambda b, i: b * TILES_L + i
    gs = pltpu.PrefetchScalarGridSpec(
        num_scalar_prefetch=1, grid=(B, TILES_L),
        in_specs=[pl.BlockSpec(memory_space=pl.ANY),
                  pl.BlockSpec((ROW_T, LANES), lambda b, i, d: (tok(b, i), 0)),
                  pl.BlockSpec((1, ROW_T, D), lambda b, i, d: (b, i, 0)),
                  pl.BlockSpec((1, 1, D), lambda b, i, d: (b, 0, 5)),
                  pl.BlockSpec((1, D), lambda b, i, d: (0, 0))],
        out_specs=pl.BlockSpec((1, ROW_T, D), lambda b, i, d: (b, i, 0)),
        scratch_shapes=[pltpu.VMEM((2, TOPK * ROW_T * PACK_W, LANES), I32),
                        pltpu.VMEM((ROW_T, D), F32),
                        pltpu.SemaphoreType.DMA((2,))])
    return pl.pallas_call(
        _combine_kernel, grid_spec=gs,
        out_shape=jax.ShapeDtypeStruct((B, N, D), F32),
        compiler_params=_cparams(("arbitrary", "arbitrary")),
        name="moe_combine",
    )(dest, ys, gates, x1, mod3, ln_post)


def _rot_cols(w):
    half = ROPE // 2
    return jnp.concatenate([-w[..., half:], w[..., :half]], axis=-1)


def _rope_tables():
    t = np.arange(N)
    r = (t // GRID_W).astype(np.float64)
    cl = (t % GRID_W).astype(np.float64)
    n_freq = ROPE // 4
    inv = ROPE_THETA ** (-np.arange(n_freq, dtype=np.float64) / n_freq)
    ang = np.concatenate([r[:, None] * inv, cl[:, None] * inv], axis=-1)
    cos, sin = np.cos(ang), np.sin(ang)
    cs = np.concatenate([cos, cos, sin, sin], axis=-1).astype(np.float32)
    ctx = np.concatenate([np.ones((M, ROPE), np.float32), np.zeros((M, ROPE), np.float32)], axis=-1)
    return jnp.asarray(np.concatenate([ctx, cs], axis=0)), jnp.asarray(cs * QK_SCALE)


def _routing_tables(ids, counts_f):
    e = ids[:, :TOPK]
    rank = ids[:, TOPK:2 * TOPK]
    counts = counts_f[0, :NE].astype(I32)
    nsub = (counts + SUB - 1) // SUB
    sub_start = jnp.cumsum(nsub) - nsub
    onehot = e[..., None] == jnp.arange(NE, dtype=I32)
    dest = jnp.sum(jnp.where(onehot, sub_start * SUB, 0), axis=-1) + rank
    nsb = (nsub + SB_SUBS - 1) // SB_SUBS
    sb_end = jnp.cumsum(nsb)
    sb_start = sb_end - nsb
    n_active = sb_end[-1]
    s_idx = jnp.arange(S_MAX, dtype=I32)
    s_eff = jnp.minimum(s_idx, n_active - 1)
    sb_e = jnp.minimum(jnp.searchsorted(sb_end, s_eff, side="right"), NE - 1).astype(I32)
    local = s_eff - sb_start[sb_e]
    sb_b = (sub_start[sb_e] + local * SB_SUBS).astype(I32)
    sb_n = jnp.where(s_idx < n_active, jnp.clip(nsub[sb_e] - local * SB_SUBS, 0, SB_SUBS), 0).astype(I32)
    pad = jnp.concatenate([sub_start * SUB + counts, nsub * SUB - counts,
                           jnp.stack([jnp.sum(nsub), N_SUBS - jnp.sum(nsub)])]).astype(I32)
    return dest.reshape(-1).astype(I32), pad, sb_e, sb_n, sb_b


def kernel(x, c, ctx, c_ctx, w_mod, b_mod, ln_mix_pre, ln_mix_post, ln_ffn_pre, ln_ffn_post, w_in, q_norm, kv_norm,
           w_uq, w_ukv, hgrn_lb, g_norm, w_out, router_w, router_b, w_gate_up, b_gate_up, w_down, b_down):
    c_all = jnp.concatenate([c, c_ctx[None, :], jnp.zeros((MOD_ROWS - B - 1, D), F32)], axis=0)
    mod = _mod_call(c_all, w_mod[0], b_mod)
    mod3 = mod.reshape(MOD_ROWS, 1, 6 * D)

    w = w_in[0]
    o_q, o_kv, o_kr = 0, Q_RANK, Q_RANK + KV_RANK
    o_hq = o_kr + ROPE
    wide = H * HW
    o_ff, o_fb, o_hi, o_hg = o_hq + wide, o_hq + 2 * wide, o_hq + 3 * wide, o_hq + 4 * wide
    w_kr = w[:, o_kr:o_kr + ROPE]
    w_p = jnp.concatenate([w[:, o_q:o_kr], w_kr, _rot_cols(w_kr), jnp.zeros((D, LANES), F32),
                           w[:, o_hq:o_hq + wide], w[:, o_hi:o_hi + wide], w[:, o_hg:o_hg + wide]],
                          axis=1).astype(BF16)
    w_f = w[:, o_ff:o_ff + 2 * wide].astype(BF16)
    wq = w_uq[0]
    wq_r = wq[..., HW:]
    w_q = jnp.concatenate([wq[..., :HW], wq_r, _rot_cols(wq_r)], axis=-1).reshape(Q_RANK, 2 * wide).astype(BF16)
    wkv = w_ukv[0]
    w_kv = jnp.concatenate([wkv[..., :HW].reshape(KV_RANK, wide), wkv[..., HW:].reshape(KV_RANK, wide)],
                           axis=1).astype(BF16)
    w_o = w_out[0].astype(BF16)
    cs_all, cs_q = _rope_tables()
    lb = jnp.cumsum(jax.nn.softmax(hgrn_lb.astype(F32), axis=1), axis=1)
    r_w = jnp.concatenate([router_w[0], jnp.zeros((D, LANES - NE), F32)], axis=1)
    r_w_hi = r_w.astype(BF16)
    r_w = jnp.concatenate([r_w_hi, (r_w - r_w_hi.astype(F32)).astype(BF16)], axis=1)
    r_b = jnp.concatenate([router_b[0], jnp.zeros((LANES - NE,), F32)])[None, :]

    h_all = _normmod_call(x, ctx, mod3, ln_mix_pre)
    h_flat = h_all.reshape(TA, D)
    p = _matmul_call(h_flat, w_p, BF16, "in_proj")
    fraw = _matmul_call(h_flat, w_f, F32, "in_proj_forget")
    k_all, v_all = _kvprep_call(p, kv_norm, w_kv, cs_all)
    q_all = _qprep_call(p, q_norm, w_q, cs_q)
    att = _attn_call(q_all, k_all, v_all)
    o_f, o_b = _hgrn_call(p, fraw, lb[0, 0][None, :], lb[1, 0][None, :])

    g_norm_t = g_norm
    x1, h2w, ids, gates, counts = _mixout_call(att, o_f, o_b, p, g_norm_t, w_o[:wide], w_o[wide:], x, mod3,
                                               ln_mix_post, ln_ffn_pre, r_w, r_b)

    dest, pad, sb_e, sb_n, sb_b = _routing_tables(ids, counts)
    xs = _dispatch_call(dest, pad, h2w)
    ys = _expert_call(sb_e, sb_n, sb_b, xs, w_gate_up[0], b_gate_up[0][:, None, :], w_down[0], b_down[0][:, None, :])
    return _combine_call(dest, ys, gates, x1, mod3, ln_ffn_post)
```
